```python
import math
import jax
import jax.numpy as jnp
from jax import lax
import numpy as np

D_MODEL = 1024
BATCH = 8
SEQ = 2048
DEPTH = 2

CTX_LEN = 256
GRID_W = 64
MIX_WIDTH = D_MODEL
LRU_WIDTH = MIX_WIDTH // 2
LRU_BLOCK = 64
LRU_HEADS = LRU_WIDTH // LRU_BLOCK
LRU_C = 8.0
SSD_INNER = MIX_WIDTH - LRU_WIDTH
SSD_HEAD_DIM = 64
SSD_HEADS = SSD_INNER // SSD_HEAD_DIM
SSD_GROUPS = 2
SSD_HPG = SSD_HEADS // SSD_GROUPS
SSD_STATE = 128
SSD_CHUNK = 128
SSD_XBC = SSD_INNER + 2 * SSD_GROUPS * SSD_STATE
SSD_DT = 2 * SSD_HEADS
CONV_K = 4
CONV_LEFT = 1
D_FF = 4 * D_MODEL
N_SCAN_COLS = LRU_WIDTH + SSD_XBC + SSD_DT
N_IN_COLS = N_SCAN_COLS + LRU_WIDTH + SSD_INNER
N_MOD = 6
EPS = 1e-6

kernel_name = 'hybrid_rglru_ssd_prefix_dit_block'


def _rmsnorm(x, g):
    xf = x.astype(jnp.float32)
    y = xf * lax.rsqrt(jnp.mean(xf * xf, axis=-1, keepdims=True) + EPS)
    return (y * g.astype(jnp.float32)).astype(x.dtype)


def _modulate(h, shift, scale):
    return h * (1.0 + scale) + shift


def _sq_relu_mlp(h, w1, w2):
    return jnp.square(jax.nn.relu(h @ w1)) @ w2


def _dwconv(u, w, b):
    y = lax.conv_general_dilated(
        u, w[:, None, :].astype(u.dtype), window_strides=(1,),
        padding=[(CONV_LEFT, CONV_K - 1 - CONV_LEFT)],
        dimension_numbers=('NWC', 'WIO', 'NWC'), feature_group_count=u.shape[-1])
    return y + b.astype(u.dtype)


def _to_col_major(t, rows):
    b, l, ch = t.shape
    return t.reshape(b, rows, GRID_W, ch).swapaxes(1, 2).reshape(b, l, ch)


def _from_col_major(t, rows):
    b, l, ch = t.shape
    return t.reshape(b, GRID_W, rows, ch).swapaxes(1, 2).reshape(b, l, ch)


def _flip(t, d):
    return jnp.flip(t, axis=1) if d else t


def _compose(left, right):
    return left[0] * right[0], right[0] * left[1] + right[1]


def _linear_scan(a, b, h0):
    a_cum, h = lax.associative_scan(_compose, (a, b), axis=1)
    if h0 is None:
        return h
    return h + a_cum * h0[:, None]


def _lru_coeffs(u, wa, ba, wx, bx, lam):
    bsz, ln, _ = u.shape
    uf = u.astype(jnp.float32)
    ub = uf.reshape(bsz, ln, LRU_HEADS, LRU_BLOCK)
    r = jax.nn.sigmoid(jnp.einsum('blhi,hij->blhj', ub, wa.astype(jnp.float32)).reshape(bsz, ln, LRU_WIDTH) + ba.astype(jnp.float32))
    i = jax.nn.sigmoid(jnp.einsum('blhi,hij->blhj', ub, wx.astype(jnp.float32)).reshape(bsz, ln, LRU_WIDTH) + bx.astype(jnp.float32))
    log_a = -LRU_C * r * jax.nn.softplus(-lam.astype(jnp.float32))
    return jnp.exp(log_a), jnp.sqrt(-jnp.expm1(2.0 * log_a)) * (i * uf)


def _rglru_bidir(u_ctx, u_lat, wa, ba, wx, bx, lam, need_ctx_out):
    ys_ctx, ys_lat = [], []
    for d in range(2):
        a, b = _lru_coeffs(_flip(u_ctx, d), wa[d], ba[d], wx[d], bx[d], lam[d])
        h_ctx = _linear_scan(a, b, None)
        a, b = _lru_coeffs(_flip(u_lat, d), wa[d], ba[d], wx[d], bx[d], lam[d])
        h_lat = _linear_scan(a, b, h_ctx[:, -1])
        ys_lat.append(_flip(h_lat, d))
        if need_ctx_out:
            ys_ctx.append(_flip(h_ctx, d))
    y_ctx = ys_ctx[0] + ys_ctx[1] if need_ctx_out else None
    return y_ctx, ys_lat[0] + ys_lat[1]


def _ssd_chunked(x, log_a, bm, cm, h0, want_y, want_final):
    bsz, ln = x.shape[0], x.shape[1]
    nc = ln // SSD_CHUNK
    X = x.reshape(bsz, nc, SSD_CHUNK, SSD_GROUPS, SSD_HPG, SSD_HEAD_DIM)
    A = log_a.reshape(bsz, nc, SSD_CHUNK, SSD_GROUPS, SSD_HPG)
    Bc = bm.reshape(bsz, nc, SSD_CHUNK, SSD_GROUPS, SSD_STATE)
    Cc = cm.reshape(bsz, nc, SSD_CHUNK, SSD_GROUPS, SSD_STATE)
    a_cs = jnp.cumsum(A, axis=2)
    a_last = a_cs[:, :, -1]
    states = jnp.einsum('bclgn,bclge,bclgep->bcgepn', Bc, jnp.exp(a_last[:, :, None] - a_cs), X)
    chunk_cum = jnp.cumsum(jnp.pad(a_last, ((0, 0), (1, 0), (0, 0), (0, 0))), axis=1)
    row_idx = np.arange(0 if want_y else nc, nc + 1 if want_final else nc)
    if h0 is None:
        col_idx = np.arange(1, nc + 1)
        states_all = states
    else:
        col_idx = np.arange(0, nc + 1)
        h0g = h0.reshape(bsz, SSD_GROUPS, SSD_HPG, SSD_HEAD_DIM, SSD_STATE)
        states_all = jnp.concatenate([h0g[:, None], states], axis=1)
    seg = chunk_cum[:, row_idx][:, :, None] - chunk_cum[:, col_idx][:, None, :]
    cmask = (row_idx[:, None] >= col_idx[None, :])[None, :, :, None, None]
    decay_chunk = jnp.exp(jnp.where(cmask, seg, -jnp.inf))
    new_states = jnp.einsum('bzkge,bkgepn->bzgepn', decay_chunk, states_all)
    final = new_states[:, -1].reshape(bsz, SSD_HEADS, SSD_HEAD_DIM, SSD_STATE) if want_final else None
    y = None
    if want_y:
        prev = new_states[:, :nc]
        seg_in = a_cs[:, :, :, None] - a_cs[:, :, None, :]
        tri = np.tril(np.ones((SSD_CHUNK, SSD_CHUNK), dtype=bool))[None, None, :, :, None, None]
        l_mat = jnp.exp(jnp.where(tri, seg_in, -jnp.inf))
        cb = jnp.einsum('bclgn,bcsgn->bclsg', Cc, Bc)
        y_diag = jnp.einsum('bclsg,bclsge,bcsgep->bclgep', cb, l_mat, X)
        y_off = jnp.einsum('bclgn,bcgepn,bclge->bclgep', Cc, prev, jnp.exp(a_cs))
        y = (y_diag + y_off).reshape(bsz, ln, SSD_HEADS, SSD_HEAD_DIM)
    return y, final


def _ssd_unpack(xbc, dtr):
    bsz, ln, _ = xbc.shape
    xs, bm, cm = jnp.split(xbc.astype(jnp.float32), [SSD_INNER, SSD_INNER + SSD_GROUPS * SSD_STATE], axis=-1)
    return (xs.reshape(bsz, ln, SSD_HEADS, SSD_HEAD_DIM),
            bm.reshape(bsz, ln, SSD_GROUPS, SSD_STATE),
            cm.reshape(bsz, ln, SSD_GROUPS, SSD_STATE),
            dtr.astype(jnp.float32).reshape(bsz, ln, 2, SSD_HEADS))


def _ssd_bidir(xbc_c, dt_c, xbc_l, dt_l, dt_bias, a_log, d_skip, need_ctx_out):
    xs_c, b_c, cm_c, dtr_c = _ssd_unpack(xbc_c, dt_c)
    xs_l, b_l, cm_l, dtr_l = _ssd_unpack(xbc_l, dt_l)

    def drive(xs, dtr, d):
        dt = jax.nn.softplus(dtr[:, :, d] + dt_bias[d].astype(jnp.float32))
        return xs * dt[..., None], dt * (-jnp.exp(a_log[d].astype(jnp.float32)))

    ys_ctx, ys_lat = [], []
    for d in range(2):
        xc, ac = drive(xs_c, dtr_c, d)
        yc, hc = _ssd_chunked(_flip(xc, d), _flip(ac, d), _flip(b_c, d), _flip(cm_c, d), None, need_ctx_out, True)
        xl, al = drive(xs_l, dtr_l, d)
        yl, _ = _ssd_chunked(_flip(xl, d), _flip(al, d), _flip(b_l, d), _flip(cm_l, d), hc, True, False)
        ys_lat.append(_flip(yl, d))
        if need_ctx_out:
            ys_ctx.append(_flip(yc, d))
    skip = d_skip.astype(jnp.float32)[:, None]
    y_lat = ys_lat[0] + ys_lat[1] + skip * xs_l
    y_ctx = ys_ctx[0] + ys_ctx[1] + skip * xs_c if need_ctx_out else None
    return y_ctx, y_lat


def _gated_group_rmsnorm(y, z, g):
    yf = y * jax.nn.silu(z.astype(jnp.float32))
    b, l, w = yf.shape
    yg = yf.reshape(b, l, SSD_GROUPS, w // SSD_GROUPS)
    yg = yg * lax.rsqrt(jnp.mean(yg * yg, axis=-1, keepdims=True) + EPS)
    return yg.reshape(b, l, w) * g.astype(jnp.float32)


def _mixer(h_lat, h_ctx, w_in, lru_conv_w, lru_conv_b, lru_wa, lru_ba, lru_wx, lru_bx, lru_lambda,
           ssd_conv_w, ssd_conv_b, ssd_dt_bias, ssd_a_log, ssd_d, ssd_norm_g, w_out, need_ctx_out):
    bsz, n_lat, _ = h_lat.shape
    rows = n_lat // GRID_W
    splits = [LRU_WIDTH, LRU_WIDTH + SSD_XBC, N_SCAN_COLS, N_SCAN_COLS + LRU_WIDTH]
    lx_l, xbc_l, dt_l, lg_l, z_l = jnp.split(h_lat @ w_in, splits, axis=-1)
    if need_ctx_out:
        lx_c, xbc_c, dt_c, lg_c, z_c = jnp.split(h_ctx @ w_in, splits, axis=-1)
    else:
        lx_c, xbc_c, dt_c = jnp.split(h_ctx @ w_in[:, :N_SCAN_COLS], splits[:2], axis=-1)

    lru_c, lru_l = _rglru_bidir(_dwconv(lx_c, lru_conv_w, lru_conv_b), _dwconv(lx_l, lru_conv_w, lru_conv_b),
                                lru_wa, lru_ba, lru_wx, lru_bx, lru_lambda, need_ctx_out)

    xbc_l = jax.nn.silu(_dwconv(_to_col_major(xbc_l, rows), ssd_conv_w, ssd_conv_b))
    xbc_c = jax.nn.silu(_dwconv(xbc_c, ssd_conv_w, ssd_conv_b))
    ssd_c, ssd_l = _ssd_bidir(xbc_c, dt_c, xbc_l, _to_col_major(dt_l, rows),
                              ssd_dt_bias, ssd_a_log, ssd_d, need_ctx_out)
    ssd_l = _from_col_major(ssd_l.reshape(bsz, n_lat, SSD_INNER), rows)

    cat_l = jnp.concatenate([lru_l * jax.nn.gelu(lg_l.astype(jnp.float32)),
                             _gated_group_rmsnorm(ssd_l, z_l, ssd_norm_g)], axis=-1)
    out_l = cat_l.astype(h_lat.dtype) @ w_out
    out_c = None
    if need_ctx_out:
        ssd_c = ssd_c.reshape(bsz, h_ctx.shape[1], SSD_INNER)
        cat_c = jnp.concatenate([lru_c * jax.nn.gelu(lg_c.astype(jnp.float32)),
                                 _gated_group_rmsnorm(ssd_c, z_c, ssd_norm_g)], axis=-1)
        out_c = cat_c.astype(h_ctx.dtype) @ w_out
    return out_l, out_c


def setup_inputs(seed: int = 0) -> dict:
    key = jax.random.key(seed)
    ks = jax.random.split(key, 32)

    def nrm(k, shape, scale):
        return jax.random.normal(k, shape, jnp.float32) * scale

    x = nrm(ks[0], (BATCH, SEQ, D_MODEL), 1.0)
    c = nrm(ks[1], (BATCH, D_MODEL), 1.0)
    ctx = nrm(ks[2], (BATCH, CTX_LEN, D_MODEL), 1.0)
    c_ctx = nrm(ks[3], (D_MODEL,), 1.0)
    ada_w = nrm(ks[4], (DEPTH, D_MODEL, N_MOD * D_MODEL), 0.5 * D_MODEL ** -0.5)
    ada_b = nrm(ks[5], (DEPTH, N_MOD * D_MODEL), 0.01)
    norm1_g = 1.0 + nrm(ks[6], (DEPTH, D_MODEL), 0.1)
    norm2_g = 1.0 + nrm(ks[7], (DEPTH, D_MODEL), 0.1)
    w_in = nrm(ks[8], (DEPTH, D_MODEL, N_IN_COLS), D_MODEL ** -0.5)
    lru_conv_w = nrm(ks[9], (DEPTH, CONV_K, LRU_WIDTH), CONV_K ** -0.5)
    lru_conv_b = nrm(ks[10], (DEPTH, LRU_WIDTH), 0.01)
    lru_wa = nrm(ks[11], (DEPTH, 2, LRU_HEADS, LRU_BLOCK, LRU_BLOCK), LRU_BLOCK ** -0.5)
    lru_ba = nrm(ks[12], (DEPTH, 2, LRU_WIDTH), 0.01)
    lru_wx = nrm(ks[13], (DEPTH, 2, LRU_HEADS, LRU_BLOCK, LRU_BLOCK), LRU_BLOCK ** -0.5)
    lru_bx = nrm(ks[14], (DEPTH, 2, LRU_WIDTH), 0.01)
    a_pow_c = jax.random.uniform(ks[15], (DEPTH, 2, LRU_WIDTH), jnp.float32, 0.9, 0.999)
    log_a = jnp.log(a_pow_c) / LRU_C
    lru_lambda = log_a - jnp.log(-jnp.expm1(log_a))
    ssd_conv_w = nrm(ks[16], (DEPTH, CONV_K, SSD_XBC), CONV_K ** -0.5)
    ssd_conv_b = nrm(ks[17], (DEPTH, SSD_XBC), 0.01)
    dt0 = jnp.exp(jax.random.uniform(ks[18], (DEPTH, 2, SSD_HEADS), jnp.float32, math.log(1e-3), math.log(1e-1)))
    ssd_dt_bias = dt0 + jnp.log(-jnp.expm1(-dt0))
    ssd_a_log = jnp.log(jax.random.uniform(ks[19], (DEPTH, 2, SSD_HEADS), jnp.float32, 1.0, 16.0))
    ssd_d = 1.0 + nrm(ks[20], (DEPTH, SSD_HEADS), 0.1)
    ssd_norm_g = 1.0 + nrm(ks[21], (DEPTH, SSD_INNER), 0.1)
    w_out = nrm(ks[22], (DEPTH, MIX_WIDTH, D_MODEL), MIX_WIDTH ** -0.5)
    mlp_w1 = nrm(ks[23], (DEPTH, D_MODEL, D_FF), D_MODEL ** -0.5)
    mlp_w2 = nrm(ks[24], (DEPTH, D_FF, D_MODEL), D_FF ** -0.5)
    final_g = 1.0 + nrm(ks[25], (D_MODEL,), 0.1)
    return {'x': x, 'c': c, 'ctx': ctx, 'c_ctx': c_ctx, 'ada_w': ada_w, 'ada_b': ada_b,
            'norm1_g': norm1_g, 'norm2_g': norm2_g, 'w_in': w_in,
            'lru_conv_w': lru_conv_w, 'lru_conv_b': lru_conv_b, 'lru_wa': lru_wa, 'lru_ba': lru_ba,
            'lru_wx': lru_wx, 'lru_bx': lru_bx, 'lru_lambda': lru_lambda,
            'ssd_conv_w': ssd_conv_w, 'ssd_conv_b': ssd_conv_b, 'ssd_dt_bias': ssd_dt_bias,
            'ssd_a_log': ssd_a_log, 'ssd_d': ssd_d, 'ssd_norm_g': ssd_norm_g,
            'w_out': w_out, 'mlp_w1': mlp_w1, 'mlp_w2': mlp_w2, 'final_g': final_g}


def reference(x, c, ctx, c_ctx, ada_w, ada_b, norm1_g, norm2_g, w_in,
              lru_conv_w, lru_conv_b, lru_wa, lru_ba, lru_wx, lru_bx, lru_lambda,
              ssd_conv_w, ssd_conv_b, ssd_dt_bias, ssd_a_log, ssd_d, ssd_norm_g,
              w_out, mlp_w1, mlp_w2, final_g):
    silu_c = jax.nn.silu(c)
    silu_cc = jax.nn.silu(c_ctx)
    for l in range(DEPTH):
        need_ctx_out = l < DEPTH - 1
        mod = silu_c @ ada_w[l] + ada_b[l]
        sh1, sc1, g1, sh2, sc2, g2 = jnp.split(mod[:, None, :], N_MOD, axis=-1)
        csh1, csc1, cg1, csh2, csc2, cg2 = jnp.split(silu_cc @ ada_w[l] + ada_b[l], N_MOD, axis=-1)
        h_lat = _modulate(_rmsnorm(x, norm1_g[l]), sh1, sc1)
        h_ctx = _modulate(_rmsnorm(ctx, norm1_g[l]), csh1, csc1)
        y_lat, y_ctx = _mixer(h_lat, h_ctx, w_in[l], lru_conv_w[l], lru_conv_b[l], lru_wa[l], lru_ba[l],
                              lru_wx[l], lru_bx[l], lru_lambda[l], ssd_conv_w[l], ssd_conv_b[l],
                              ssd_dt_bias[l], ssd_a_log[l], ssd_d[l], ssd_norm_g[l], w_out[l], need_ctx_out)
        x = x + g1 * y_lat
        x = x + g2 * _sq_relu_mlp(_modulate(_rmsnorm(x, norm2_g[l]), sh2, sc2), mlp_w1[l], mlp_w2[l])
        if need_ctx_out:
            ctx = ctx + cg1 * y_ctx
            ctx = ctx + cg2 * _sq_relu_mlp(_modulate(_rmsnorm(ctx, norm2_g[l]), csh2, csc2), mlp_w1[l], mlp_w2[l])
    return _rmsnorm(x, final_g)
```

```python
import functools
import math

import jax
import jax.numpy as jnp
from jax import lax
from jax.experimental import pallas as pl
from jax.experimental.pallas import tpu as pltpu

F32 = jnp.float32
BF16 = jnp.bfloat16

EPS = 1e-6
GRID_W = 64
LRU_BLOCK = 64
LRU_C = 8.0
SSD_HEAD_DIM = 64
SSD_HEADS = 8
SSD_GROUPS = 2
SSD_STATE = 128
SSD_CHUNK = 128
CONV_K = 4
N_MOD = 6
LANES = 128
SUBLANES = 8
MOD_ROWS = 16
VMEM_LIMIT = 56 * 1024 * 1024
NEG_BIG = -1e30


def _sigmoid(x):
    return 0.5 * (jnp.tanh(0.5 * x) + 1.0)


def _silu(x):
    return x * _sigmoid(x)


def _softplus(x):
    return jnp.maximum(x, 0.0) + jnp.log1p(jnp.exp(-jnp.abs(x)))


def _gelu_tanh(x):
    return 0.5 * x * (1.0 + jnp.tanh(math.sqrt(2.0 / math.pi) * (x + 0.044715 * (x * x * x))))


def _rmsnorm(x, g):
    return x * lax.rsqrt(jnp.mean(x * x, axis=-1, keepdims=True) + EPS) * g


def _dot(a, b):
    return jnp.dot(a, b, preferred_element_type=F32)


def _cparams(*sem):
    return pltpu.CompilerParams(dimension_semantics=sem, vmem_limit_bytes=VMEM_LIMIT)


def _mod_kernel(c_ref, w_ref, b_ref, o_ref):
    s = _silu(c_ref[...]).astype(BF16)
    o_ref[0] = _dot(s, w_ref[0].astype(BF16)) + b_ref[0]


def _modulation(cvec, ada_w, ada_b):
    depth, d, n = ada_w.shape
    tn = 512
    return pl.pallas_call(
        _mod_kernel,
        out_shape=jax.ShapeDtypeStruct((depth, MOD_ROWS, n), F32),
        grid=(depth, n // tn),
        in_specs=[
            pl.BlockSpec((MOD_ROWS, d), lambda l, j: (0, 0)),
            pl.BlockSpec((1, d, tn), lambda l, j: (l, 0, j)),
            pl.BlockSpec((1, 1, tn), lambda l, j: (l, 0, j)),
        ],
        out_specs=pl.BlockSpec((1, MOD_ROWS, tn), lambda l, j: (l, 0, j)),
        compiler_params=_cparams("arbitrary", "arbitrary"),
        name="modulation",
    )(cvec, ada_w, ada_b.reshape(depth, 1, n))


def _inproj_kernel(x_ref, mod_ref, g_ref, w_ref, lx_ref, xbc_ref, lg_ref, z_ref, dt_ref, *, lw, nxbc):
    x = x_ref[0]
    h = _rmsnorm(x, g_ref[...]) * (1.0 + mod_ref[0, 1:2, :]) + mod_ref[0, 0:1, :]
    h = h.astype(BF16)
    lx_ref[0] = _dot(h, w_ref[:, 0:lw])
    xbc = _dot(h, w_ref[:, lw:lw + nxbc])
    for j in range(nxbc // LANES):
        xbc_ref[0, j] = xbc[:, j * LANES:(j + 1) * LANES]
    o = lw + nxbc
    lg_ref[0] = _dot(h, w_ref[:, o:o + lw])
    z_ref[0] = _dot(h, w_ref[:, o + lw:o + 2 * lw])
    dt_ref[0] = _dot(h, w_ref[:, o + 2 * lw:o + 2 * lw + LANES])


def _inproj(x, mod, mod_row, g, w, *, lw, nxbc, tm):
    b, t, d = x.shape
    ns = nxbc // LANES
    kern = functools.partial(_inproj_kernel, lw=lw, nxbc=nxbc)
    tok = lambda n: pl.BlockSpec((1, tm, n), lambda i, j: (i, j, 0))
    return pl.pallas_call(
        kern,
        out_shape=(
            jax.ShapeDtypeStruct((b, t, lw), F32),
            jax.ShapeDtypeStruct((b, ns, t, LANES), F32),
            jax.ShapeDtypeStruct((b, t, lw), F32),
            jax.ShapeDtypeStruct((b, t, lw), F32),
            jax.ShapeDtypeStruct((b, t, LANES), F32),
        ),
        grid=(b, t // tm),
        in_specs=[
            tok(d),
            pl.BlockSpec((1, N_MOD, d), lambda i, j: (mod_row(i), 0, 0)),
            pl.BlockSpec((1, d), lambda i, j: (0, 0)),
            pl.BlockSpec(w.shape, lambda i, j: (0, 0), pipeline_mode=pl.Buffered(1)),
        ],
        out_specs=(
            tok(lw),
            pl.BlockSpec((1, ns, tm, LANES), lambda i, j: (i, 0, j, 0)),
            tok(lw),
            tok(lw),
            tok(LANES),
        ),
        compiler_params=_cparams("arbitrary", "arbitrary"),
        name="inproj",
    )(x, mod, g, w)


LRU_ROWS = 256


def _lru_seq(t, u_ref, lg_ref, out_ref, h0, refs, scr, need_out):
    cw_ref, cb_ref, wa_ref, wx_ref, ba_ref, bx_ref, lam_ref = refs
    upad, ubuf, uint, hbuf, pbuf, ybuf = scr
    seg = t // SUBLANES
    pitch = seg + 4
    nblk = max(t // LRU_ROWS, 1)
    rb = t // nblk
    nv = rb // SUBLANES

    zeros8 = jnp.zeros((SUBLANES, LANES), F32)
    upad[0:SUBLANES, :] = zeros8
    upad[SUBLANES:SUBLANES + t, :] = u_ref[...]
    upad[SUBLANES + t:2 * SUBLANES + t, :] = zeros8

    cw = cw_ref[...]
    n = seg + 2 * SUBLANES
    for s in range(SUBLANES):
        v = upad[s * seg:s * seg + n, :]
        acc = cb_ref[...] + cw[1:2, :] * v
        acc = acc + cw[0:1, :] * pltpu.roll(v, 1, 0)
        acc = acc + cw[2:3, :] * pltpu.roll(v, n - 1, 0)
        acc = acc + cw[3:4, :] * pltpu.roll(v, n - 2, 0)
        ubuf[s * pitch:s * pitch + seg, :] = acc[SUBLANES:SUBLANES + seg, :]

    for tau in range(seg):
        uint[tau * SUBLANES:(tau + 1) * SUBLANES, :] = ubuf[pl.ds(tau, SUBLANES, stride=pitch), :]

    row = lax.broadcasted_iota(jnp.int32, (SUBLANES, LANES), 0)
    finals = []
    cmats = []
    for d in range(2):
        sp = _softplus(-lam_ref[d:d + 1, :])
        ba = ba_ref[d:d + 1, :]
        bx = bx_ref[d:d + 1, :]

        def blk_body(i, carry, d=d, sp=sp, ba=ba, bx=bx):
            h, p = carry
            blk = i if d == 0 else nblk - 1 - i
            r0 = pl.multiple_of(blk * rb, rb)
            u = uint[pl.ds(r0, rb), :]
            ub = u.astype(BF16)
            r = _sigmoid(_dot(ub, wa_ref[d, 0]) + ba)
            ig = _sigmoid(_dot(ub, wx_ref[d, 0]) + bx)
            log_a = (-LRU_C) * r * sp
            a = jnp.exp(log_a)
            bb = jnp.sqrt(jnp.tanh(-log_a) * (1.0 + a * a)) * (ig * u)
            hs = [None] * nv
            ps = [None] * nv
            order = range(nv) if d == 0 else range(nv - 1, -1, -1)
            for k in order:
                ak = a[k * SUBLANES:(k + 1) * SUBLANES, :]
                h = ak * h + bb[k * SUBLANES:(k + 1) * SUBLANES, :]
                p = ak * p
                hs[k] = h
                ps[k] = p
            hbuf[d, pl.ds(r0, rb), :] = jnp.concatenate(hs, axis=0)
            pbuf[d, pl.ds(r0, rb), :] = jnp.concatenate(ps, axis=0)
            return h, p

        h, p = lax.fori_loop(0, nblk, blk_body, (zeros8, jnp.ones((SUBLANES, LANES), F32)))
        c = h0[d]
        cmat = zeros8
        order = range(SUBLANES) if d == 0 else range(SUBLANES - 1, -1, -1)
        for s in order:
            cmat = jnp.where(row == s, jnp.broadcast_to(c, (SUBLANES, LANES)), cmat)
            c = h[s:s + 1, :] + p[s:s + 1, :] * c
        finals.append(c)
        cmats.append(cmat)

    if need_out:
        def out_body(i, carry):
            r0 = pl.multiple_of(i * rb, rb)
            y = hbuf[0, pl.ds(r0, rb), :].reshape(nv, SUBLANES, LANES)
            y = y + pbuf[0, pl.ds(r0, rb), :].reshape(nv, SUBLANES, LANES) * cmats[0][None]
            y = y + hbuf[1, pl.ds(r0, rb), :].reshape(nv, SUBLANES, LANES)
            y = y + pbuf[1, pl.ds(r0, rb), :].reshape(nv, SUBLANES, LANES) * cmats[1][None]
            ybuf[pl.ds(r0, rb), :] = y.reshape(rb, LANES)
            return carry

        lax.fori_loop(0, nblk, out_body, 0)
        for tau in range(seg):
            ubuf[pl.ds(tau, SUBLANES, stride=pitch), :] = ybuf[tau * SUBLANES:(tau + 1) * SUBLANES, :]
        for s in range(SUBLANES):
            out_ref[s * seg:(s + 1) * seg, :] = (
                ubuf[s * pitch:s * pitch + seg, :] * _gelu_tanh(lg_ref[s * seg:(s + 1) * seg, :]))
    return finals


def _lru_kernel(*args, t_lat, t_ctx, need_ctx):
    if need_ctx:
        (ul_ref, uc_ref, lgl_ref, lgc_ref, *refs, yl_ref, yc_ref, upad, ubuf, uint, hbuf, pbuf, ybuf) = args
    else:
        (ul_ref, uc_ref, lgl_ref, *refs, yl_ref, upad, ubuf, uint, hbuf, pbuf, ybuf) = args
        lgc_ref = yc_ref = None
    scr = (upad, ubuf, uint, hbuf, pbuf, ybuf)
    zero = jnp.zeros((1, LANES), F32)
    ctx_final = _lru_seq(t_ctx, uc_ref.at[0], lgc_ref.at[0] if need_ctx else None,
                         yc_ref.at[0] if need_ctx else None, (zero, zero), refs, scr, need_ctx)
    _lru_seq(t_lat, ul_ref.at[0], lgl_ref.at[0], yl_ref.at[0], ctx_final, refs, scr, True)


def _lru(lx_l, lx_c, lg_l, lg_c, conv_w, conv_b, wa_bd, wx_bd, ba, bx, lam, *, need_ctx):
    b, t_lat, w = lx_l.shape
    t_ctx = lx_c.shape[1]
    nt = w // LANES
    seq = lambda t: pl.BlockSpec((1, t, LANES), lambda i, j: (i, 0, j))
    par = lambda r: pl.BlockSpec((r, LANES), lambda i, j: (0, j))
    gate = pl.BlockSpec((2, 1, LANES, LANES), lambda i, j: (0, j, 0, 0))
    kern = functools.partial(_lru_kernel, t_lat=t_lat, t_ctx=t_ctx, need_ctx=need_ctx)
    ins = [lx_l, lx_c, lg_l] + ([lg_c] if need_ctx else [])
    in_specs = [seq(t_lat), seq(t_ctx), seq(t_lat)] + ([seq(t_ctx)] if need_ctx else [])
    ins += [conv_w, conv_b, wa_bd, wx_bd, ba, bx, lam]
    in_specs += [par(CONV_K), par(1), gate, gate, par(2), par(2), par(2)]
    out_shape = [jax.ShapeDtypeStruct((b, t_lat, w), F32)]
    out_specs = [seq(t_lat)]
    if need_ctx:
        out_shape.append(jax.ShapeDtypeStruct((b, t_ctx, w), F32))
        out_specs.append(seq(t_ctx))
    pitch_rows = SUBLANES * (t_lat // SUBLANES + 4) + SUBLANES
    res = pl.pallas_call(
        kern,
        out_shape=tuple(out_shape),
        grid=(b, nt),
        in_specs=in_specs,
        out_specs=tuple(out_specs),
        scratch_shapes=[
            pltpu.VMEM((t_lat + 2 * SUBLANES, LANES), F32),
            pltpu.VMEM((pitch_rows, LANES), F32),
            pltpu.VMEM((t_lat, LANES), F32),
            pltpu.VMEM((2, t_lat, LANES), F32),
            pltpu.VMEM((2, t_lat, LANES), F32),
            pltpu.VMEM((t_lat, LANES), F32),
        ],
        compiler_params=_cparams("arbitrary", "arbitrary"),
        name="rglru",
    )(*ins)
    return (res[0], res[1]) if need_ctx else (res[0], None)


def _ssd_kernel(*args, t_lat, t_ctx, need_ctx):
    if need_ctx:
        (xl_ref, dtl_ref, xc_ref, dtc_ref, cw_ref, cb_ref, dtb_ref, alog_ref, dsk_ref,
         yl_ref, yc_ref, stg, scx, scbc, dtp, ht, yacc) = args
    else:
        (xl_ref, dtl_ref, xc_ref, dtc_ref, cw_ref, cb_ref, dtb_ref, alog_ref, dsk_ref,
         yl_ref, stg, scx, scbc, dtp, ht, yacc) = args
        yc_ref = None
    ck = SSD_CHUNK
    rows = t_lat // GRID_W
    wpc = ck // rows
    nc_ctx = t_ctx // ck
    nc_lat = t_lat // ck
    nx = SSD_HEADS * SSD_HEAD_DIM // LANES
    ng = SSD_GROUPS
    nslab = nx + 2 * ng
    halo = SUBLANES

    zrow = jnp.zeros((1, LANES), F32)

    def conv_store(j, row0):
        n = ck + 2 * halo
        v = stg[...]
        cw = cw_ref[j]
        acc = cb_ref[j] + cw[1:2, :] * v
        acc = acc + cw[0:1, :] * pltpu.roll(v, 1, 0)
        acc = acc + cw[2:3, :] * pltpu.roll(v, n - 1, 0)
        acc = acc + cw[3:4, :] * pltpu.roll(v, n - 2, 0)
        s = _silu(acc[halo:halo + ck, :])

        @pl.when(j < nx)
        def _():
            scx[jnp.minimum(j, nx - 1), pl.ds(row0, ck), :] = s

        @pl.when(j >= nx)
        def _():
            scbc[jnp.maximum(j - nx, 0), pl.ds(row0, ck), :] = s.astype(BF16)

    stg[0:halo, :] = jnp.zeros((halo, LANES), F32)
    stg[halo + ck:2 * halo + ck, :] = jnp.zeros((halo, LANES), F32)
    for k in range(nc_ctx):
        def ctx_slab(j, carry, k=k):
            stg[halo:halo + ck, :] = xc_ref[0, j, k * ck:(k + 1) * ck, :]
            stg[halo - 1:halo, :] = xc_ref[0, j, k * ck - 1:k * ck, :] if k > 0 else zrow
            if k < nc_ctx - 1:
                stg[halo + ck:halo + ck + 2, :] = xc_ref[0, j, (k + 1) * ck:(k + 1) * ck + 2, :]
            else:
                stg[halo + ck:halo + ck + 2, :] = jnp.zeros((2, LANES), F32)
            conv_store(j, k * ck)
            return carry

        lax.fori_loop(0, nslab, ctx_slab, 0)
    dtp[0:t_ctx, :] = dtc_ref[0]

    def lat_chunk(c, carry):
        w0 = c * wpc
        row0 = pl.multiple_of(t_ctx + c * ck, ck)
        for wl in range(wpc):
            dtp[pl.ds(row0 + wl * rows, rows), :] = dtl_ref[0, pl.ds(w0 + wl, rows, stride=GRID_W), :]

        def lat_slab(j, carry2):
            for wl in range(wpc):
                stg[halo + wl * rows:halo + (wl + 1) * rows, :] = (
                    xl_ref[0, j, pl.ds(w0 + wl, rows, stride=GRID_W), :])
            prev = xl_ref[0, j, pl.ds((rows - 1) * GRID_W + jnp.maximum(w0 - 1, 0), 1), :]
            stg[halo - 1:halo, :] = jnp.where(c > 0, prev, zrow)
            wn = jnp.minimum(w0 + wpc, GRID_W - 1)
            nxt = xl_ref[0, j, pl.ds(wn, 2, stride=GRID_W), :]
            stg[halo + ck:halo + ck + 2, :] = jnp.where(c < nc_lat - 1, nxt, jnp.zeros((2, LANES), F32))
            conv_store(j, row0)
            return carry2

        lax.fori_loop(0, nslab, lat_slab, 0)
        return carry

    lax.fori_loop(0, nc_lat, lat_chunk, 0)

    ri = lax.broadcasted_iota(jnp.int32, (ck, ck), 0)
    ci = lax.broadcasted_iota(jnp.int32, (ck, ck), 1)
    lane_lo = ci < SSD_HEAD_DIM
    dtb = dtb_ref[...]
    nega = -jnp.exp(alog_ref[...])
    lane_ok = lax.broadcasted_iota(jnp.int32, (1, LANES), 1) < 2 * SSD_HEADS
    nega = jnp.where(lane_ok, nega, 0.0)

    def chunk(kc, d, want_y, is_ctx):
        r0 = kc * ck if isinstance(kc, int) else pl.multiple_of(kc * ck, ck)
        tri = (ri >= ci) if d == 0 else (ri <= ci)
        dt = _softplus(dtp[pl.ds(r0, ck), :] + dtb)
        la = dt * nega
        cum = jnp.dot(tri.astype(F32), la, precision=lax.Precision.HIGHEST, preferred_element_type=F32)
        tot = cum[ck - 1:ck, :] if d == 0 else cum[0:1, :]
        wgt = dt * jnp.exp(tot - cum)
        cum_t = cum.T
        dt_t = dt.T
        etot = jnp.exp(tot)
        xs = [scx[j, pl.ds(r0, ck), :] for j in range(nx)]
        for g in range(ng):
            bm = scbc[g, pl.ds(r0, ck), :]
            cm = scbc[ng + g, pl.ds(r0, ck), :]
            if want_y:
                cb = lax.dot_general(cm, bm, (((1,), (1,)), ((), ())), preferred_element_type=F32)
            hpg = SSD_HEADS // ng
            for pr in range(hpg // 2):
                j = g * (hpg // 2) + pr
                hc = [d * SSD_HEADS + 2 * j, d * SSD_HEADS + 2 * j + 1]
                x = xs[j]
                col = [jnp.broadcast_to(cum[:, h:h + 1], (ck, LANES)) for h in hc]
                wcol = [jnp.broadcast_to(wgt[:, h:h + 1], (ck, LANES)) for h in hc]
                xw = (x * jnp.where(lane_lo, wcol[0], wcol[1])).astype(BF16)
                hprev = ht[j]
                if want_y:
                    xb = x.astype(BF16)
                    zb = jnp.zeros_like(xb)
                    ms = []
                    for q in range(2):
                        seg = col[q] - cum_t[hc[q]:hc[q] + 1, :]
                        lm = jnp.exp(jnp.where(tri, seg, NEG_BIG))
                        ms.append((cb * lm * dt_t[hc[q]:hc[q] + 1, :]).astype(BF16))
                    mcat = jnp.concatenate(ms, axis=1)
                    xcat = jnp.concatenate([jnp.where(lane_lo, xb, zb), jnp.where(lane_lo, zb, xb)], axis=0)
                    y = _dot(mcat, xcat)
                    ecs = jnp.exp(jnp.where(lane_lo, col[0], col[1]))
                    y = y + _dot(cm, hprev.astype(BF16)) * ecs
                    if d == 0:
                        yacc[j, pl.ds(r0, ck), :] = y + dsk_ref[j] * x
                    else:
                        y = y + yacc[j, pl.ds(r0, ck), :]
                        if is_ctx:
                            yc_ref[0, j, pl.ds(r0, ck), :] = y
                        else:
                            w0 = (kc - nc_ctx) * wpc
                            for wl in range(wpc):
                                yl_ref[0, j, pl.ds(w0 + wl, rows, stride=GRID_W), :] = (
                                    y[wl * rows:(wl + 1) * rows, :])
                et = jnp.where(lane_lo[0:1, :], jnp.broadcast_to(etot[:, hc[0]:hc[0] + 1], (1, LANES)),
                               jnp.broadcast_to(etot[:, hc[1]:hc[1] + 1], (1, LANES)))
                ht[j] = hprev * et + lax.dot_general(
                    bm, xw, (((0,), (0,)), ((), ())), preferred_element_type=F32)

    for d in range(2):
        ht[...] = jnp.zeros(ht.shape, F32)
        ctx_order = range(nc_ctx) if d == 0 else range(nc_ctx - 1, -1, -1)
        for kc in ctx_order:
            chunk(kc, d, need_ctx, True)

        def lat_body(i, carry, d=d):
            c = i if d == 0 else nc_lat - 1 - i
            chunk(nc_ctx + c, d, True, False)
            return carry

        lax.fori_loop(0, nc_lat, lat_body, 0)


def _ssd(xbc_l, dt_l, xbc_c, dt_c, cw, cb, dtb, alog, dsk, *, need_ctx):
    b, nslab, t_lat, _ = xbc_l.shape
    t_ctx = xbc_c.shape[2]
    nx = SSD_HEADS * SSD_HEAD_DIM // LANES
    tot = t_lat + t_ctx
    kern = functools.partial(_ssd_kernel, t_lat=t_lat, t_ctx=t_ctx, need_ctx=need_ctx)
    slab = lambda n, t: pl.BlockSpec((1, n, t, LANES), lambda i: (i, 0, 0, 0))
    seq = lambda t: pl.BlockSpec((1, t, LANES), lambda i: (i, 0, 0))
    full = lambda a: pl.BlockSpec(a.shape, lambda i: (0,) * a.ndim)
    out_shape = [jax.ShapeDtypeStruct((b, nx, t_lat, LANES), F32)]
    out_specs = [slab(nx, t_lat)]
    if need_ctx:
        out_shape.append(jax.ShapeDtypeStruct((b, nx, t_ctx, LANES), F32))
        out_specs.append(slab(nx, t_ctx))
    res = pl.pallas_call(
        kern,
        out_shape=tuple(out_shape),
        grid=(b,),
        in_specs=[slab(nslab, t_lat), seq(t_lat), slab(nslab, t_ctx), seq(t_ctx),
                  full(cw), full(cb), full(dtb), full(alog), full(dsk)],
        out_specs=tuple(out_specs),
        scratch_shapes=[
            pltpu.VMEM((SSD_CHUNK + 2 * SUBLANES, LANES), F32),
            pltpu.VMEM((nx, tot, LANES), F32),
            pltpu.VMEM((nslab - nx, tot, LANES), BF16),
            pltpu.VMEM((tot, LANES), F32),
            pltpu.VMEM((nx, SSD_STATE, LANES), F32),
            pltpu.VMEM((nx, tot, LANES), F32),
        ],
        compiler_params=_cparams("arbitrary"),
        name="ssd",
    )(xbc_l, dt_l, xbc_c, dt_c, cw, cb, dtb, alog, dsk)
    return (res[0], res[1]) if need_ctx else (res[0], None)


def _outmlp_kernel(*args, final_norm, ff_chunk):
    if final_norm:
        (x_ref, lru_ref, ssd_ref, z_ref, mod_ref, sg_ref, n2_ref, wo_ref, w1_ref, w2_ref, fg_ref, o_ref) = args
    else:
        (x_ref, lru_ref, ssd_ref, z_ref, mod_ref, sg_ref, n2_ref, wo_ref, w1_ref, w2_ref, o_ref) = args
    x = x_ref[0]
    nx = ssd_ref.shape[1]
    z = z_ref[0]
    lw = z.shape[-1]
    gw = lw // SSD_GROUPS
    spg = nx // SSD_GROUPS
    parts = []
    for g in range(SSD_GROUPS):
        y = jnp.concatenate([ssd_ref[0, g * spg + k] for k in range(spg)], axis=1)
        y = y * _silu(z[:, g * gw:(g + 1) * gw])
        y = y * lax.rsqrt(jnp.mean(y * y, axis=-1, keepdims=True) + EPS)
        parts.append(y * sg_ref[:, g * gw:(g + 1) * gw])
    ssd_n = jnp.concatenate(parts, axis=1).astype(BF16)
    lru = lru_ref[0].astype(BF16)
    y = _dot(lru, wo_ref[0:lw, :]) + _dot(ssd_n, wo_ref[lw:2 * lw, :])
    x1 = x + mod_ref[0, 2:3, :] * y
    h = (_rmsnorm(x1, n2_ref[...]) * (1.0 + mod_ref[0, 4:5, :]) + mod_ref[0, 3:4, :]).astype(BF16)
    dff = w1_ref.shape[1]
    acc = jnp.zeros_like(x1)
    for j in range(dff // ff_chunk):
        a = _dot(h, w1_ref[:, j * ff_chunk:(j + 1) * ff_chunk])
        a = jnp.square(jnp.maximum(a, 0.0)).astype(BF16)
        acc = acc + _dot(a, w2_ref[j * ff_chunk:(j + 1) * ff_chunk, :])
    x2 = x1 + mod_ref[0, 5:6, :] * acc
    if final_norm:
        x2 = _rmsnorm(x2, fg_ref[...])
    o_ref[0] = x2


def _outmlp(x, lru, ssd, z, mod, mod_row, sg, n2, wo, w1, w2, fg, *, tm, ff_chunk=1024):
    b, t, d = x.shape
    lw = lru.shape[-1]
    nx = ssd.shape[1]
    final_norm = fg is not None
    kern = functools.partial(_outmlp_kernel, final_norm=final_norm, ff_chunk=ff_chunk)
    tok = lambda n: pl.BlockSpec((1, tm, n), lambda i, j: (i, j, 0))
    res = lambda a: pl.BlockSpec(a.shape, lambda i, j: (0, 0), pipeline_mode=pl.Buffered(1))
    row = lambda n: pl.BlockSpec((1, n), lambda i, j: (0, 0))
    ins = [x, lru, ssd, z, mod, sg, n2, wo, w1, w2]
    in_specs = [tok(d), tok(lw), pl.BlockSpec((1, nx, tm, LANES), lambda i, j: (i, 0, j, 0)), tok(lw),
                pl.BlockSpec((1, N_MOD, d), lambda i, j: (mod_row(i), 0, 0)),
                row(lw), row(d), res(wo), res(w1), res(w2)]
    if final_norm:
        ins.append(fg)
        in_specs.append(row(d))
    return pl.pallas_call(
        kern,
        out_shape=jax.ShapeDtypeStruct((b, t, d), F32),
        grid=(b, t // tm),
        in_specs=in_specs,
        out_specs=tok(d),
        compiler_params=_cparams("arbitrary", "arbitrary"),
        name="outmlp",
    )(*ins)


def _pair_block_diag(w):
    two, h, k, _ = w.shape
    w = w.reshape(two, h // 2, 2, k, k)
    z = jnp.zeros_like(w[:, :, 0])
    top = jnp.concatenate([w[:, :, 0], z], axis=-1)
    bot = jnp.concatenate([z, w[:, :, 1]], axis=-1)
    return jnp.concatenate([top, bot], axis=-2)


def _pad_lanes(a):
    return jnp.pad(a, [(0, 0)] * (a.ndim - 1) + [(0, LANES - a.shape[-1])])


def kernel(x, c, ctx, c_ctx, ada_w, ada_b, norm1_g, norm2_g, w_in, lru_conv_w, lru_conv_b, lru_wa, lru_ba,
           lru_wx, lru_bx, lru_lambda, ssd_conv_w, ssd_conv_b, ssd_dt_bias, ssd_a_log, ssd_d, ssd_norm_g,
           w_out, mlp_w1, mlp_w2, final_g):
    bsz, t_lat, d = x.shape
    t_ctx = ctx.shape[1]
    depth = ada_w.shape[0]
    lw = lru_conv_w.shape[-1]
    nxbc = ssd_conv_w.shape[-1]
    ndt = 2 * SSD_HEADS
    nscan = lw + nxbc + ndt
    assert bsz + 1 <= MOD_ROWS and t_lat % (GRID_W * SUBLANES) == 0 and t_ctx % SSD_CHUNK == 0

    cvec = jnp.zeros((MOD_ROWS, d), F32).at[:bsz].set(c).at[bsz].set(c_ctx)
    mod = _modulation(cvec, ada_w, ada_b).reshape(depth, MOD_ROWS, N_MOD, d)

    lat_row = lambda i: i
    ctx_row = lambda i: bsz
    tm_lat = 512
    tm_ctx = t_ctx
    nslab = nxbc // LANES

    for l in range(depth):
        need_ctx = l < depth - 1
        wl = w_in[l]
        w_re = jnp.concatenate(
            [wl[:, :lw + nxbc], wl[:, nscan:], _pad_lanes(wl[:, lw + nxbc:nscan])], axis=1).astype(BF16)
        g1 = norm1_g[l].reshape(1, d)
        lx_l, xbc_l, lg_l, z_l, dt_l = _inproj(x, mod[l], lat_row, g1, w_re, lw=lw, nxbc=nxbc, tm=tm_lat)
        lx_c, xbc_c, lg_c, z_c, dt_c = _inproj(ctx, mod[l], ctx_row, g1, w_re, lw=lw, nxbc=nxbc, tm=tm_ctx)

        lru_l, lru_c = _lru(
            lx_l, lx_c, lg_l, lg_c, lru_conv_w[l], lru_conv_b[l].reshape(1, lw),
            _pair_block_diag(lru_wa[l]).astype(BF16), _pair_block_diag(lru_wx[l]).astype(BF16),
            lru_ba[l], lru_bx[l], lru_lambda[l], need_ctx=need_ctx)

        cw = ssd_conv_w[l].reshape(CONV_K, nslab, LANES).transpose(1, 0, 2)
        cb = ssd_conv_b[l].reshape(nslab, 1, LANES)
        dtb = _pad_lanes(ssd_dt_bias[l].reshape(1, ndt))
        alog = _pad_lanes(ssd_a_log[l].reshape(1, ndt))
        dsk = jnp.repeat(ssd_d[l], SSD_HEAD_DIM).reshape(lw // LANES, 1, LANES)
        ssd_l, ssd_c = _ssd(xbc_l, dt_l, xbc_c, dt_c, cw, cb, dtb, alog, dsk, need_ctx=need_ctx)

        sg = ssd_norm_g[l].reshape(1, lw)
        n2 = norm2_g[l].reshape(1, d)
        wo = w_out[l].astype(BF16)
        w1 = mlp_w1[l].astype(BF16)
        w2 = mlp_w2[l].astype(BF16)
        fg = final_g.reshape(1, d) if l == depth - 1 else None
        x = _outmlp(x, lru_l, ssd_l, z_l, mod[l], lat_row, sg, n2, wo, w1, w2, fg, tm=tm_lat)
        if need_ctx:
            ctx = _outmlp(ctx, lru_c, ssd_c, z_c, mod[l], ctx_row, sg, n2, wo, w1, w2, None, tm=tm_ctx)
    return x
```

```python
import functools
import math

import jax
import jax.numpy as jnp
from jax import lax
from jax.experimental import pallas as pl
from jax.experimental.pallas import tpu as pltpu

F32 = jnp.float32
BF16 = jnp.bfloat16

EPS = 1e-6
GRID_W = 64
LRU_BLOCK = 64
LRU_C = 8.0
SSD_HEAD_DIM = 64
SSD_HEADS = 8
SSD_GROUPS = 2
SSD_STATE = 128
SSD_CHUNK = 128
CONV_K = 4
N_MOD = 6
LANES = 128
SUBLANES = 8
MOD_ROWS = 16
VMEM_LIMIT = 56 * 1024 * 1024
NEG_BIG = -1e30


def _sigmoid(x):
    return 0.5 * (jnp.tanh(0.5 * x) + 1.0)


def _silu(x):
    return x * _sigmoid(x)


def _softplus(x):
    return jnp.maximum(x, 0.0) + jnp.log1p(jnp.exp(-jnp.abs(x)))


def _gelu_tanh(x):
    return 0.5 * x * (1.0 + jnp.tanh(math.sqrt(2.0 / math.pi) * (x + 0.044715 * (x * x * x))))


def _rmsnorm(x, g):
    return x * lax.rsqrt(jnp.mean(x * x, axis=-1, keepdims=True) + EPS) * g


def _dot(a, b):
    return jnp.dot(a, b, preferred_element_type=F32)


def _cparams(*sem):
    return pltpu.CompilerParams(dimension_semantics=sem, vmem_limit_bytes=VMEM_LIMIT)


def _mod_kernel(c_ref, w_ref, b_ref, o_ref):
    s = _silu(c_ref[...]).astype(BF16)
    o_ref[0] = _dot(s, w_ref[0].astype(BF16)) + b_ref[0]


def _modulation(cvec, ada_w, ada_b):
    depth, d, n = ada_w.shape
    tn = 512
    return pl.pallas_call(
        _mod_kernel,
        out_shape=jax.ShapeDtypeStruct((depth, MOD_ROWS, n), F32),
        grid=(depth, n // tn),
        in_specs=[
            pl.BlockSpec((MOD_ROWS, d), lambda l, j: (0, 0)),
            pl.BlockSpec((1, d, tn), lambda l, j: (l, 0, j)),
            pl.BlockSpec((1, 1, tn), lambda l, j: (l, 0, j)),
        ],
        out_specs=pl.BlockSpec((1, MOD_ROWS, tn), lambda l, j: (l, 0, j)),
        compiler_params=_cparams("arbitrary", "arbitrary"),
        name="modulation",
    )(cvec, ada_w, ada_b.reshape(depth, 1, n))


def _inproj_kernel(x_ref, mod_ref, g_ref, w_ref, lx_ref, xbc_ref, lg_ref, z_ref, dt_ref, *, lw, nxbc):
    x = x_ref[0]
    h = _rmsnorm(x, g_ref[...]) * (1.0 + mod_ref[0, 1:2, :]) + mod_ref[0, 0:1, :]
    h = h.astype(BF16)
    lx_ref[0] = _dot(h, w_ref[:, 0:lw])
    xbc = _dot(h, w_ref[:, lw:lw + nxbc])
    for j in range(nxbc // LANES):
        xbc_ref[0, j] = xbc[:, j * LANES:(j + 1) * LANES]
    o = lw + nxbc
    lg_ref[0] = _dot(h, w_ref[:, o:o + lw])
    z_ref[0] = _dot(h, w_ref[:, o + lw:o + 2 * lw])
    dt_ref[0] = _dot(h, w_ref[:, o + 2 * lw:o + 2 * lw + LANES])


def _inproj(x, mod, mod_row, g, w, *, lw, nxbc, tm):
    b, t, d = x.shape
    ns = nxbc // LANES
    kern = functools.partial(_inproj_kernel, lw=lw, nxbc=nxbc)
    tok = lambda n: pl.BlockSpec((1, tm, n), lambda i, j: (i, j, 0))
    return pl.pallas_call(
        kern,
        out_shape=(
            jax.ShapeDtypeStruct((b, t, lw), F32),
            jax.ShapeDtypeStruct((b, ns, t, LANES), F32),
            jax.ShapeDtypeStruct((b, t, lw), F32),
            jax.ShapeDtypeStruct((b, t, lw), F32),
            jax.ShapeDtypeStruct((b, t, LANES), F32),
        ),
        grid=(b, t // tm),
        in_specs=[
            tok(d),
            pl.BlockSpec((1, N_MOD, d), lambda i, j: (mod_row(i), 0, 0)),
            pl.BlockSpec((1, d), lambda i, j: (0, 0)),
            pl.BlockSpec(w.shape, lambda i, j: (0, 0), pipeline_mode=pl.Buffered(1)),
        ],
        out_specs=(
            tok(lw),
            pl.BlockSpec((1, ns, tm, LANES), lambda i, j: (i, 0, j, 0)),
            tok(lw),
            tok(lw),
            tok(LANES),
        ),
        compiler_params=_cparams("arbitrary", "arbitrary"),
        name="inproj",
    )(x, mod, g, w)


LRU_ROWS = 256


def _lru_seq(t, u_ref, lg_ref, out_ref, h0, refs, scr, need_out):
    cw_ref, cb_ref, wa_ref, wx_ref, ba_ref, bx_ref, lam_ref = refs
    upad, ubuf, uint, hbuf, pbuf, ybuf = scr
    seg = t // SUBLANES
    pitch = seg + 4
    nblk = max(t // LRU_ROWS, 1)
    rb = t // nblk
    nv = rb // SUBLANES

    zeros8 = jnp.zeros((SUBLANES, LANES), F32)
    upad[0:SUBLANES, :] = zeros8
    upad[SUBLANES:SUBLANES + t, :] = u_ref[...]
    upad[SUBLANES + t:2 * SUBLANES + t, :] = zeros8

    cw = cw_ref[...]
    n = seg + 2 * SUBLANES
    for s in range(SUBLANES):
        v = upad[s * seg:s * seg + n, :]
        acc = cb_ref[...] + cw[1:2, :] * v
        acc = acc + cw[0:1, :] * pltpu.roll(v, 1, 0)
        acc = acc + cw[2:3, :] * pltpu.roll(v, n - 1, 0)
        acc = acc + cw[3:4, :] * pltpu.roll(v, n - 2, 0)
        ubuf[s * pitch:s * pitch + seg, :] = acc[SUBLANES:SUBLANES + seg, :]

    for tau in range(seg):
        uint[tau * SUBLANES:(tau + 1) * SUBLANES, :] = ubuf[pl.ds(tau, SUBLANES, stride=pitch), :]

    row = lax.broadcasted_iota(jnp.int32, (SUBLANES, LANES), 0)
    finals = []
    cmats = []
    for d in range(2):
        sp = _softplus(-lam_ref[d:d + 1, :])
        ba = ba_ref[d:d + 1, :]
        bx = bx_ref[d:d + 1, :]

        def blk_body(i, carry, d=d, sp=sp, ba=ba, bx=bx):
            h, p = carry
            blk = i if d == 0 else nblk - 1 - i
            r0 = pl.multiple_of(blk * rb, rb)
            u = uint[pl.ds(r0, rb), :]
            ub = u.astype(BF16)
            r = _sigmoid(_dot(ub, wa_ref[d, 0]) + ba)
            ig = _sigmoid(_dot(ub, wx_ref[d, 0]) + bx)
            log_a = (-LRU_C) * r * sp
            a = jnp.exp(log_a)
            bb = jnp.sqrt(jnp.tanh(-log_a) * (1.0 + a * a)) * (ig * u)
            hs = [None] * nv
            ps = [None] * nv
            order = range(nv) if d == 0 else range(nv - 1, -1, -1)
            for k in order:
                ak = a[k * SUBLANES:(k + 1) * SUBLANES, :]
                h = ak * h + bb[k * SUBLANES:(k + 1) * SUBLANES, :]
                p = ak * p
                hs[k] = h
                ps[k] = p
            hbuf[d, pl.ds(r0, rb), :] = jnp.concatenate(hs, axis=0)
            pbuf[d, pl.ds(r0, rb), :] = jnp.concatenate(ps, axis=0)
            return h, p

        h, p = lax.fori_loop(0, nblk, blk_body, (zeros8, jnp.ones((SUBLANES, LANES), F32)))
        c = h0[d]
        cmat = zeros8
        order = range(SUBLANES) if d == 0 else range(SUBLANES - 1, -1, -1)
        for s in order:
            cmat = jnp.where(row == s, jnp.broadcast_to(c, (SUBLANES, LANES)), cmat)
            c = h[s:s + 1, :] + p[s:s + 1, :] * c
        finals.append(c)
        cmats.append(cmat)

    if need_out:
        def out_body(i, carry):
            r0 = pl.multiple_of(i * rb, rb)
            y = hbuf[0, pl.ds(r0, rb), :].reshape(nv, SUBLANES, LANES)
            y = y + pbuf[0, pl.ds(r0, rb), :].reshape(nv, SUBLANES, LANES) * cmats[0][None]
            y = y + hbuf[1, pl.ds(r0, rb), :].reshape(nv, SUBLANES, LANES)
            y = y + pbuf[1, pl.ds(r0, rb), :].reshape(nv, SUBLANES, LANES) * cmats[1][None]
            ybuf[pl.ds(r0, rb), :] = y.reshape(rb, LANES)
            return carry

        lax.fori_loop(0, nblk, out_body, 0)
        for tau in range(seg):
            ubuf[pl.ds(tau, SUBLANES, stride=pitch), :] = ybuf[tau * SUBLANES:(tau + 1) * SUBLANES, :]
        for s in range(SUBLANES):
            out_ref[s * seg:(s + 1) * seg, :] = (
                ubuf[s * pitch:s * pitch + seg, :] * _gelu_tanh(lg_ref[s * seg:(s + 1) * seg, :]))
    return finals


def _lru_kernel(*args, t_lat, t_ctx, need_ctx):
    if need_ctx:
        (ul_ref, uc_ref, lgl_ref, lgc_ref, *refs, yl_ref, yc_ref, upad, ubuf, uint, hbuf, pbuf, ybuf) = args
    else:
        (ul_ref, uc_ref, lgl_ref, *refs, yl_ref, upad, ubuf, uint, hbuf, pbuf, ybuf) = args
        lgc_ref = yc_ref = None
    scr = (upad, ubuf, uint, hbuf, pbuf, ybuf)
    zero = jnp.zeros((1, LANES), F32)
    ctx_final = _lru_seq(t_ctx, uc_ref.at[0], lgc_ref.at[0] if need_ctx else None,
                         yc_ref.at[0] if need_ctx else None, (zero, zero), refs, scr, need_ctx)
    _lru_seq(t_lat, ul_ref.at[0], lgl_ref.at[0], yl_ref.at[0], ctx_final, refs, scr, True)


def _lru(lx_l, lx_c, lg_l, lg_c, conv_w, conv_b, wa_bd, wx_bd, ba, bx, lam, *, need_ctx):
    b, t_lat, w = lx_l.shape
    t_ctx = lx_c.shape[1]
    nt = w // LANES
    seq = lambda t: pl.BlockSpec((1, t, LANES), lambda i, j: (i, 0, j))
    par = lambda r: pl.BlockSpec((r, LANES), lambda i, j: (0, j))
    gate = pl.BlockSpec((2, 1, LANES, LANES), lambda i, j: (0, j, 0, 0))
    kern = functools.partial(_lru_kernel, t_lat=t_lat, t_ctx=t_ctx, need_ctx=need_ctx)
    ins = [lx_l, lx_c, lg_l] + ([lg_c] if need_ctx else [])
    in_specs = [seq(t_lat), seq(t_ctx), seq(t_lat)] + ([seq(t_ctx)] if need_ctx else [])
    ins += [conv_w, conv_b, wa_bd, wx_bd, ba, bx, lam]
    in_specs += [par(CONV_K), par(1), gate, gate, par(2), par(2), par(2)]
    out_shape = [jax.ShapeDtypeStruct((b, t_lat, w), F32)]
    out_specs = [seq(t_lat)]
    if need_ctx:
        out_shape.append(jax.ShapeDtypeStruct((b, t_ctx, w), F32))
        out_specs.append(seq(t_ctx))
    pitch_rows = SUBLANES * (t_lat // SUBLANES + 4) + SUBLANES
    res = pl.pallas_call(
        kern,
        out_shape=tuple(out_shape),
        grid=(b, nt),
        in_specs=in_specs,
        out_specs=tuple(out_specs),
        scratch_shapes=[
            pltpu.VMEM((t_lat + 2 * SUBLANES, LANES), F32),
            pltpu.VMEM((pitch_rows, LANES), F32),
            pltpu.VMEM((t_lat, LANES), F32),
            pltpu.VMEM((2, t_lat, LANES), F32),
            pltpu.VMEM((2, t_lat, LANES), F32),
            pltpu.VMEM((t_lat, LANES), F32),
        ],
        compiler_params=_cparams("arbitrary", "arbitrary"),
        name="rglru",
    )(*ins)
    return (res[0], res[1]) if need_ctx else (res[0], None)


def _ssd_kernel(*args, t_lat, t_ctx, need_ctx):
    if need_ctx:
        (xl_ref, dtl_ref, xc_ref, dtc_ref, cw_ref, cb_ref, dtb_ref, alog_ref, dsk_ref,
         yl_ref, yc_ref, stg, xm, scc, sbt, sbtf, dtp, srct, etb, sbuf, hst, yacc) = args
    else:
        (xl_ref, dtl_ref, xc_ref, dtc_ref, cw_ref, cb_ref, dtb_ref, alog_ref, dsk_ref,
         yl_ref, stg, xm, scc, sbt, sbtf, dtp, srct, etb, sbuf, hst, yacc) = args
        yc_ref = None
    ck = SSD_CHUNK
    rows = t_lat // GRID_W
    wpc = ck // rows
    nc_ctx = t_ctx // ck
    nc_lat = t_lat // ck
    nx = SSD_HEADS * SSD_HEAD_DIM // LANES
    ng = SSD_GROUPS
    halo = SUBLANES

    zrow = jnp.zeros((1, LANES), F32)
    ri = lax.broadcasted_iota(jnp.int32, (ck, ck), 0)
    ci = lax.broadcasted_iota(jnp.int32, (ck, ck), 1)
    lane_lo = ci < SSD_HEAD_DIM

    def conv_silu(j):
        n = ck + 2 * halo
        v = stg[...]
        cw = cw_ref[j]
        acc = cb_ref[j] + cw[1:2, :] * v
        acc = acc + cw[0:1, :] * pltpu.roll(v, 1, 0)
        acc = acc + cw[2:3, :] * pltpu.roll(v, n - 1, 0)
        acc = acc + cw[3:4, :] * pltpu.roll(v, n - 2, 0)
        return _silu(acc[halo:halo + ck, :])

    def store_x(j, row0):
        s = conv_silu(j)
        yacc[j, pl.ds(row0, ck), :] = dsk_ref[j] * s
        sb = s.astype(BF16)
        zb = jnp.zeros_like(sb)
        xm[j, pl.ds(row0, ck), :] = jnp.where(lane_lo, sb, zb)
        xm[nx + j, pl.ds(row0, ck), :] = jnp.where(lane_lo, zb, sb)

    def store_b(g, row0):
        st = conv_silu(nx + g).T
        sbtf[g, pl.ds(row0, ck), :] = st
        sbt[g, pl.ds(row0, ck), :] = st.astype(BF16)

    def store_c(g, row0):
        scc[g, pl.ds(row0, ck), :] = conv_silu(nx + ng + g).astype(BF16)

    kinds = ((store_x, 0, nx), (store_b, nx, ng), (store_c, nx + ng, ng))

    stg[0:halo, :] = jnp.zeros((halo, LANES), F32)
    stg[halo + ck:2 * halo + ck, :] = jnp.zeros((halo, LANES), F32)
    for k in range(nc_ctx):
        def ctx_fill(j, k=k):
            stg[halo:halo + ck, :] = xc_ref[0, j, k * ck:(k + 1) * ck, :]
            stg[halo - 1:halo, :] = xc_ref[0, j, k * ck - 1:k * ck, :] if k > 0 else zrow
            if k < nc_ctx - 1:
                stg[halo + ck:halo + ck + 2, :] = xc_ref[0, j, (k + 1) * ck:(k + 1) * ck + 2, :]
            else:
                stg[halo + ck:halo + ck + 2, :] = jnp.zeros((2, LANES), F32)

        for store, j0, cnt in kinds:
            def ctx_slab(i, carry, k=k, store=store, j0=j0, fill=ctx_fill):
                fill(j0 + i)
                store(i, k * ck)
                return carry

            lax.fori_loop(0, cnt, ctx_slab, 0)
    dtp[0:t_ctx, :] = dtc_ref[0]

    def lat_chunk(c, carry):
        w0 = c * wpc
        row0 = pl.multiple_of(t_ctx + c * ck, ck)
        for wl in range(wpc):
            dtp[pl.ds(row0 + wl * rows, rows), :] = dtl_ref[0, pl.ds(w0 + wl, rows, stride=GRID_W), :]

        def lat_fill(j):
            for wl in range(wpc):
                stg[halo + wl * rows:halo + (wl + 1) * rows, :] = (
                    xl_ref[0, j, pl.ds(w0 + wl, rows, stride=GRID_W), :])
            prev = xl_ref[0, j, pl.ds((rows - 1) * GRID_W + jnp.maximum(w0 - 1, 0), 1), :]
            stg[halo - 1:halo, :] = jnp.where(c > 0, prev, zrow)
            wn = jnp.minimum(w0 + wpc, GRID_W - 1)
            nxt = xl_ref[0, j, pl.ds(wn, 2, stride=GRID_W), :]
            stg[halo + ck:halo + ck + 2, :] = jnp.where(c < nc_lat - 1, nxt, jnp.zeros((2, LANES), F32))

        for store, j0, cnt in kinds:
            def lat_slab(i, carry2, store=store, j0=j0):
                lat_fill(j0 + i)
                store(i, row0)
                return carry2

            lax.fori_loop(0, cnt, lat_slab, 0)
        return carry

    lax.fori_loop(0, nc_lat, lat_chunk, 0)

    dtb = dtb_ref[...]
    lane_ok = lax.broadcasted_iota(jnp.int32, (1, LANES), 1) < 2 * SSD_HEADS
    nega2 = jnp.where(lane_ok, -jnp.exp(alog_ref[...]) * math.log2(math.e), 0.0)

    nck = nc_ctx + nc_lat
    nh2 = 2 * SSD_HEADS
    hpp = SSD_HEADS // ng // 2
    dir1_col = lax.broadcasted_iota(jnp.int32, (1, LANES), 1) >= SSD_HEADS
    tril = (ri >= ci).astype(F32)
    tris = (ri >= ci, ri <= ci)

    def xcat_of(j, r0):
        return jnp.concatenate([xm[j, pl.ds(r0, ck), :], xm[nx + j, pl.ds(r0, ck), :]], axis=0)

    def pass1(kc, carry):
        r0 = pl.multiple_of(kc * ck, ck)
        dt = _softplus(dtp[pl.ds(r0, ck), :] + dtb)
        la = dt * nega2
        cumf = jnp.dot(tril, la, precision=lax.Precision.HIGHEST, preferred_element_type=F32)
        tot = cumf[ck - 1:ck, :]
        cum = jnp.where(dir1_col, tot - cumf + la, cumf)
        ldt = jnp.log2(dt)
        dtp[pl.ds(r0, ck), :] = cum
        srct[pl.ds(pl.multiple_of(kc * nh2, nh2), nh2), :] = (cum - ldt).T[0:nh2, :]
        wgt_t = jnp.exp2((tot - cum + ldt).T[0:nh2, :])
        etb[pl.ds(pl.multiple_of(kc * SUBLANES, SUBLANES), SUBLANES), :] = jnp.broadcast_to(
            jnp.exp2(tot), (SUBLANES, LANES))
        for g in range(ng):
            btf = sbtf[g, pl.ds(r0, ck), :]
            for pr in range(hpp):
                j = g * hpp + pr
                xcat = xcat_of(j, r0)
                for d in range(2):
                    hc = (d * SSD_HEADS + 2 * j, d * SSD_HEADS + 2 * j + 1)
                    btw = jnp.concatenate([(btf * wgt_t[h:h + 1, :]).astype(BF16) for h in hc], axis=1)
                    sbuf[d * nx + j, pl.ds(r0, ck), :] = _dot(btw, xcat).astype(BF16)
        return carry

    lax.fori_loop(0, nck, pass1, 0, unroll=2)

    hst[...] = jnp.zeros(hst.shape, F32)

    def pass2(i, carry):
        for d in range(2):
            if d == 0:
                kc = i
            else:
                kc = jnp.where(i < nc_ctx, nc_ctx - 1 - i, nck - 1 - (i - nc_ctx))
            r0 = pl.multiple_of(kc * ck, ck)
            e = etb[pl.ds(pl.multiple_of(kc * SUBLANES, SUBLANES), 1), :]
            for j in range(nx):
                hc = (d * SSD_HEADS + 2 * j, d * SSD_HEADS + 2 * j + 1)
                et = jnp.where(lane_lo[0:1, :], jnp.broadcast_to(e[:, hc[0]:hc[0] + 1], (1, LANES)),
                               jnp.broadcast_to(e[:, hc[1]:hc[1] + 1], (1, LANES)))
                h = hst[d * nx + j]
                s = sbuf[d * nx + j, pl.ds(r0, ck), :].astype(F32)
                sbuf[d * nx + j, pl.ds(r0, ck), :] = h.astype(BF16)
                hst[d * nx + j] = h * et + s
        return carry

    lax.fori_loop(0, nck, pass2, 0)

    def pass3(kc, is_ctx):
        r0 = kc * ck if isinstance(kc, int) else pl.multiple_of(kc * ck, ck)
        s0 = kc * nh2 if isinstance(kc, int) else pl.multiple_of(kc * nh2, nh2)
        cum = dtp[pl.ds(r0, ck), :]
        src_t = srct[pl.ds(s0, nh2), :]
        for g in range(ng):
            cm = scc[g, pl.ds(r0, ck), :]
            cb = _dot(cm, sbt[g, pl.ds(r0, ck), :])
            for pr in range(hpp):
                j = g * hpp + pr
                xcat = xcat_of(j, r0)
                ms = []
                ecs = []
                for d in range(2):
                    hc = (d * SSD_HEADS + 2 * j, d * SSD_HEADS + 2 * j + 1)
                    col = [jnp.broadcast_to(cum[:, h:h + 1], (ck, LANES)) for h in hc]
                    ms += [(cb * jnp.exp2(jnp.where(tris[d], col[q] - src_t[hc[q]:hc[q] + 1, :], NEG_BIG))
                            ).astype(BF16) for q in range(2)]
                    ecs.append(jnp.exp2(jnp.where(lane_lo, col[0], col[1])))
                y = yacc[j, pl.ds(r0, ck), :] + _dot(jnp.concatenate(ms, axis=1),
                                                      jnp.concatenate([xcat, xcat], axis=0))
                hin = jnp.concatenate([sbuf[j, pl.ds(r0, ck), :], sbuf[nx + j, pl.ds(r0, ck), :]], axis=1)
                yoff = _dot(cm, hin)
                y = y + yoff[:, 0:LANES] * ecs[0] + yoff[:, LANES:2 * LANES] * ecs[1]
                if is_ctx:
                    yc_ref[0, j, pl.ds(r0, ck), :] = y
                else:
                    w0 = (kc - nc_ctx) * wpc
                    for wl in range(wpc):
                        yl_ref[0, j, pl.ds(w0 + wl, rows, stride=GRID_W), :] = y[wl * rows:(wl + 1) * rows, :]

    if need_ctx:
        for kc in range(nc_ctx):
            pass3(kc, True)

    def lat_body(c, carry):
        pass3(nc_ctx + c, False)
        return carry

    lax.fori_loop(0, nc_lat, lat_body, 0, unroll=2)


def _ssd(xbc_l, dt_l, xbc_c, dt_c, cw, cb, dtb, alog, dsk, *, need_ctx):
    b, nslab, t_lat, _ = xbc_l.shape
    t_ctx = xbc_c.shape[2]
    nx = SSD_HEADS * SSD_HEAD_DIM // LANES
    ng = SSD_GROUPS
    tot = t_lat + t_ctx
    kern = functools.partial(_ssd_kernel, t_lat=t_lat, t_ctx=t_ctx, need_ctx=need_ctx)
    slab = lambda n, t: pl.BlockSpec((1, n, t, LANES), lambda i: (i, 0, 0, 0))
    seq = lambda t: pl.BlockSpec((1, t, LANES), lambda i: (i, 0, 0))
    full = lambda a: pl.BlockSpec(a.shape, lambda i: (0,) * a.ndim)
    out_shape = [jax.ShapeDtypeStruct((b, nx, t_lat, LANES), F32)]
    out_specs = [slab(nx, t_lat)]
    if need_ctx:
        out_shape.append(jax.ShapeDtypeStruct((b, nx, t_ctx, LANES), F32))
        out_specs.append(slab(nx, t_ctx))
    res = pl.pallas_call(
        kern,
        out_shape=tuple(out_shape),
        grid=(b,),
        in_specs=[slab(nslab, t_lat), seq(t_lat), slab(nslab, t_ctx), seq(t_ctx),
                  full(cw), full(cb), full(dtb), full(alog), full(dsk)],
        out_specs=tuple(out_specs),
        scratch_shapes=[
            pltpu.VMEM((SSD_CHUNK + 2 * SUBLANES, LANES), F32),
            pltpu.VMEM((2 * nx, tot, LANES), BF16),
            pltpu.VMEM((ng, tot, LANES), BF16),
            pltpu.VMEM((ng, tot, LANES), BF16),
            pltpu.VMEM((ng, tot, LANES), F32),
            pltpu.VMEM((tot, LANES), F32),
            pltpu.VMEM((tot // SSD_CHUNK * 2 * SSD_HEADS, LANES), F32),
            pltpu.VMEM((tot // SSD_CHUNK * SUBLANES, LANES), F32),
            pltpu.VMEM((2 * nx, tot, LANES), BF16),
            pltpu.VMEM((2 * nx, SSD_STATE, LANES), F32),
            pltpu.VMEM((nx, tot, LANES), F32),
        ],
        compiler_params=_cparams("arbitrary"),
        name="ssd",
    )(xbc_l, dt_l, xbc_c, dt_c, cw, cb, dtb, alog, dsk)
    return (res[0], res[1]) if need_ctx else (res[0], None)


def _outmlp_kernel(*args, final_norm, ff_chunk):
    if final_norm:
        (x_ref, lru_ref, ssd_ref, z_ref, mod_ref, sg_ref, n2_ref, wo_ref, w1_ref, w2_ref, fg_ref, o_ref) = args
    else:
        (x_ref, lru_ref, ssd_ref, z_ref, mod_ref, sg_ref, n2_ref, wo_ref, w1_ref, w2_ref, o_ref) = args
    x = x_ref[0]
    nx = ssd_ref.shape[1]
    z = z_ref[0]
    lw = z.shape[-1]
    gw = lw // SSD_GROUPS
    spg = nx // SSD_GROUPS
    parts = []
    for g in range(SSD_GROUPS):
        y = jnp.concatenate([ssd_ref[0, g * spg + k] for k in range(spg)], axis=1)
        y = y * _silu(z[:, g * gw:(g + 1) * gw])
        y = y * lax.rsqrt(jnp.mean(y * y, axis=-1, keepdims=True) + EPS)
        parts.append(y * sg_ref[:, g * gw:(g + 1) * gw])
    ssd_n = jnp.concatenate(parts, axis=1).astype(BF16)
    lru = lru_ref[0].astype(BF16)
    y = _dot(lru, wo_ref[0:lw, :]) + _dot(ssd_n, wo_ref[lw:2 * lw, :])
    x1 = x + mod_ref[0, 2:3, :] * y
    h = (_rmsnorm(x1, n2_ref[...]) * (1.0 + mod_ref[0, 4:5, :]) + mod_ref[0, 3:4, :]).astype(BF16)
    dff = w1_ref.shape[1]
    acc = jnp.zeros_like(x1)
    for j in range(dff // ff_chunk):
        a = _dot(h, w1_ref[:, j * ff_chunk:(j + 1) * ff_chunk])
        a = jnp.square(jnp.maximum(a, 0.0)).astype(BF16)
        acc = acc + _dot(a, w2_ref[j * ff_chunk:(j + 1) * ff_chunk, :])
    x2 = x1 + mod_ref[0, 5:6, :] * acc
    if final_norm:
        x2 = _rmsnorm(x2, fg_ref[...])
    o_ref[0] = x2


def _outmlp(x, lru, ssd, z, mod, mod_row, sg, n2, wo, w1, w2, fg, *, tm, ff_chunk=1024):
    b, t, d = x.shape
    lw = lru.shape[-1]
    nx = ssd.shape[1]
    final_norm = fg is not None
    kern = functools.partial(_outmlp_kernel, final_norm=final_norm, ff_chunk=ff_chunk)
    tok = lambda n: pl.BlockSpec((1, tm, n), lambda i, j: (i, j, 0))
    res = lambda a: pl.BlockSpec(a.shape, lambda i, j: (0, 0), pipeline_mode=pl.Buffered(1))
    row = lambda n: pl.BlockSpec((1, n), lambda i, j: (0, 0))
    ins = [x, lru, ssd, z, mod, sg, n2, wo, w1, w2]
    in_specs = [tok(d), tok(lw), pl.BlockSpec((1, nx, tm, LANES), lambda i, j: (i, 0, j, 0)), tok(lw),
                pl.BlockSpec((1, N_MOD, d), lambda i, j: (mod_row(i), 0, 0)),
                row(lw), row(d), res(wo), res(w1), res(w2)]
    if final_norm:
        ins.append(fg)
        in_specs.append(row(d))
    return pl.pallas_call(
        kern,
        out_shape=jax.ShapeDtypeStruct((b, t, d), F32),
        grid=(b, t // tm),
        in_specs=in_specs,
        out_specs=tok(d),
        compiler_params=_cparams("arbitrary", "arbitrary"),
        name="outmlp",
    )(*ins)


def _pair_block_diag(w):
    two, h, k, _ = w.shape
    w = w.reshape(two, h // 2, 2, k, k)
    z = jnp.zeros_like(w[:, :, 0])
    top = jnp.concatenate([w[:, :, 0], z], axis=-1)
    bot = jnp.concatenate([z, w[:, :, 1]], axis=-1)
    return jnp.concatenate([top, bot], axis=-2)


def _pad_lanes(a):
    return jnp.pad(a, [(0, 0)] * (a.ndim - 1) + [(0, LANES - a.shape[-1])])


def kernel(x, c, ctx, c_ctx, ada_w, ada_b, norm1_g, norm2_g, w_in, lru_conv_w, lru_conv_b, lru_wa, lru_ba,
           lru_wx, lru_bx, lru_lambda, ssd_conv_w, ssd_conv_b, ssd_dt_bias, ssd_a_log, ssd_d, ssd_norm_g,
           w_out, mlp_w1, mlp_w2, final_g):
    bsz, t_lat, d = x.shape
    t_ctx = ctx.shape[1]
    depth = ada_w.shape[0]
    lw = lru_conv_w.shape[-1]
    nxbc = ssd_conv_w.shape[-1]
    ndt = 2 * SSD_HEADS
    nscan = lw + nxbc + ndt
    assert bsz + 1 <= MOD_ROWS and t_lat % (GRID_W * SUBLANES) == 0 and t_ctx % SSD_CHUNK == 0

    cvec = jnp.zeros((MOD_ROWS, d), F32).at[:bsz].set(c).at[bsz].set(c_ctx)
    mod = _modulation(cvec, ada_w, ada_b).reshape(depth, MOD_ROWS, N_MOD, d)

    lat_row = lambda i: i
    ctx_row = lambda i: bsz
    tm_lat = 512
    tm_ctx = t_ctx
    nslab = nxbc // LANES

    for l in range(depth):
        need_ctx = l < depth - 1
        wl = w_in[l]
        w_re = jnp.concatenate(
            [wl[:, :lw + nxbc], wl[:, nscan:], _pad_lanes(wl[:, lw + nxbc:nscan])], axis=1).astype(BF16)
        g1 = norm1_g[l].reshape(1, d)
        lx_l, xbc_l, lg_l, z_l, dt_l = _inproj(x, mod[l], lat_row, g1, w_re, lw=lw, nxbc=nxbc, tm=tm_lat)
        lx_c, xbc_c, lg_c, z_c, dt_c = _inproj(ctx, mod[l], ctx_row, g1, w_re, lw=lw, nxbc=nxbc, tm=tm_ctx)

        lru_l, lru_c = _lru(
            lx_l, lx_c, lg_l, lg_c, lru_conv_w[l], lru_conv_b[l].reshape(1, lw),
            _pair_block_diag(lru_wa[l]).astype(BF16), _pair_block_diag(lru_wx[l]).astype(BF16),
            lru_ba[l], lru_bx[l], lru_lambda[l], need_ctx=need_ctx)

        cw = ssd_conv_w[l].reshape(CONV_K, nslab, LANES).transpose(1, 0, 2)
        cb = ssd_conv_b[l].reshape(nslab, 1, LANES)
        dtb = _pad_lanes(ssd_dt_bias[l].reshape(1, ndt))
        alog = _pad_lanes(ssd_a_log[l].reshape(1, ndt))
        dsk = jnp.repeat(ssd_d[l], SSD_HEAD_DIM).reshape(lw // LANES, 1, LANES)
        ssd_l, ssd_c = _ssd(xbc_l, dt_l, xbc_c, dt_c, cw, cb, dtb, alog, dsk, need_ctx=need_ctx)

        sg = ssd_norm_g[l].reshape(1, lw)
        n2 = norm2_g[l].reshape(1, d)
        wo = w_out[l].astype(BF16)
        w1 = mlp_w1[l].astype(BF16)
        w2 = mlp_w2[l].astype(BF16)
        fg = final_g.reshape(1, d) if l == depth - 1 else None
        x = _outmlp(x, lru_l, ssd_l, z_l, mod[l], lat_row, sg, n2, wo, w1, w2, fg, tm=tm_lat)
        if need_ctx:
            ctx = _outmlp(ctx, lru_c, ssd_c, z_c, mod[l], ctx_row, sg, n2, wo, w1, w2, None, tm=tm_ctx)
    return x
```

```python
import functools
import math

import jax
import jax.numpy as jnp
from jax import lax
from jax.experimental import pallas as pl
from jax.experimental.pallas import tpu as pltpu

F32 = jnp.float32
BF16 = jnp.bfloat16

EPS = 1e-6
GRID_W = 64
LRU_BLOCK = 64
LRU_C = 8.0
SSD_HEAD_DIM = 64
SSD_HEADS = 8
SSD_GROUPS = 2
SSD_STATE = 128
SSD_CHUNK = 128
CONV_K = 4
N_MOD = 6
LANES = 128
SUBLANES = 8
MOD_ROWS = 16
VMEM_LIMIT = 56 * 1024 * 1024
NEG_BIG = -1e30


def _sigmoid(x):
    return 0.5 * (jnp.tanh(0.5 * x) + 1.0)


def _silu(x):
    return x * _sigmoid(x)


def _softplus(x):
    return jnp.maximum(x, 0.0) + jnp.log1p(jnp.exp(-jnp.abs(x)))


def _gelu_tanh(x):
    c = math.sqrt(2.0 / math.pi)
    return (0.5 * x) * (1.0 + jnp.tanh(x * (c + (c * 0.044715) * (x * x))))


def _rmsnorm(x, g):
    return x * lax.rsqrt(jnp.mean(x * x, axis=-1, keepdims=True) + EPS) * g


def _dot(a, b):
    return jnp.dot(a, b, preferred_element_type=F32)


def _cparams(*sem):
    return pltpu.CompilerParams(dimension_semantics=sem, vmem_limit_bytes=VMEM_LIMIT)


def _mod_kernel(c_ref, w_ref, b_ref, o_ref):
    s = _silu(c_ref[...]).astype(BF16)
    o_ref[0] = _dot(s, w_ref[0].astype(BF16)) + b_ref[0]


def _modulation(cvec, ada_w, ada_b):
    depth, d, n = ada_w.shape
    tn = 512
    return pl.pallas_call(
        _mod_kernel,
        out_shape=jax.ShapeDtypeStruct((depth, MOD_ROWS, n), F32),
        grid=(depth, n // tn),
        in_specs=[
            pl.BlockSpec((MOD_ROWS, d), lambda l, j: (0, 0)),
            pl.BlockSpec((1, d, tn), lambda l, j: (l, 0, j)),
            pl.BlockSpec((1, 1, tn), lambda l, j: (l, 0, j)),
        ],
        out_specs=pl.BlockSpec((1, MOD_ROWS, tn), lambda l, j: (l, 0, j)),
        compiler_params=_cparams("arbitrary", "arbitrary"),
        name="modulation",
    )(cvec, ada_w, ada_b.reshape(depth, 1, n))


def _inproj_kernel(x_ref, mod_ref, g_ref, w_ref, lx_ref, xbc_ref, lg_ref, z_ref, dt_ref, *, lw, nxbc):
    x = x_ref[0]
    h = _rmsnorm(x, g_ref[...]) * (1.0 + mod_ref[0, 1:2, :]) + mod_ref[0, 0:1, :]
    h = h.astype(BF16)
    lx_ref[0] = _dot(h, w_ref[:, 0:lw])
    xbc = _dot(h, w_ref[:, lw:lw + nxbc])
    for j in range(nxbc // LANES):
        xbc_ref[0, j] = xbc[:, j * LANES:(j + 1) * LANES]
    o = lw + nxbc
    lg_ref[0] = _dot(h, w_ref[:, o:o + lw])
    z_ref[0] = _dot(h, w_ref[:, o + lw:o + 2 * lw])
    dt_ref[0] = _dot(h, w_ref[:, o + 2 * lw:o + 2 * lw + LANES])


def _inproj(x, mod, mod_row, g, w, *, lw, nxbc, tm):
    b, t, d = x.shape
    ns = nxbc // LANES
    kern = functools.partial(_inproj_kernel, lw=lw, nxbc=nxbc)
    tok = lambda n: pl.BlockSpec((1, tm, n), lambda i, j: (i, j, 0))
    return pl.pallas_call(
        kern,
        out_shape=(
            jax.ShapeDtypeStruct((b, t, lw), F32),
            jax.ShapeDtypeStruct((b, ns, t, LANES), F32),
            jax.ShapeDtypeStruct((b, t, lw), F32),
            jax.ShapeDtypeStruct((b, t, lw), F32),
            jax.ShapeDtypeStruct((b, t, LANES), F32),
        ),
        grid=(b, t // tm),
        in_specs=[
            tok(d),
            pl.BlockSpec((1, N_MOD, d), lambda i, j: (mod_row(i), 0, 0)),
            pl.BlockSpec((1, d), lambda i, j: (0, 0)),
            pl.BlockSpec(w.shape, lambda i, j: (0, 0), pipeline_mode=pl.Buffered(1)),
        ],
        out_specs=(
            tok(lw),
            pl.BlockSpec((1, ns, tm, LANES), lambda i, j: (i, 0, j, 0)),
            tok(lw),
            tok(lw),
            tok(LANES),
        ),
        compiler_params=_cparams("arbitrary", "arbitrary"),
        name="inproj",
    )(x, mod, g, w)


LRU_ROWS = 256


def _lru_seq(t, u_ref, lg_ref, out_ref, h0, refs, scr, need_out):
    cw_ref, cb_ref, wa_ref, wx_ref, ba_ref, bx_ref, lam_ref = refs
    upad, ubuf, uint, hbuf, pbuf, ybuf = scr
    seg = t // SUBLANES
    pitch = seg + 4
    nblk = max(t // LRU_ROWS, 1)
    rb = t // nblk
    nv = rb // SUBLANES

    zeros8 = jnp.zeros((SUBLANES, LANES), F32)
    upad[0:SUBLANES, :] = zeros8
    upad[SUBLANES:SUBLANES + t, :] = u_ref[...]
    upad[SUBLANES + t:2 * SUBLANES + t, :] = zeros8

    cw = cw_ref[...]
    n = seg + 2 * SUBLANES
    for s in range(SUBLANES):
        v = upad[s * seg:s * seg + n, :]
        acc = cb_ref[...] + cw[1:2, :] * v
        acc = acc + cw[0:1, :] * pltpu.roll(v, 1, 0)
        acc = acc + cw[2:3, :] * pltpu.roll(v, n - 1, 0)
        acc = acc + cw[3:4, :] * pltpu.roll(v, n - 2, 0)
        ubuf[s * pitch:s * pitch + seg, :] = acc[SUBLANES:SUBLANES + seg, :]

    for tau in range(seg):
        uint[tau * SUBLANES:(tau + 1) * SUBLANES, :] = ubuf[pl.ds(tau, SUBLANES, stride=pitch), :]

    row = lax.broadcasted_iota(jnp.int32, (SUBLANES, LANES), 0)
    finals = []
    cmats = []
    kdec = [(-0.5 * LRU_C) * _softplus(-lam_ref[d:d + 1, :]) for d in range(2)]

    def blk_body(i, carry):
        carry = list(carry)
        for d in range(2):
            h, p = carry[2 * d], carry[2 * d + 1]
            blk = i if d == 0 else nblk - 1 - i
            r0 = pl.multiple_of(blk * rb, rb)
            u = uint[pl.ds(r0, rb), :]
            ub = u.astype(BF16)
            log_a = kdec[d] * jnp.tanh(_dot(ub, wa_ref[d, 0]) + ba_ref[d:d + 1, :]) + kdec[d]
            gate = jnp.tanh(_dot(ub, wx_ref[d, 0]) + bx_ref[d:d + 1, :]) + 1.0
            a = jnp.exp(log_a)
            v = jnp.tanh(-log_a) * (1.0 + a * a)
            root = jnp.where(v > 0.0, v * lax.rsqrt(v), 0.0)
            bb = root * (gate * (0.5 * u))
            hs = [None] * nv
            ps = [None] * nv
            order = range(nv) if d == 0 else range(nv - 1, -1, -1)
            for k in order:
                ak = a[k * SUBLANES:(k + 1) * SUBLANES, :]
                h = ak * h + bb[k * SUBLANES:(k + 1) * SUBLANES, :]
                p = ak * p
                hs[k] = h
                ps[k] = p
            hbuf[d, pl.ds(r0, rb), :] = jnp.concatenate(hs, axis=0)
            pbuf[d, pl.ds(r0, rb), :] = jnp.concatenate(ps, axis=0)
            carry[2 * d], carry[2 * d + 1] = h, p
        return tuple(carry)

    ones8 = jnp.ones((SUBLANES, LANES), F32)
    hp = lax.fori_loop(0, nblk, blk_body, (zeros8, ones8, zeros8, ones8), unroll=2 if nblk % 2 == 0 else 1)
    for d in range(2):
        h, p = hp[2 * d], hp[2 * d + 1]
        c = h0[d]
        cmat = zeros8
        order = range(SUBLANES) if d == 0 else range(SUBLANES - 1, -1, -1)
        for s in order:
            cmat = jnp.where(row == s, jnp.broadcast_to(c, (SUBLANES, LANES)), cmat)
            c = h[s:s + 1, :] + p[s:s + 1, :] * c
        finals.append(c)
        cmats.append(cmat)

    if need_out:
        def out_body(i, carry):
            r0 = pl.multiple_of(i * rb, rb)
            y = hbuf[0, pl.ds(r0, rb), :].reshape(nv, SUBLANES, LANES)
            y = y + pbuf[0, pl.ds(r0, rb), :].reshape(nv, SUBLANES, LANES) * cmats[0][None]
            y = y + hbuf[1, pl.ds(r0, rb), :].reshape(nv, SUBLANES, LANES)
            y = y + pbuf[1, pl.ds(r0, rb), :].reshape(nv, SUBLANES, LANES) * cmats[1][None]
            ybuf[pl.ds(r0, rb), :] = y.reshape(rb, LANES)
            return carry

        lax.fori_loop(0, nblk, out_body, 0)
        for tau in range(seg):
            ubuf[pl.ds(tau, SUBLANES, stride=pitch), :] = ybuf[tau * SUBLANES:(tau + 1) * SUBLANES, :]
        for s in range(SUBLANES):
            out_ref[s * seg:(s + 1) * seg, :] = (
                ubuf[s * pitch:s * pitch + seg, :] * _gelu_tanh(lg_ref[s * seg:(s + 1) * seg, :]))
    return finals


def _lru_kernel(*args, t_lat, t_ctx, need_ctx):
    if need_ctx:
        (ul_ref, uc_ref, lgl_ref, lgc_ref, *refs, yl_ref, yc_ref, upad, ubuf, uint, hbuf, pbuf, ybuf) = args
    else:
        (ul_ref, uc_ref, lgl_ref, *refs, yl_ref, upad, ubuf, uint, hbuf, pbuf, ybuf) = args
        lgc_ref = yc_ref = None
    scr = (upad, ubuf, uint, hbuf, pbuf, ybuf)
    zero = jnp.zeros((1, LANES), F32)
    ctx_final = _lru_seq(t_ctx, uc_ref.at[0], lgc_ref.at[0] if need_ctx else None,
                         yc_ref.at[0] if need_ctx else None, (zero, zero), refs, scr, need_ctx)
    _lru_seq(t_lat, ul_ref.at[0], lgl_ref.at[0], yl_ref.at[0], ctx_final, refs, scr, True)


def _lru(lx_l, lx_c, lg_l, lg_c, conv_w, conv_b, wa_bd, wx_bd, ba, bx, lam, *, need_ctx):
    b, t_lat, w = lx_l.shape
    t_ctx = lx_c.shape[1]
    nt = w // LANES
    seq = lambda t: pl.BlockSpec((1, t, LANES), lambda i, j: (i, 0, j))
    par = lambda r: pl.BlockSpec((r, LANES), lambda i, j: (0, j))
    gate = pl.BlockSpec((2, 1, LANES, LANES), lambda i, j: (0, j, 0, 0))
    kern = functools.partial(_lru_kernel, t_lat=t_lat, t_ctx=t_ctx, need_ctx=need_ctx)
    ins = [lx_l, lx_c, lg_l] + ([lg_c] if need_ctx else [])
    in_specs = [seq(t_lat), seq(t_ctx), seq(t_lat)] + ([seq(t_ctx)] if need_ctx else [])
    ins += [conv_w, conv_b, wa_bd, wx_bd, ba, bx, lam]
    in_specs += [par(CONV_K), par(1), gate, gate, par(2), par(2), par(2)]
    out_shape = [jax.ShapeDtypeStruct((b, t_lat, w), F32)]
    out_specs = [seq(t_lat)]
    if need_ctx:
        out_shape.append(jax.ShapeDtypeStruct((b, t_ctx, w), F32))
        out_specs.append(seq(t_ctx))
    pitch_rows = SUBLANES * (t_lat // SUBLANES + 4) + SUBLANES
    res = pl.pallas_call(
        kern,
        out_shape=tuple(out_shape),
        grid=(b, nt),
        in_specs=in_specs,
        out_specs=tuple(out_specs),
        scratch_shapes=[
            pltpu.VMEM((t_lat + 2 * SUBLANES, LANES), F32),
            pltpu.VMEM((pitch_rows, LANES), F32),
            pltpu.VMEM((t_lat, LANES), F32),
            pltpu.VMEM((2, t_lat, LANES), F32),
            pltpu.VMEM((2, t_lat, LANES), F32),
            pltpu.VMEM((t_lat, LANES), F32),
        ],
        compiler_params=_cparams("arbitrary", "arbitrary"),
        name="rglru",
    )(*ins)
    return (res[0], res[1]) if need_ctx else (res[0], None)


def _ssd_kernel(*args, t_lat, t_ctx, need_ctx):
    if need_ctx:
        (xl_ref, dtl_ref, xc_ref, dtc_ref, cw_ref, cb_ref, dtb_ref, alog_ref, dsk_ref,
         yl_ref, yc_ref, stg, xm, scc, sbt, sbtf, dtp, srct, etb, sbuf, hst, yacc) = args
    else:
        (xl_ref, dtl_ref, xc_ref, dtc_ref, cw_ref, cb_ref, dtb_ref, alog_ref, dsk_ref,
         yl_ref, stg, xm, scc, sbt, sbtf, dtp, srct, etb, sbuf, hst, yacc) = args
        yc_ref = None
    ck = SSD_CHUNK
    rows = t_lat // GRID_W
    wpc = ck // rows
    nc_ctx = t_ctx // ck
    nc_lat = t_lat // ck
    nx = SSD_HEADS * SSD_HEAD_DIM // LANES
    ng = SSD_GROUPS
    halo = SUBLANES

    zrow = jnp.zeros((1, LANES), F32)
    ri = lax.broadcasted_iota(jnp.int32, (ck, ck), 0)
    ci = lax.broadcasted_iota(jnp.int32, (ck, ck), 1)
    lane_lo = ci < SSD_HEAD_DIM

    def conv_silu(j):
        n = ck + 2 * halo
        v = stg[...]
        cw = cw_ref[j]
        acc = cb_ref[j] + cw[1:2, :] * v
        acc = acc + cw[0:1, :] * pltpu.roll(v, 1, 0)
        acc = acc + cw[2:3, :] * pltpu.roll(v, n - 1, 0)
        acc = acc + cw[3:4, :] * pltpu.roll(v, n - 2, 0)
        return _silu(acc[halo:halo + ck, :])

    def store_x(j, row0):
        s = conv_silu(j)
        yacc[j, pl.ds(row0, ck), :] = dsk_ref[j] * s
        sb = s.astype(BF16)
        zb = jnp.zeros_like(sb)
        xm[j, pl.ds(row0, ck), :] = jnp.where(lane_lo, sb, zb)
        xm[nx + j, pl.ds(row0, ck), :] = jnp.where(lane_lo, zb, sb)

    def store_b(g, row0):
        st = conv_silu(nx + g).T
        sbtf[g, pl.ds(row0, ck), :] = st
        sbt[g, pl.ds(row0, ck), :] = st.astype(BF16)

    def store_c(g, row0):
        scc[g, pl.ds(row0, ck), :] = conv_silu(nx + ng + g).astype(BF16)

    kinds = ((store_x, 0, nx), (store_b, nx, ng), (store_c, nx + ng, ng))

    stg[0:halo, :] = jnp.zeros((halo, LANES), F32)
    stg[halo + ck:2 * halo + ck, :] = jnp.zeros((halo, LANES), F32)
    for k in range(nc_ctx):
        def ctx_fill(j, k=k):
            stg[halo:halo + ck, :] = xc_ref[0, j, k * ck:(k + 1) * ck, :]
            stg[halo - 1:halo, :] = xc_ref[0, j, k * ck - 1:k * ck, :] if k > 0 else zrow
            if k < nc_ctx - 1:
                stg[halo + ck:halo + ck + 2, :] = xc_ref[0, j, (k + 1) * ck:(k + 1) * ck + 2, :]
            else:
                stg[halo + ck:halo + ck + 2, :] = jnp.zeros((2, LANES), F32)

        for store, j0, cnt in kinds:
            def ctx_slab(i, carry, k=k, store=store, j0=j0, fill=ctx_fill):
                fill(j0 + i)
                store(i, k * ck)
                return carry

            lax.fori_loop(0, cnt, ctx_slab, 0)
    dtp[0:t_ctx, :] = dtc_ref[0]

    def lat_chunk(c, carry):
        w0 = c * wpc
        row0 = pl.multiple_of(t_ctx + c * ck, ck)
        for wl in range(wpc):
            dtp[pl.ds(row0 + wl * rows, rows), :] = dtl_ref[0, pl.ds(w0 + wl, rows, stride=GRID_W), :]

        def lat_fill(j):
            for wl in range(wpc):
                stg[halo + wl * rows:halo + (wl + 1) * rows, :] = (
                    xl_ref[0, j, pl.ds(w0 + wl, rows, stride=GRID_W), :])
            prev = xl_ref[0, j, pl.ds((rows - 1) * GRID_W + jnp.maximum(w0 - 1, 0), 1), :]
            stg[halo - 1:halo, :] = jnp.where(c > 0, prev, zrow)
            wn = jnp.minimum(w0 + wpc, GRID_W - 1)
            nxt = xl_ref[0, j, pl.ds(wn, 2, stride=GRID_W), :]
            stg[halo + ck:halo + ck + 2, :] = jnp.where(c < nc_lat - 1, nxt, jnp.zeros((2, LANES), F32))

        for store, j0, cnt in kinds:
            def lat_slab(i, carry2, store=store, j0=j0):
                lat_fill(j0 + i)
                store(i, row0)
                return carry2

            lax.fori_loop(0, cnt, lat_slab, 0)
        return carry

    lax.fori_loop(0, nc_lat, lat_chunk, 0)

    dtb = dtb_ref[...]
    lane_ok = lax.broadcasted_iota(jnp.int32, (1, LANES), 1) < 2 * SSD_HEADS
    nega2 = jnp.where(lane_ok, -jnp.exp(alog_ref[...]) * math.log2(math.e), 0.0)

    nck = nc_ctx + nc_lat
    nh2 = 2 * SSD_HEADS
    hpp = SSD_HEADS // ng // 2
    dir1_col = lax.broadcasted_iota(jnp.int32, (1, LANES), 1) >= SSD_HEADS
    tril = (ri >= ci).astype(F32)
    tris = (ri >= ci, ri <= ci)

    def xcat_of(j, r0):
        return jnp.concatenate([xm[j, pl.ds(r0, ck), :], xm[nx + j, pl.ds(r0, ck), :]], axis=0)

    def pass1(kc, carry):
        r0 = pl.multiple_of(kc * ck, ck)
        dt = _softplus(dtp[pl.ds(r0, ck), :] + dtb)
        la = dt * nega2
        cumf = jnp.dot(tril, la, precision=lax.Precision.HIGHEST, preferred_element_type=F32)
        tot = cumf[ck - 1:ck, :]
        cum = jnp.where(dir1_col, tot - cumf + la, cumf)
        ldt = jnp.log2(dt)
        dtp[pl.ds(r0, ck), :] = cum
        srct[pl.ds(pl.multiple_of(kc * nh2, nh2), nh2), :] = (cum - ldt).T[0:nh2, :]
        wgt_t = jnp.exp2((tot - cum + ldt).T[0:nh2, :])
        etb[pl.ds(pl.multiple_of(kc * SUBLANES, SUBLANES), SUBLANES), :] = jnp.broadcast_to(
            jnp.exp2(tot), (SUBLANES, LANES))
        for g in range(ng):
            btf = sbtf[g, pl.ds(r0, ck), :]
            for pr in range(hpp):
                j = g * hpp + pr
                xcat = xcat_of(j, r0)
                for d in range(2):
                    hc = (d * SSD_HEADS + 2 * j, d * SSD_HEADS + 2 * j + 1)
                    btw = jnp.concatenate([(btf * wgt_t[h:h + 1, :]).astype(BF16) for h in hc], axis=1)
                    sbuf[d * nx + j, pl.ds(r0, ck), :] = _dot(btw, xcat).astype(BF16)
        return carry

    lax.fori_loop(0, nck, pass1, 0, unroll=2)

    hst[...] = jnp.zeros(hst.shape, F32)

    def pass2(i, carry):
        for d in range(2):
            if d == 0:
                kc = i
            else:
                kc = jnp.where(i < nc_ctx, nc_ctx - 1 - i, nck - 1 - (i - nc_ctx))
            r0 = pl.multiple_of(kc * ck, ck)
            e = etb[pl.ds(pl.multiple_of(kc * SUBLANES, SUBLANES), 1), :]
            for j in range(nx):
                hc = (d * SSD_HEADS + 2 * j, d * SSD_HEADS + 2 * j + 1)
                et = jnp.where(lane_lo[0:1, :], jnp.broadcast_to(e[:, hc[0]:hc[0] + 1], (1, LANES)),
                               jnp.broadcast_to(e[:, hc[1]:hc[1] + 1], (1, LANES)))
                h = hst[d * nx + j]
                s = sbuf[d * nx + j, pl.ds(r0, ck), :].astype(F32)
                sbuf[d * nx + j, pl.ds(r0, ck), :] = h.astype(BF16)
                hst[d * nx + j] = h * et + s
        return carry

    lax.fori_loop(0, nck, pass2, 0)

    def pass3(kc, is_ctx):
        r0 = kc * ck if isinstance(kc, int) else pl.multiple_of(kc * ck, ck)
        s0 = kc * nh2 if isinstance(kc, int) else pl.multiple_of(kc * nh2, nh2)
        cum = dtp[pl.ds(r0, ck), :]
        src_t = srct[pl.ds(s0, nh2), :]
        for g in range(ng):
            cm = scc[g, pl.ds(r0, ck), :]
            cb = _dot(cm, sbt[g, pl.ds(r0, ck), :])
            for pr in range(hpp):
                j = g * hpp + pr
                xcat = xcat_of(j, r0)
                ms = []
                ecs = []
                for d in range(2):
                    hc = (d * SSD_HEADS + 2 * j, d * SSD_HEADS + 2 * j + 1)
                    col = [jnp.broadcast_to(cum[:, h:h + 1], (ck, LANES)) for h in hc]
                    ms += [(cb * jnp.exp2(jnp.where(tris[d], col[q] - src_t[hc[q]:hc[q] + 1, :], NEG_BIG))
                            ).astype(BF16) for q in range(2)]
                    ecs.append(jnp.exp2(jnp.where(lane_lo, col[0], col[1])))
                y = yacc[j, pl.ds(r0, ck), :] + _dot(jnp.concatenate(ms, axis=1),
                                                      jnp.concatenate([xcat, xcat], axis=0))
                hin = jnp.concatenate([sbuf[j, pl.ds(r0, ck), :], sbuf[nx + j, pl.ds(r0, ck), :]], axis=1)
                yoff = _dot(cm, hin)
                y = y + yoff[:, 0:LANES] * ecs[0] + yoff[:, LANES:2 * LANES] * ecs[1]
                if is_ctx:
                    yc_ref[0, j, pl.ds(r0, ck), :] = y
                else:
                    w0 = (kc - nc_ctx) * wpc
                    for wl in range(wpc):
                        yl_ref[0, j, pl.ds(w0 + wl, rows, stride=GRID_W), :] = y[wl * rows:(wl + 1) * rows, :]

    if need_ctx:
        for kc in range(nc_ctx):
            pass3(kc, True)

    def lat_body(c, carry):
        pass3(nc_ctx + c, False)
        return carry

    lax.fori_loop(0, nc_lat, lat_body, 0, unroll=2)


def _ssd(xbc_l, dt_l, xbc_c, dt_c, cw, cb, dtb, alog, dsk, *, need_ctx):
    b, nslab, t_lat, _ = xbc_l.shape
    t_ctx = xbc_c.shape[2]
    nx = SSD_HEADS * SSD_HEAD_DIM // LANES
    ng = SSD_GROUPS
    tot = t_lat + t_ctx
    kern = functools.partial(_ssd_kernel, t_lat=t_lat, t_ctx=t_ctx, need_ctx=need_ctx)
    slab = lambda n, t: pl.BlockSpec((1, n, t, LANES), lambda i: (i, 0, 0, 0))
    seq = lambda t: pl.BlockSpec((1, t, LANES), lambda i: (i, 0, 0))
    full = lambda a: pl.BlockSpec(a.shape, lambda i: (0,) * a.ndim)
    out_shape = [jax.ShapeDtypeStruct((b, nx, t_lat, LANES), F32)]
    out_specs = [slab(nx, t_lat)]
    if need_ctx:
        out_shape.append(jax.ShapeDtypeStruct((b, nx, t_ctx, LANES), F32))
        out_specs.append(slab(nx, t_ctx))
    res = pl.pallas_call(
        kern,
        out_shape=tuple(out_shape),
        grid=(b,),
        in_specs=[slab(nslab, t_lat), seq(t_lat), slab(nslab, t_ctx), seq(t_ctx),
                  full(cw), full(cb), full(dtb), full(alog), full(dsk)],
        out_specs=tuple(out_specs),
        scratch_shapes=[
            pltpu.VMEM((SSD_CHUNK + 2 * SUBLANES, LANES), F32),
            pltpu.VMEM((2 * nx, tot, LANES), BF16),
            pltpu.VMEM((ng, tot, LANES), BF16),
            pltpu.VMEM((ng, tot, LANES), BF16),
            pltpu.VMEM((ng, tot, LANES), F32),
            pltpu.VMEM((tot, LANES), F32),
            pltpu.VMEM((tot // SSD_CHUNK * 2 * SSD_HEADS, LANES), F32),
            pltpu.VMEM((tot // SSD_CHUNK * SUBLANES, LANES), F32),
            pltpu.VMEM((2 * nx, tot, LANES), BF16),
            pltpu.VMEM((2 * nx, SSD_STATE, LANES), F32),
            pltpu.VMEM((nx, tot, LANES), F32),
        ],
        compiler_params=_cparams("arbitrary"),
        name="ssd",
    )(xbc_l, dt_l, xbc_c, dt_c, cw, cb, dtb, alog, dsk)
    return (res[0], res[1]) if need_ctx else (res[0], None)


def _outmlp_kernel(*args, final_norm, ff_chunk):
    if final_norm:
        (x_ref, lru_ref, ssd_ref, z_ref, mod_ref, sg_ref, n2_ref, wo_ref, w1_ref, w2_ref, fg_ref, o_ref) = args
    else:
        (x_ref, lru_ref, ssd_ref, z_ref, mod_ref, sg_ref, n2_ref, wo_ref, w1_ref, w2_ref, o_ref) = args
    x = x_ref[0]
    nx = ssd_ref.shape[1]
    z = z_ref[0]
    lw = z.shape[-1]
    gw = lw // SSD_GROUPS
    spg = nx // SSD_GROUPS
    parts = []
    for g in range(SSD_GROUPS):
        y = jnp.concatenate([ssd_ref[0, g * spg + k] for k in range(spg)], axis=1)
        y = y * _silu(z[:, g * gw:(g + 1) * gw])
        y = y * lax.rsqrt(jnp.mean(y * y, axis=-1, keepdims=True) + EPS)
        parts.append(y * sg_ref[:, g * gw:(g + 1) * gw])
    ssd_n = jnp.concatenate(parts, axis=1).astype(BF16)
    lru = lru_ref[0].astype(BF16)
    y = _dot(lru, wo_ref[0:lw, :]) + _dot(ssd_n, wo_ref[lw:2 * lw, :])
    x1 = x + mod_ref[0, 2:3, :] * y
    h = (_rmsnorm(x1, n2_ref[...]) * (1.0 + mod_ref[0, 4:5, :]) + mod_ref[0, 3:4, :]).astype(BF16)
    dff = w1_ref.shape[1]
    acc = jnp.zeros_like(x1)
    for j in range(dff // ff_chunk):
        a = _dot(h, w1_ref[:, j * ff_chunk:(j + 1) * ff_chunk])
        a = jnp.square(jnp.maximum(a, 0.0)).astype(BF16)
        acc = acc + _dot(a, w2_ref[j * ff_chunk:(j + 1) * ff_chunk, :])
    x2 = x1 + mod_ref[0, 5:6, :] * acc
    if final_norm:
        x2 = _rmsnorm(x2, fg_ref[...])
    o_ref[0] = x2


def _outmlp(x, lru, ssd, z, mod, mod_row, sg, n2, wo, w1, w2, fg, *, tm, ff_chunk=1024):
    b, t, d = x.shape
    lw = lru.shape[-1]
    nx = ssd.shape[1]
    final_norm = fg is not None
    kern = functools.partial(_outmlp_kernel, final_norm=final_norm, ff_chunk=ff_chunk)
    tok = lambda n: pl.BlockSpec((1, tm, n), lambda i, j: (i, j, 0))
    res = lambda a: pl.BlockSpec(a.shape, lambda i, j: (0, 0), pipeline_mode=pl.Buffered(1))
    row = lambda n: pl.BlockSpec((1, n), lambda i, j: (0, 0))
    ins = [x, lru, ssd, z, mod, sg, n2, wo, w1, w2]
    in_specs = [tok(d), tok(lw), pl.BlockSpec((1, nx, tm, LANES), lambda i, j: (i, 0, j, 0)), tok(lw),
                pl.BlockSpec((1, N_MOD, d), lambda i, j: (mod_row(i), 0, 0)),
                row(lw), row(d), res(wo), res(w1), res(w2)]
    if final_norm:
        ins.append(fg)
        in_specs.append(row(d))
    return pl.pallas_call(
        kern,
        out_shape=jax.ShapeDtypeStruct((b, t, d), F32),
        grid=(b, t // tm),
        in_specs=in_specs,
        out_specs=tok(d),
        compiler_params=_cparams("arbitrary", "arbitrary"),
        name="outmlp",
    )(*ins)


def _pair_block_diag(w):
    two, h, k, _ = w.shape
    w = w.reshape(two, h // 2, 2, k, k)
    z = jnp.zeros_like(w[:, :, 0])
    top = jnp.concatenate([w[:, :, 0], z], axis=-1)
    bot = jnp.concatenate([z, w[:, :, 1]], axis=-1)
    return jnp.concatenate([top, bot], axis=-2)


def _pad_lanes(a):
    return jnp.pad(a, [(0, 0)] * (a.ndim - 1) + [(0, LANES - a.shape[-1])])


def kernel(x, c, ctx, c_ctx, ada_w, ada_b, norm1_g, norm2_g, w_in, lru_conv_w, lru_conv_b, lru_wa, lru_ba,
           lru_wx, lru_bx, lru_lambda, ssd_conv_w, ssd_conv_b, ssd_dt_bias, ssd_a_log, ssd_d, ssd_norm_g,
           w_out, mlp_w1, mlp_w2, final_g):
    bsz, t_lat, d = x.shape
    t_ctx = ctx.shape[1]
    depth = ada_w.shape[0]
    lw = lru_conv_w.shape[-1]
    nxbc = ssd_conv_w.shape[-1]
    ndt = 2 * SSD_HEADS
    nscan = lw + nxbc + ndt
    assert bsz + 1 <= MOD_ROWS and t_lat % (GRID_W * SUBLANES) == 0 and t_ctx % SSD_CHUNK == 0

    cvec = jnp.zeros((MOD_ROWS, d), F32).at[:bsz].set(c).at[bsz].set(c_ctx)
    mod = _modulation(cvec, ada_w, ada_b).reshape(depth, MOD_ROWS, N_MOD, d)

    lat_row = lambda i: i
    ctx_row = lambda i: bsz
    tm_lat = 512
    tm_ctx = t_ctx
    nslab = nxbc // LANES

    for l in range(depth):
        need_ctx = l < depth - 1
        wl = w_in[l]
        w_re = jnp.concatenate(
            [wl[:, :lw + nxbc], wl[:, nscan:], _pad_lanes(wl[:, lw + nxbc:nscan])], axis=1).astype(BF16)
        g1 = norm1_g[l].reshape(1, d)
        lx_l, xbc_l, lg_l, z_l, dt_l = _inproj(x, mod[l], lat_row, g1, w_re, lw=lw, nxbc=nxbc, tm=tm_lat)
        lx_c, xbc_c, lg_c, z_c, dt_c = _inproj(ctx, mod[l], ctx_row, g1, w_re, lw=lw, nxbc=nxbc, tm=tm_ctx)

        lru_l, lru_c = _lru(
            lx_l, lx_c, lg_l, lg_c, lru_conv_w[l], lru_conv_b[l].reshape(1, lw),
            _pair_block_diag(0.5 * lru_wa[l]).astype(BF16), _pair_block_diag(0.5 * lru_wx[l]).astype(BF16),
            0.5 * lru_ba[l], 0.5 * lru_bx[l], lru_lambda[l], need_ctx=need_ctx)

        cw = ssd_conv_w[l].reshape(CONV_K, nslab, LANES).transpose(1, 0, 2)
        cb = ssd_conv_b[l].reshape(nslab, 1, LANES)
        dtb = _pad_lanes(ssd_dt_bias[l].reshape(1, ndt))
        alog = _pad_lanes(ssd_a_log[l].reshape(1, ndt))
        dsk = jnp.repeat(ssd_d[l], SSD_HEAD_DIM).reshape(lw // LANES, 1, LANES)
        ssd_l, ssd_c = _ssd(xbc_l, dt_l, xbc_c, dt_c, cw, cb, dtb, alog, dsk, need_ctx=need_ctx)

        sg = ssd_norm_g[l].reshape(1, lw)
        n2 = norm2_g[l].reshape(1, d)
        wo = w_out[l].astype(BF16)
        w1 = mlp_w1[l].astype(BF16)
        w2 = mlp_w2[l].astype(BF16)
        fg = final_g.reshape(1, d) if l == depth - 1 else None
        x = _outmlp(x, lru_l, ssd_l, z_l, mod[l], lat_row, sg, n2, wo, w1, w2, fg, tm=tm_lat)
        if need_ctx:
            ctx = _outmlp(ctx, lru_c, ssd_c, z_c, mod[l], ctx_row, sg, n2, wo, w1, w2, None, tm=tm_ctx)
    return x
```

```python
import functools
import math

import jax
import jax.numpy as jnp
from jax import lax
from jax.experimental import pallas as pl
from jax.experimental.pallas import tpu as pltpu

F32 = jnp.float32
BF16 = jnp.bfloat16

EPS = 1e-6
GRID_W = 64
LRU_BLOCK = 64
LRU_C = 8.0
SSD_HEAD_DIM = 64
SSD_HEADS = 8
SSD_GROUPS = 2
SSD_STATE = 128
SSD_CHUNK = 128
CONV_K = 4
N_MOD = 6
LANES = 128
SUBLANES = 8
MOD_ROWS = 16
VMEM_LIMIT = 56 * 1024 * 1024
NEG_BIG = -1e30


def _sigmoid(x):
    return 0.5 * (jnp.tanh(0.5 * x) + 1.0)


def _silu(x):
    return x * _sigmoid(x)


def _softplus(x):
    return jnp.maximum(x, 0.0) + jnp.log1p(jnp.exp(-jnp.abs(x)))


def _gelu_tanh(x):
    c = math.sqrt(2.0 / math.pi)
    return (0.5 * x) * (1.0 + jnp.tanh(x * (c + (c * 0.044715) * (x * x))))


def _rmsnorm(x, g):
    return x * lax.rsqrt(jnp.mean(x * x, axis=-1, keepdims=True) + EPS) * g


def _dot(a, b):
    return jnp.dot(a, b, preferred_element_type=F32)


def _cparams(*sem):
    return pltpu.CompilerParams(dimension_semantics=sem, vmem_limit_bytes=VMEM_LIMIT)


def _mod_kernel(c_ref, w_ref, b_ref, o_ref):
    s = _silu(c_ref[...]).astype(BF16)
    o_ref[0] = _dot(s, w_ref[0].astype(BF16)) + b_ref[0]


def _modulation(cvec, ada_w, ada_b):
    depth, d, n = ada_w.shape
    tn = 512
    return pl.pallas_call(
        _mod_kernel,
        out_shape=jax.ShapeDtypeStruct((depth, MOD_ROWS, n), F32),
        grid=(depth, n // tn),
        in_specs=[
            pl.BlockSpec((MOD_ROWS, d), lambda l, j: (0, 0)),
            pl.BlockSpec((1, d, tn), lambda l, j: (l, 0, j)),
            pl.BlockSpec((1, 1, tn), lambda l, j: (l, 0, j)),
        ],
        out_specs=pl.BlockSpec((1, MOD_ROWS, tn), lambda l, j: (l, 0, j)),
        compiler_params=_cparams("arbitrary", "arbitrary"),
        name="modulation",
    )(cvec, ada_w, ada_b.reshape(depth, 1, n))


def _inproj_kernel(x_ref, mod_ref, g_ref, w_ref, lx_ref, xd_ref, lg_ref, z_ref, *, lw, nxd):
    x = x_ref[0]
    h = _rmsnorm(x, g_ref[...]) * (1.0 + mod_ref[0, 1:2, :]) + mod_ref[0, 0:1, :]
    h = h.astype(BF16)
    lx_ref[0] = _dot(h, w_ref[:, 0:lw])
    xd_ref[0] = _dot(h, w_ref[:, lw:lw + nxd])
    o = lw + nxd
    lg_ref[0] = _dot(h, w_ref[:, o:o + lw])
    z_ref[0] = _dot(h, w_ref[:, o + lw:o + 2 * lw])


def _inproj(x, mod, mod_row, g, w, *, lw, nxd, tm):
    b, t, d = x.shape
    kern = functools.partial(_inproj_kernel, lw=lw, nxd=nxd)
    tok = lambda n: pl.BlockSpec((1, tm, n), lambda i, j: (i, j, 0))
    return pl.pallas_call(
        kern,
        out_shape=(
            jax.ShapeDtypeStruct((b, t, lw), F32),
            jax.ShapeDtypeStruct((b, t, nxd), F32),
            jax.ShapeDtypeStruct((b, t, lw), F32),
            jax.ShapeDtypeStruct((b, t, lw), F32),
        ),
        grid=(b, t // tm),
        in_specs=[
            tok(d),
            pl.BlockSpec((1, N_MOD, d), lambda i, j: (mod_row(i), 0, 0)),
            pl.BlockSpec((1, d), lambda i, j: (0, 0)),
            pl.BlockSpec(w.shape, lambda i, j: (0, 0), pipeline_mode=pl.Buffered(1)),
        ],
        out_specs=(tok(lw), tok(nxd), tok(lw), tok(lw)),
        compiler_params=_cparams("arbitrary", "arbitrary"),
        name="inproj",
    )(x, mod, g, w)


LRU_ROWS = 256


def _lru_seq(t, u_ref, lg_ref, out_ref, h0, refs, scr, need_out):
    cw_ref, cb_ref, wa_ref, wx_ref, ba_ref, bx_ref, lam_ref = refs
    upad, ubuf, uint, hbuf, pbuf, ybuf = scr
    seg = t // SUBLANES
    pitch = seg + 4
    nblk = max(t // LRU_ROWS, 1)
    rb = t // nblk
    nv = rb // SUBLANES

    zeros8 = jnp.zeros((SUBLANES, LANES), F32)
    upad[0:SUBLANES, :] = zeros8
    upad[SUBLANES:SUBLANES + t, :] = u_ref[...]
    upad[SUBLANES + t:2 * SUBLANES + t, :] = zeros8

    cw = cw_ref[...]
    n = seg + 2 * SUBLANES
    for s in range(SUBLANES):
        v = upad[s * seg:s * seg + n, :]
        acc = cb_ref[...] + cw[1:2, :] * v
        acc = acc + cw[0:1, :] * pltpu.roll(v, 1, 0)
        acc = acc + cw[2:3, :] * pltpu.roll(v, n - 1, 0)
        acc = acc + cw[3:4, :] * pltpu.roll(v, n - 2, 0)
        ubuf[s * pitch:s * pitch + seg, :] = acc[SUBLANES:SUBLANES + seg, :]

    for tau in range(seg):
        uint[tau * SUBLANES:(tau + 1) * SUBLANES, :] = ubuf[pl.ds(tau, SUBLANES, stride=pitch), :]

    row = lax.broadcasted_iota(jnp.int32, (SUBLANES, LANES), 0)
    finals = []
    cmats = []
    kdec = [(-0.5 * LRU_C) * _softplus(-lam_ref[d:d + 1, :]) for d in range(2)]

    def blk_body(i, carry):
        carry = list(carry)
        for d in range(2):
            h, p = carry[2 * d], carry[2 * d + 1]
            blk = i if d == 0 else nblk - 1 - i
            r0 = pl.multiple_of(blk * rb, rb)
            u = uint[pl.ds(r0, rb), :]
            ub = u.astype(BF16)
            log_a = kdec[d] * jnp.tanh(_dot(ub, wa_ref[d, 0]) + ba_ref[d:d + 1, :]) + kdec[d]
            gate = jnp.tanh(_dot(ub, wx_ref[d, 0]) + bx_ref[d:d + 1, :]) + 1.0
            a = jnp.exp(log_a)
            v = jnp.tanh(-log_a) * (1.0 + a * a)
            root = jnp.where(v > 0.0, v * lax.rsqrt(v), 0.0)
            bb = root * (gate * (0.5 * u))
            hs = [None] * nv
            ps = [None] * nv
            order = range(nv) if d == 0 else range(nv - 1, -1, -1)
            for k in order:
                ak = a[k * SUBLANES:(k + 1) * SUBLANES, :]
                h = ak * h + bb[k * SUBLANES:(k + 1) * SUBLANES, :]
                p = ak * p
                hs[k] = h
                ps[k] = p
            hbuf[d, pl.ds(r0, rb), :] = jnp.concatenate(hs, axis=0)
            pbuf[d, pl.ds(r0, rb), :] = jnp.concatenate(ps, axis=0)
            carry[2 * d], carry[2 * d + 1] = h, p
        return tuple(carry)

    ones8 = jnp.ones((SUBLANES, LANES), F32)
    hp = lax.fori_loop(0, nblk, blk_body, (zeros8, ones8, zeros8, ones8), unroll=2 if nblk % 2 == 0 else 1)
    for d in range(2):
        h, p = hp[2 * d], hp[2 * d + 1]
        c = h0[d]
        cmat = zeros8
        order = range(SUBLANES) if d == 0 else range(SUBLANES - 1, -1, -1)
        for s in order:
            cmat = jnp.where(row == s, jnp.broadcast_to(c, (SUBLANES, LANES)), cmat)
            c = h[s:s + 1, :] + p[s:s + 1, :] * c
        finals.append(c)
        cmats.append(cmat)

    if need_out:
        def out_body(i, carry):
            r0 = pl.multiple_of(i * rb, rb)
            y = hbuf[0, pl.ds(r0, rb), :].reshape(nv, SUBLANES, LANES)
            y = y + pbuf[0, pl.ds(r0, rb), :].reshape(nv, SUBLANES, LANES) * cmats[0][None]
            y = y + hbuf[1, pl.ds(r0, rb), :].reshape(nv, SUBLANES, LANES)
            y = y + pbuf[1, pl.ds(r0, rb), :].reshape(nv, SUBLANES, LANES) * cmats[1][None]
            ybuf[pl.ds(r0, rb), :] = y.reshape(rb, LANES)
            return carry

        lax.fori_loop(0, nblk, out_body, 0)
        for tau in range(seg):
            ubuf[pl.ds(tau, SUBLANES, stride=pitch), :] = ybuf[tau * SUBLANES:(tau + 1) * SUBLANES, :]
        for s in range(SUBLANES):
            out_ref[s * seg:(s + 1) * seg, :] = (
                ubuf[s * pitch:s * pitch + seg, :] * _gelu_tanh(lg_ref[s * seg:(s + 1) * seg, :]))
    return finals


def _lru_kernel(*args, t_lat, t_ctx, need_ctx):
    if need_ctx:
        (ul_ref, uc_ref, lgl_ref, lgc_ref, *refs, yl_ref, yc_ref, upad, ubuf, uint, hbuf, pbuf, ybuf) = args
    else:
        (ul_ref, uc_ref, lgl_ref, *refs, yl_ref, upad, ubuf, uint, hbuf, pbuf, ybuf) = args
        lgc_ref = yc_ref = None
    scr = (upad, ubuf, uint, hbuf, pbuf, ybuf)
    zero = jnp.zeros((1, LANES), F32)
    ctx_final = _lru_seq(t_ctx, uc_ref.at[0], lgc_ref.at[0] if need_ctx else None,
                         yc_ref.at[0] if need_ctx else None, (zero, zero), refs, scr, need_ctx)
    _lru_seq(t_lat, ul_ref.at[0], lgl_ref.at[0], yl_ref.at[0], ctx_final, refs, scr, True)


def _lru(lx_l, lx_c, lg_l, lg_c, conv_w, conv_b, wa_bd, wx_bd, ba, bx, lam, *, need_ctx):
    b, t_lat, w = lx_l.shape
    t_ctx = lx_c.shape[1]
    nt = w // LANES
    seq = lambda t: pl.BlockSpec((1, t, LANES), lambda i, j: (i, 0, j))
    par = lambda r: pl.BlockSpec((r, LANES), lambda i, j: (0, j))
    gate = pl.BlockSpec((2, 1, LANES, LANES), lambda i, j: (0, j, 0, 0))
    kern = functools.partial(_lru_kernel, t_lat=t_lat, t_ctx=t_ctx, need_ctx=need_ctx)
    ins = [lx_l, lx_c, lg_l] + ([lg_c] if need_ctx else [])
    in_specs = [seq(t_lat), seq(t_ctx), seq(t_lat)] + ([seq(t_ctx)] if need_ctx else [])
    ins += [conv_w, conv_b, wa_bd, wx_bd, ba, bx, lam]
    in_specs += [par(CONV_K), par(1), gate, gate, par(2), par(2), par(2)]
    out_shape = [jax.ShapeDtypeStruct((b, t_lat, w), F32)]
    out_specs = [seq(t_lat)]
    if need_ctx:
        out_shape.append(jax.ShapeDtypeStruct((b, t_ctx, w), F32))
        out_specs.append(seq(t_ctx))
    pitch_rows = SUBLANES * (t_lat // SUBLANES + 4) + SUBLANES
    res = pl.pallas_call(
        kern,
        out_shape=tuple(out_shape),
        grid=(b, nt),
        in_specs=in_specs,
        out_specs=tuple(out_specs),
        scratch_shapes=[
            pltpu.VMEM((t_lat + 2 * SUBLANES, LANES), F32),
            pltpu.VMEM((pitch_rows, LANES), F32),
            pltpu.VMEM((t_lat, LANES), F32),
            pltpu.VMEM((2, t_lat, LANES), F32),
            pltpu.VMEM((2, t_lat, LANES), F32),
            pltpu.VMEM((t_lat, LANES), F32),
        ],
        compiler_params=_cparams("arbitrary", "arbitrary"),
        name="rglru",
    )(*ins)
    return (res[0], res[1]) if need_ctx else (res[0], None)


def _ssd_kernel(*args, t_lat, t_ctx, need_ctx):
    if need_ctx:
        (xl_hbm, xc_hbm, cw_ref, cb_ref, dtb_ref, alog_ref, dsk_ref, yl_hbm, yc_ref,
         xg, ybuf, sem_in, sem_out, xm, scc, sbt, sbtf, dtp, srct, etb, sbuf, hst, yacc) = args
    else:
        (xl_hbm, xc_hbm, cw_ref, cb_ref, dtb_ref, alog_ref, dsk_ref, yl_hbm,
         xg, ybuf, sem_in, sem_out, xm, scc, sbt, sbtf, dtp, srct, etb, sbuf, hst, yacc) = args
        yc_ref = None
    ck = SSD_CHUNK
    rows = t_lat // GRID_W
    nc_ctx = t_ctx // ck
    nc_lat = t_lat // ck
    nx = SSD_HEADS * SSD_HEAD_DIM // LANES
    ng = SSD_GROUPS
    nslab = nx + 2 * ng
    halo = SUBLANES

    ri = lax.broadcasted_iota(jnp.int32, (ck, ck), 0)
    ci = lax.broadcasted_iota(jnp.int32, (ck, ck), 1)
    lane_lo = ci < SSD_HEAD_DIM

    b = pl.program_id(0)
    nb = pl.num_programs(0)
    slot = lax.rem(b, 2)
    ctx0 = halo
    lat0 = ctx0 + t_ctx + halo
    width = xg.shape[-1]

    def in_copies(bi, sl):
        cps = [pltpu.make_async_copy(xc_hbm.at[bi], xg.at[sl, pl.ds(ctx0, t_ctx), :], sem_in.at[sl])]
        for w in range(GRID_W):
            cps.append(pltpu.make_async_copy(
                xl_hbm.at[bi, :, w, :], xg.at[sl, pl.ds(lat0 + w * rows, rows), :], sem_in.at[sl]))
        return cps

    def out_copies(bi):
        return [pltpu.make_async_copy(ybuf.at[pl.ds(w * rows, rows), :], yl_hbm.at[bi, :, w, :], sem_out.at[0])
                for w in range(GRID_W)]

    @pl.when(b == 0)
    def _():
        zpad = jnp.zeros((halo, width), F32)
        for sl in range(2):
            for r in (0, ctx0 + t_ctx, lat0 + t_lat):
                xg[sl, r:r + halo, :] = zpad
        for cp in in_copies(0, 0):
            cp.start()

    @pl.when(b + 1 < nb)
    def _():
        for cp in in_copies(b + 1, 1 - slot):
            cp.start()

    for cp in in_copies(b, slot):
        cp.wait()

    xs = xg.at[slot]

    def phase_a(k, carry):
        src0 = pl.multiple_of(jnp.where(k < nc_ctx, ctx0 + k * ck, lat0 + (k - nc_ctx) * ck), SUBLANES)
        r0 = pl.multiple_of(k * ck, ck)
        n = ck + 2 * halo
        for j in range(nslab):
            v = xs[pl.ds(src0 - halo, n), j * LANES:(j + 1) * LANES]
            cw = cw_ref[j]
            acc = cb_ref[j] + cw[1:2, :] * v
            acc = acc + cw[0:1, :] * pltpu.roll(v, 1, 0)
            acc = acc + cw[2:3, :] * pltpu.roll(v, n - 1, 0)
            acc = acc + cw[3:4, :] * pltpu.roll(v, n - 2, 0)
            s = _silu(acc[halo:halo + ck, :])
            if j < nx:
                yacc[j, pl.ds(r0, ck), :] = dsk_ref[j] * s
                sb = s.astype(BF16)
                zb = jnp.zeros_like(sb)
                xm[j, pl.ds(r0, ck), :] = jnp.where(lane_lo, sb, zb)
                xm[nx + j, pl.ds(r0, ck), :] = jnp.where(lane_lo, zb, sb)
            elif j < nx + ng:
                st = s.T
                sbtf[j - nx, pl.ds(r0, ck), :] = st
                sbt[j - nx, pl.ds(r0, ck), :] = st.astype(BF16)
            else:
                scc[j - nx - ng, pl.ds(r0, ck), :] = s.astype(BF16)
        dtp[pl.ds(r0, ck), :] = xs[pl.ds(src0, ck), nslab * LANES:(nslab + 1) * LANES]
        return carry

    lax.fori_loop(0, nc_ctx + nc_lat, phase_a, 0)

    dtb = dtb_ref[...]
    lane_ok = lax.broadcasted_iota(jnp.int32, (1, LANES), 1) < 2 * SSD_HEADS
    nega2 = jnp.where(lane_ok, -jnp.exp(alog_ref[...]) * math.log2(math.e), 0.0)

    nck = nc_ctx + nc_lat
    nh2 = 2 * SSD_HEADS
    hpp = SSD_HEADS // ng // 2
    dir1_col = lax.broadcasted_iota(jnp.int32, (1, LANES), 1) >= SSD_HEADS
    tril = (ri >= ci).astype(F32)
    tris = (ri >= ci, ri <= ci)

    def xcat_of(j, r0):
        return jnp.concatenate([xm[j, pl.ds(r0, ck), :], xm[nx + j, pl.ds(r0, ck), :]], axis=0)

    def pass1(kc, carry):
        r0 = pl.multiple_of(kc * ck, ck)
        dt = _softplus(dtp[pl.ds(r0, ck), :] + dtb)
        la = dt * nega2
        cumf = jnp.dot(tril, la, precision=lax.Precision.HIGHEST, preferred_element_type=F32)
        tot = cumf[ck - 1:ck, :]
        cum = jnp.where(dir1_col, tot - cumf + la, cumf)
        ldt = jnp.log2(dt)
        dtp[pl.ds(r0, ck), :] = cum
        srct[pl.ds(pl.multiple_of(kc * nh2, nh2), nh2), :] = (cum - ldt).T[0:nh2, :]
        wgt_t = jnp.exp2((tot - cum + ldt).T[0:nh2, :])
        etb[pl.ds(pl.multiple_of(kc * SUBLANES, SUBLANES), SUBLANES), :] = jnp.broadcast_to(
            jnp.exp2(tot), (SUBLANES, LANES))
        for g in range(ng):
            btf = sbtf[g, pl.ds(r0, ck), :]
            for pr in range(hpp):
                j = g * hpp + pr
                xcat = xcat_of(j, r0)
                for d in range(2):
                    hc = (d * SSD_HEADS + 2 * j, d * SSD_HEADS + 2 * j + 1)
                    btw = jnp.concatenate([(btf * wgt_t[h:h + 1, :]).astype(BF16) for h in hc], axis=1)
                    sbuf[d * nx + j, pl.ds(r0, ck), :] = _dot(btw, xcat).astype(BF16)
        return carry

    lax.fori_loop(0, nck, pass1, 0, unroll=2)

    hst[...] = jnp.zeros(hst.shape, F32)

    def pass2(i, carry):
        for d in range(2):
            if d == 0:
                kc = i
            else:
                kc = jnp.where(i < nc_ctx, nc_ctx - 1 - i, nck - 1 - (i - nc_ctx))
            r0 = pl.multiple_of(kc * ck, ck)
            e = etb[pl.ds(pl.multiple_of(kc * SUBLANES, SUBLANES), 1), :]
            for j in range(nx):
                hc = (d * SSD_HEADS + 2 * j, d * SSD_HEADS + 2 * j + 1)
                et = jnp.where(lane_lo[0:1, :], jnp.broadcast_to(e[:, hc[0]:hc[0] + 1], (1, LANES)),
                               jnp.broadcast_to(e[:, hc[1]:hc[1] + 1], (1, LANES)))
                h = hst[d * nx + j]
                s = sbuf[d * nx + j, pl.ds(r0, ck), :].astype(F32)
                sbuf[d * nx + j, pl.ds(r0, ck), :] = h.astype(BF16)
                hst[d * nx + j] = h * et + s
        return carry

    lax.fori_loop(0, nck, pass2, 0)

    def pass3(kc, is_ctx):
        r0 = kc * ck if isinstance(kc, int) else pl.multiple_of(kc * ck, ck)
        s0 = kc * nh2 if isinstance(kc, int) else pl.multiple_of(kc * nh2, nh2)
        cum = dtp[pl.ds(r0, ck), :]
        src_t = srct[pl.ds(s0, nh2), :]
        for g in range(ng):
            cm = scc[g, pl.ds(r0, ck), :]
            cb = _dot(cm, sbt[g, pl.ds(r0, ck), :])
            for pr in range(hpp):
                j = g * hpp + pr
                xcat = xcat_of(j, r0)
                ms = []
                ecs = []
                for d in range(2):
                    hc = (d * SSD_HEADS + 2 * j, d * SSD_HEADS + 2 * j + 1)
                    col = [jnp.broadcast_to(cum[:, h:h + 1], (ck, LANES)) for h in hc]
                    ms += [(cb * jnp.exp2(jnp.where(tris[d], col[q] - src_t[hc[q]:hc[q] + 1, :], NEG_BIG))
                            ).astype(BF16) for q in range(2)]
                    ecs.append(jnp.exp2(jnp.where(lane_lo, col[0], col[1])))
                y = yacc[j, pl.ds(r0, ck), :] + _dot(jnp.concatenate(ms, axis=1),
                                                      jnp.concatenate([xcat, xcat], axis=0))
                hin = jnp.concatenate([sbuf[j, pl.ds(r0, ck), :], sbuf[nx + j, pl.ds(r0, ck), :]], axis=1)
                yoff = _dot(cm, hin)
                y = y + yoff[:, 0:LANES] * ecs[0] + yoff[:, LANES:2 * LANES] * ecs[1]
                if is_ctx:
                    yc_ref[0, pl.ds(r0, ck), j * LANES:(j + 1) * LANES] = y
                else:
                    y0 = pl.multiple_of((kc - nc_ctx) * ck, ck)
                    ybuf[pl.ds(y0, ck), j * LANES:(j + 1) * LANES] = y

    if need_ctx:
        for kc in range(nc_ctx):
            pass3(kc, True)

    @pl.when(b > 0)
    def _():
        for cp in out_copies(b - 1):
            cp.wait()

    def lat_body(c, carry):
        pass3(nc_ctx + c, False)
        return carry

    lax.fori_loop(0, nc_lat, lat_body, 0, unroll=2)

    for cp in out_copies(b):
        cp.start()

    @pl.when(b == nb - 1)
    def _():
        for cp in out_copies(b):
            cp.wait()


def _ssd(xd_l, xd_c, cw, cb, dtb, alog, dsk, *, need_ctx):
    b, t_lat, width = xd_l.shape
    t_ctx = xd_c.shape[1]
    nx = SSD_HEADS * SSD_HEAD_DIM // LANES
    ng = SSD_GROUPS
    inner = nx * LANES
    tot = t_lat + t_ctx
    rows = t_lat // GRID_W
    kern = functools.partial(_ssd_kernel, t_lat=t_lat, t_ctx=t_ctx, need_ctx=need_ctx)
    full = lambda a: pl.BlockSpec(a.shape, lambda i: (0,) * a.ndim)
    hbm = pl.BlockSpec(memory_space=pl.ANY)
    out_shape = [jax.ShapeDtypeStruct((b, rows, GRID_W, inner), F32)]
    out_specs = [hbm]
    if need_ctx:
        out_shape.append(jax.ShapeDtypeStruct((b, t_ctx, inner), F32))
        out_specs.append(pl.BlockSpec((1, t_ctx, inner), lambda i: (i, 0, 0)))
    res = pl.pallas_call(
        kern,
        out_shape=tuple(out_shape),
        grid=(b,),
        in_specs=[hbm, hbm, full(cw), full(cb), full(dtb), full(alog), full(dsk)],
        out_specs=tuple(out_specs),
        scratch_shapes=[
            pltpu.VMEM((2, tot + 3 * SUBLANES, width), F32),
            pltpu.VMEM((t_lat, inner), F32),
            pltpu.SemaphoreType.DMA((2,)),
            pltpu.SemaphoreType.DMA((1,)),
            pltpu.VMEM((2 * nx, tot, LANES), BF16),
            pltpu.VMEM((ng, tot, LANES), BF16),
            pltpu.VMEM((ng, tot, LANES), BF16),
            pltpu.VMEM((ng, tot, LANES), F32),
            pltpu.VMEM((tot, LANES), F32),
            pltpu.VMEM((tot // SSD_CHUNK * 2 * SSD_HEADS, LANES), F32),
            pltpu.VMEM((tot // SSD_CHUNK * SUBLANES, LANES), F32),
            pltpu.VMEM((2 * nx, tot, LANES), BF16),
            pltpu.VMEM((2 * nx, SSD_STATE, LANES), F32),
            pltpu.VMEM((nx, tot, LANES), F32),
        ],
        compiler_params=_cparams("arbitrary"),
        name="ssd",
    )(xd_l.reshape(b, rows, GRID_W, width), xd_c, cw, cb, dtb, alog, dsk)
    y_l = res[0].reshape(b, t_lat, inner)
    return (y_l, res[1]) if need_ctx else (y_l, None)


def _outmlp_kernel(*args, final_norm, ff_chunk):
    if final_norm:
        (x_ref, lru_ref, ssd_ref, z_ref, mod_ref, sg_ref, n2_ref, wo_ref, w1_ref, w2_ref, fg_ref, o_ref) = args
    else:
        (x_ref, lru_ref, ssd_ref, z_ref, mod_ref, sg_ref, n2_ref, wo_ref, w1_ref, w2_ref, o_ref) = args
    x = x_ref[0]
    z = z_ref[0]
    lw = z.shape[-1]
    gw = lw // SSD_GROUPS
    parts = []
    for g in range(SSD_GROUPS):
        y = ssd_ref[0, :, g * gw:(g + 1) * gw] * _silu(z[:, g * gw:(g + 1) * gw])
        y = y * lax.rsqrt(jnp.mean(y * y, axis=-1, keepdims=True) + EPS)
        parts.append(y * sg_ref[:, g * gw:(g + 1) * gw])
    ssd_n = jnp.concatenate(parts, axis=1).astype(BF16)
    lru = lru_ref[0].astype(BF16)
    y = _dot(lru, wo_ref[0:lw, :]) + _dot(ssd_n, wo_ref[lw:2 * lw, :])
    x1 = x + mod_ref[0, 2:3, :] * y
    h = (_rmsnorm(x1, n2_ref[...]) * (1.0 + mod_ref[0, 4:5, :]) + mod_ref[0, 3:4, :]).astype(BF16)
    dff = w1_ref.shape[1]
    acc = jnp.zeros_like(x1)
    for j in range(dff // ff_chunk):
        a = _dot(h, w1_ref[:, j * ff_chunk:(j + 1) * ff_chunk])
        a = jnp.square(jnp.maximum(a, 0.0)).astype(BF16)
        acc = acc + _dot(a, w2_ref[j * ff_chunk:(j + 1) * ff_chunk, :])
    x2 = x1 + mod_ref[0, 5:6, :] * acc
    if final_norm:
        x2 = _rmsnorm(x2, fg_ref[...])
    o_ref[0] = x2


def _outmlp(x, lru, ssd, z, mod, mod_row, sg, n2, wo, w1, w2, fg, *, tm, ff_chunk=1024):
    b, t, d = x.shape
    lw = lru.shape[-1]
    final_norm = fg is not None
    kern = functools.partial(_outmlp_kernel, final_norm=final_norm, ff_chunk=ff_chunk)
    tok = lambda n: pl.BlockSpec((1, tm, n), lambda i, j: (i, j, 0))
    res = lambda a: pl.BlockSpec(a.shape, lambda i, j: (0, 0), pipeline_mode=pl.Buffered(1))
    row = lambda n: pl.BlockSpec((1, n), lambda i, j: (0, 0))
    ins = [x, lru, ssd, z, mod, sg, n2, wo, w1, w2]
    in_specs = [tok(d), tok(lw), tok(lw), tok(lw),
                pl.BlockSpec((1, N_MOD, d), lambda i, j: (mod_row(i), 0, 0)),
                row(lw), row(d), res(wo), res(w1), res(w2)]
    if final_norm:
        ins.append(fg)
        in_specs.append(row(d))
    return pl.pallas_call(
        kern,
        out_shape=jax.ShapeDtypeStruct((b, t, d), F32),
        grid=(b, t // tm),
        in_specs=in_specs,
        out_specs=tok(d),
        compiler_params=_cparams("arbitrary", "arbitrary"),
        name="outmlp",
    )(*ins)


def _pair_block_diag(w):
    two, h, k, _ = w.shape
    w = w.reshape(two, h // 2, 2, k, k)
    z = jnp.zeros_like(w[:, :, 0])
    top = jnp.concatenate([w[:, :, 0], z], axis=-1)
    bot = jnp.concatenate([z, w[:, :, 1]], axis=-1)
    return jnp.concatenate([top, bot], axis=-2)


def _pad_lanes(a):
    return jnp.pad(a, [(0, 0)] * (a.ndim - 1) + [(0, LANES - a.shape[-1])])


def kernel(x, c, ctx, c_ctx, ada_w, ada_b, norm1_g, norm2_g, w_in, lru_conv_w, lru_conv_b, lru_wa, lru_ba,
           lru_wx, lru_bx, lru_lambda, ssd_conv_w, ssd_conv_b, ssd_dt_bias, ssd_a_log, ssd_d, ssd_norm_g,
           w_out, mlp_w1, mlp_w2, final_g):
    bsz, t_lat, d = x.shape
    t_ctx = ctx.shape[1]
    depth = ada_w.shape[0]
    lw = lru_conv_w.shape[-1]
    nxbc = ssd_conv_w.shape[-1]
    ndt = 2 * SSD_HEADS
    nscan = lw + nxbc + ndt
    assert bsz + 1 <= MOD_ROWS and t_lat % (GRID_W * SUBLANES) == 0 and t_ctx % SSD_CHUNK == 0

    cvec = jnp.zeros((MOD_ROWS, d), F32).at[:bsz].set(c).at[bsz].set(c_ctx)
    mod = _modulation(cvec, ada_w, ada_b).reshape(depth, MOD_ROWS, N_MOD, d)

    lat_row = lambda i: i
    ctx_row = lambda i: bsz
    tm_lat = 512
    tm_ctx = t_ctx
    nslab = nxbc // LANES

    for l in range(depth):
        need_ctx = l < depth - 1
        wl = w_in[l]
        w_re = jnp.concatenate(
            [wl[:, :lw + nxbc], _pad_lanes(wl[:, lw + nxbc:nscan]), wl[:, nscan:]], axis=1).astype(BF16)
        g1 = norm1_g[l].reshape(1, d)
        nxd = nxbc + LANES
        lx_l, xd_l, lg_l, z_l = _inproj(x, mod[l], lat_row, g1, w_re, lw=lw, nxd=nxd, tm=tm_lat)
        lx_c, xd_c, lg_c, z_c = _inproj(ctx, mod[l], ctx_row, g1, w_re, lw=lw, nxd=nxd, tm=tm_ctx)

        lru_l, lru_c = _lru(
            lx_l, lx_c, lg_l, lg_c, lru_conv_w[l], lru_conv_b[l].reshape(1, lw),
            _pair_block_diag(0.5 * lru_wa[l]).astype(BF16), _pair_block_diag(0.5 * lru_wx[l]).astype(BF16),
            0.5 * lru_ba[l], 0.5 * lru_bx[l], lru_lambda[l], need_ctx=need_ctx)

        cw = ssd_conv_w[l].reshape(CONV_K, nslab, LANES).transpose(1, 0, 2)
        cb = ssd_conv_b[l].reshape(nslab, 1, LANES)
        dtb = _pad_lanes(ssd_dt_bias[l].reshape(1, ndt))
        alog = _pad_lanes(ssd_a_log[l].reshape(1, ndt))
        dsk = jnp.repeat(ssd_d[l], SSD_HEAD_DIM).reshape(lw // LANES, 1, LANES)
        ssd_l, ssd_c = _ssd(xd_l, xd_c, cw, cb, dtb, alog, dsk, need_ctx=need_ctx)

        sg = ssd_norm_g[l].reshape(1, lw)
        n2 = norm2_g[l].reshape(1, d)
        wo = w_out[l].astype(BF16)
        w1 = mlp_w1[l].astype(BF16)
        w2 = mlp_w2[l].astype(BF16)
        fg = final_g.reshape(1, d) if l == depth - 1 else None
        x = _outmlp(x, lru_l, ssd_l, z_l, mod[l], lat_row, sg, n2, wo, w1, w2, fg, tm=tm_lat)
        if need_ctx:
            ctx = _outmlp(ctx, lru_c, ssd_c, z_c, mod[l], ctx_row, sg, n2, wo, w1, w2, None, tm=tm_ctx)
    return x
```

```python
import functools
import math

import jax
import jax.numpy as jnp
from jax import lax
from jax.experimental import pallas as pl
from jax.experimental.pallas import tpu as pltpu

F32 = jnp.float32
BF16 = jnp.bfloat16

EPS = 1e-6
GRID_W = 64
LRU_BLOCK = 64
LRU_C = 8.0
SSD_HEAD_DIM = 64
SSD_HEADS = 8
SSD_GROUPS = 2
SSD_STATE = 128
SSD_CHUNK = 128
CONV_K = 4
N_MOD = 6
LANES = 128
SUBLANES = 8
MOD_ROWS = 16
VMEM_LIMIT = 56 * 1024 * 1024
NEG_BIG = -1e30


def _sigmoid(x):
    return 0.5 * (jnp.tanh(0.5 * x) + 1.0)


def _silu(x):
    return x * _sigmoid(x)


def _softplus(x):
    return jnp.maximum(x, 0.0) + jnp.log1p(jnp.exp(-jnp.abs(x)))


def _gelu_tanh(x):
    c = math.sqrt(2.0 / math.pi)
    return (0.5 * x) * (1.0 + jnp.tanh(x * (c + (c * 0.044715) * (x * x))))


def _rmsnorm(x, g):
    return x * lax.rsqrt(jnp.mean(x * x, axis=-1, keepdims=True) + EPS) * g


def _dot(a, b):
    return jnp.dot(a, b, preferred_element_type=F32)


def _cparams(*sem):
    return pltpu.CompilerParams(dimension_semantics=sem, vmem_limit_bytes=VMEM_LIMIT)


def _mod_kernel(c_ref, w_ref, b_ref, o_ref):
    s = _silu(c_ref[...]).astype(BF16)
    o_ref[0] = _dot(s, w_ref[0].astype(BF16)) + b_ref[0]


def _modulation(cvec, ada_w, ada_b):
    depth, d, n = ada_w.shape
    tn = 512
    return pl.pallas_call(
        _mod_kernel,
        out_shape=jax.ShapeDtypeStruct((depth, MOD_ROWS, n), F32),
        grid=(depth, n // tn),
        in_specs=[
            pl.BlockSpec((MOD_ROWS, d), lambda l, j: (0, 0)),
            pl.BlockSpec((1, d, tn), lambda l, j: (l, 0, j)),
            pl.BlockSpec((1, 1, tn), lambda l, j: (l, 0, j)),
        ],
        out_specs=pl.BlockSpec((1, MOD_ROWS, tn), lambda l, j: (l, 0, j)),
        compiler_params=_cparams("arbitrary", "arbitrary"),
        name="modulation",
    )(cvec, ada_w, ada_b.reshape(depth, 1, n))


def _inproj_kernel(x_ref, mod_ref, g_ref, w_ref, lx_ref, xd_ref, lg_ref, z_ref, *, lw, nxd):
    x = x_ref[0]
    h = _rmsnorm(x, g_ref[...]) * (1.0 + mod_ref[0, 1:2, :]) + mod_ref[0, 0:1, :]
    h = h.astype(BF16)
    lx_ref[0] = _dot(h, w_ref[:, 0:lw])
    xd_ref[0] = _dot(h, w_ref[:, lw:lw + nxd])
    o = lw + nxd
    lg_ref[0] = _dot(h, w_ref[:, o:o + lw])
    z_ref[0] = _dot(h, w_ref[:, o + lw:o + 2 * lw])


def _inproj(x, mod, mod_row, g, w, *, lw, nxd, tm):
    b, t, d = x.shape
    kern = functools.partial(_inproj_kernel, lw=lw, nxd=nxd)
    tok = lambda n: pl.BlockSpec((1, tm, n), lambda i, j: (i, j, 0))
    return pl.pallas_call(
        kern,
        out_shape=(
            jax.ShapeDtypeStruct((b, t, lw), F32),
            jax.ShapeDtypeStruct((b, t, nxd), F32),
            jax.ShapeDtypeStruct((b, t, lw), F32),
            jax.ShapeDtypeStruct((b, t, lw), F32),
        ),
        grid=(b, t // tm),
        in_specs=[
            tok(d),
            pl.BlockSpec((1, N_MOD, d), lambda i, j: (mod_row(i), 0, 0)),
            pl.BlockSpec((1, d), lambda i, j: (0, 0)),
            pl.BlockSpec(w.shape, lambda i, j: (0, 0), pipeline_mode=pl.Buffered(1)),
        ],
        out_specs=(tok(lw), tok(nxd), tok(lw), tok(lw)),
        compiler_params=_cparams("arbitrary", "arbitrary"),
        name="inproj",
    )(x, mod, g, w)


LRU_ROWS = 256


def _lru_seq(t, u_ref, lg_ref, out_ref, h0, refs, scr, need_out):
    cw_ref, cb_ref, wa_ref, wx_ref, ba_ref, bx_ref, lam_ref = refs
    upad, ubuf, uint, hbuf, pbuf, ybuf = scr
    seg = t // SUBLANES
    pitch = seg + 4
    nblk = max(t // LRU_ROWS, 1)
    rb = t // nblk
    nv = rb // SUBLANES

    zeros8 = jnp.zeros((SUBLANES, LANES), F32)
    upad[0:SUBLANES, :] = zeros8
    upad[SUBLANES:SUBLANES + t, :] = u_ref[...]
    upad[SUBLANES + t:2 * SUBLANES + t, :] = zeros8

    cw = cw_ref[...]
    n = seg + 2 * SUBLANES
    for s in range(SUBLANES):
        v = upad[s * seg:s * seg + n, :]
        acc = cb_ref[...] + cw[1:2, :] * v
        acc = acc + cw[0:1, :] * pltpu.roll(v, 1, 0)
        acc = acc + cw[2:3, :] * pltpu.roll(v, n - 1, 0)
        acc = acc + cw[3:4, :] * pltpu.roll(v, n - 2, 0)
        ubuf[s * pitch:s * pitch + seg, :] = acc[SUBLANES:SUBLANES + seg, :]

    for tau in range(seg):
        uint[tau * SUBLANES:(tau + 1) * SUBLANES, :] = ubuf[pl.ds(tau, SUBLANES, stride=pitch), :]

    row = lax.broadcasted_iota(jnp.int32, (SUBLANES, LANES), 0)
    finals = []
    cmats = []
    kdec = [(-0.5 * LRU_C) * _softplus(-lam_ref[d:d + 1, :]) for d in range(2)]

    def blk_body(i, carry):
        carry = list(carry)
        for d in range(2):
            h, p = carry[2 * d], carry[2 * d + 1]
            blk = i if d == 0 else nblk - 1 - i
            r0 = pl.multiple_of(blk * rb, rb)
            u = uint[pl.ds(r0, rb), :]
            ub = u.astype(BF16)
            log_a = kdec[d] * jnp.tanh(_dot(ub, wa_ref[d, 0]) + ba_ref[d:d + 1, :]) + kdec[d]
            gate = jnp.tanh(_dot(ub, wx_ref[d, 0]) + bx_ref[d:d + 1, :]) + 1.0
            a = jnp.exp(log_a)
            v = jnp.tanh(-log_a) * (1.0 + a * a)
            root = jnp.where(v > 0.0, v * lax.rsqrt(v), 0.0)
            bb = root * (gate * (0.5 * u))
            hs = [None] * nv
            ps = [None] * nv
            order = range(nv) if d == 0 else range(nv - 1, -1, -1)
            for k in order:
                ak = a[k * SUBLANES:(k + 1) * SUBLANES, :]
                h = ak * h + bb[k * SUBLANES:(k + 1) * SUBLANES, :]
                p = ak * p
                hs[k] = h
                ps[k] = p
            hbuf[d, pl.ds(r0, rb), :] = jnp.concatenate(hs, axis=0)
            pbuf[d, pl.ds(r0, rb), :] = jnp.concatenate(ps, axis=0)
            carry[2 * d], carry[2 * d + 1] = h, p
        return tuple(carry)

    ones8 = jnp.ones((SUBLANES, LANES), F32)
    hp = lax.fori_loop(0, nblk, blk_body, (zeros8, ones8, zeros8, ones8), unroll=2 if nblk % 2 == 0 else 1)
    for d in range(2):
        h, p = hp[2 * d], hp[2 * d + 1]
        c = h0[d]
        cmat = zeros8
        order = range(SUBLANES) if d == 0 else range(SUBLANES - 1, -1, -1)
        for s in order:
            cmat = jnp.where(row == s, jnp.broadcast_to(c, (SUBLANES, LANES)), cmat)
            c = h[s:s + 1, :] + p[s:s + 1, :] * c
        finals.append(c)
        cmats.append(cmat)

    if need_out:
        def out_body(i, carry):
            r0 = pl.multiple_of(i * rb, rb)
            y = hbuf[0, pl.ds(r0, rb), :].reshape(nv, SUBLANES, LANES)
            y = y + pbuf[0, pl.ds(r0, rb), :].reshape(nv, SUBLANES, LANES) * cmats[0][None]
            y = y + hbuf[1, pl.ds(r0, rb), :].reshape(nv, SUBLANES, LANES)
            y = y + pbuf[1, pl.ds(r0, rb), :].reshape(nv, SUBLANES, LANES) * cmats[1][None]
            ybuf[pl.ds(r0, rb), :] = y.reshape(rb, LANES)
            return carry

        lax.fori_loop(0, nblk, out_body, 0)
        for tau in range(seg):
            ubuf[pl.ds(tau, SUBLANES, stride=pitch), :] = ybuf[tau * SUBLANES:(tau + 1) * SUBLANES, :]
        for s in range(SUBLANES):
            out_ref[s * seg:(s + 1) * seg, :] = (
                ubuf[s * pitch:s * pitch + seg, :] * _gelu_tanh(lg_ref[s * seg:(s + 1) * seg, :]))
    return finals


def _lru_kernel(*args, t_lat, t_ctx, need_ctx):
    if need_ctx:
        (ul_ref, uc_ref, lgl_ref, lgc_ref, *refs, yl_ref, yc_ref, upad, ubuf, uint, hbuf, pbuf, ybuf) = args
    else:
        (ul_ref, uc_ref, lgl_ref, *refs, yl_ref, upad, ubuf, uint, hbuf, pbuf, ybuf) = args
        lgc_ref = yc_ref = None
    scr = (upad, ubuf, uint, hbuf, pbuf, ybuf)
    zero = jnp.zeros((1, LANES), F32)
    ctx_final = _lru_seq(t_ctx, uc_ref.at[0], lgc_ref.at[0] if need_ctx else None,
                         yc_ref.at[0] if need_ctx else None, (zero, zero), refs, scr, need_ctx)
    _lru_seq(t_lat, ul_ref.at[0], lgl_ref.at[0], yl_ref.at[0], ctx_final, refs, scr, True)


def _lru(lx_l, lx_c, lg_l, lg_c, conv_w, conv_b, wa_bd, wx_bd, ba, bx, lam, *, need_ctx):
    b, t_lat, w = lx_l.shape
    t_ctx = lx_c.shape[1]
    nt = w // LANES
    seq = lambda t: pl.BlockSpec((1, t, LANES), lambda i, j: (i, 0, j))
    par = lambda r: pl.BlockSpec((r, LANES), lambda i, j: (0, j))
    gate = pl.BlockSpec((2, 1, LANES, LANES), lambda i, j: (0, j, 0, 0))
    kern = functools.partial(_lru_kernel, t_lat=t_lat, t_ctx=t_ctx, need_ctx=need_ctx)
    ins = [lx_l, lx_c, lg_l] + ([lg_c] if need_ctx else [])
    in_specs = [seq(t_lat), seq(t_ctx), seq(t_lat)] + ([seq(t_ctx)] if need_ctx else [])
    ins += [conv_w, conv_b, wa_bd, wx_bd, ba, bx, lam]
    in_specs += [par(CONV_K), par(1), gate, gate, par(2), par(2), par(2)]
    out_shape = [jax.ShapeDtypeStruct((b, t_lat, w), F32)]
    out_specs = [seq(t_lat)]
    if need_ctx:
        out_shape.append(jax.ShapeDtypeStruct((b, t_ctx, w), F32))
        out_specs.append(seq(t_ctx))
    pitch_rows = SUBLANES * (t_lat // SUBLANES + 4) + SUBLANES
    res = pl.pallas_call(
        kern,
        out_shape=tuple(out_shape),
        grid=(b, nt),
        in_specs=in_specs,
        out_specs=tuple(out_specs),
        scratch_shapes=[
            pltpu.VMEM((t_lat + 2 * SUBLANES, LANES), F32),
            pltpu.VMEM((pitch_rows, LANES), F32),
            pltpu.VMEM((t_lat, LANES), F32),
            pltpu.VMEM((2, t_lat, LANES), F32),
            pltpu.VMEM((2, t_lat, LANES), F32),
            pltpu.VMEM((t_lat, LANES), F32),
        ],
        compiler_params=_cparams("arbitrary", "arbitrary"),
        name="rglru",
    )(*ins)
    return (res[0], res[1]) if need_ctx else (res[0], None)


def _ssd_kernel(*args, t_lat, t_ctx, need_ctx):
    if need_ctx:
        (xl_hbm, xc_hbm, cw_ref, cb_ref, dtb_ref, alog_ref, dsk_ref, yl_hbm, yc_ref,
         xg, ybuf, sem_in, sem_out, xm, scc, sbt, sbtf, cumb, srct, etb, sbuf, hst, yacc) = args
    else:
        (xl_hbm, xc_hbm, cw_ref, cb_ref, dtb_ref, alog_ref, dsk_ref, yl_hbm,
         xg, ybuf, sem_in, sem_out, xm, scc, sbt, sbtf, cumb, srct, etb, sbuf, hst, yacc) = args
        yc_ref = None
    ck = SSD_CHUNK
    rows = t_lat // GRID_W
    nc_ctx = t_ctx // ck
    nc_lat = t_lat // ck
    nx = SSD_HEADS * SSD_HEAD_DIM // LANES
    ng = SSD_GROUPS
    nslab = nx + 2 * ng
    halo = SUBLANES

    ri = lax.broadcasted_iota(jnp.int32, (ck, ck), 0)
    ci = lax.broadcasted_iota(jnp.int32, (ck, ck), 1)
    lane_lo = ci < SSD_HEAD_DIM

    b = pl.program_id(0)
    nb = pl.num_programs(0)
    slot = lax.rem(b, 2)
    ctx0 = halo
    lat0 = ctx0 + t_ctx + halo
    width = xg.shape[-1]

    def in_copies(bi, sl):
        cps = [pltpu.make_async_copy(xc_hbm.at[bi], xg.at[sl, pl.ds(ctx0, t_ctx), :], sem_in.at[sl])]
        for w in range(GRID_W):
            cps.append(pltpu.make_async_copy(
                xl_hbm.at[bi, :, w, :], xg.at[sl, pl.ds(lat0 + w * rows, rows), :], sem_in.at[sl]))
        return cps

    def out_copies(bi):
        return [pltpu.make_async_copy(ybuf.at[pl.ds(w * rows, rows), :], yl_hbm.at[bi, :, w, :], sem_out.at[0])
                for w in range(GRID_W)]

    @pl.when(b == 0)
    def _():
        zpad = jnp.zeros((halo, width), F32)
        for sl in range(2):
            for r in (0, ctx0 + t_ctx, lat0 + t_lat):
                xg[sl, r:r + halo, :] = zpad
        for cp in in_copies(0, 0):
            cp.start()

    @pl.when(b + 1 < nb)
    def _():
        for cp in in_copies(b + 1, 1 - slot):
            cp.start()

    for cp in in_copies(b, slot):
        cp.wait()

    xs = xg.at[slot]

    def src_row(k):
        return pl.multiple_of(jnp.where(k < nc_ctx, ctx0 + k * ck, lat0 + (k - nc_ctx) * ck), SUBLANES)

    def phase_a(k, carry):
        src0 = src_row(k)
        r0 = pl.multiple_of(k * ck, ck)
        n = ck + 2 * halo
        for j in range(nslab):
            v = xs[pl.ds(src0 - halo, n), j * LANES:(j + 1) * LANES]
            cw = cw_ref[j]
            taps = (pltpu.roll(v, 1, 0), v, pltpu.roll(v, n - 1, 0), pltpu.roll(v, n - 2, 0))
            half = cb_ref[j]
            for k in range(CONV_K):
                half = half + cw[k:k + 1, :] * taps[k][halo:halo + ck, :]
            s = half * jnp.tanh(half) + half
            if j < nx:
                yacc[j, pl.ds(r0, ck), :] = dsk_ref[j] * s
                sb = s.astype(BF16)
                zb = jnp.zeros_like(sb)
                xm[j, pl.ds(r0, ck), :] = jnp.where(lane_lo, sb, zb)
                xm[nx + j, pl.ds(r0, ck), :] = jnp.where(lane_lo, zb, sb)
            elif j < nx + ng:
                st = s.T
                sbtf[j - nx, pl.ds(r0, ck), :] = st
                sbt[j - nx, pl.ds(r0, ck), :] = st.astype(BF16)
            else:
                scc[j - nx - ng, pl.ds(r0, ck), :] = s.astype(BF16)
        return carry

    lax.fori_loop(0, nc_ctx + nc_lat, phase_a, 0)

    dtb = dtb_ref[...]
    lane_ok = lax.broadcasted_iota(jnp.int32, (1, LANES), 1) < 2 * SSD_HEADS
    nega2 = jnp.where(lane_ok, -jnp.exp(alog_ref[...]) * math.log2(math.e), 0.0)

    nck = nc_ctx + nc_lat
    nh2 = 2 * SSD_HEADS
    hpp = SSD_HEADS // ng // 2
    dir1_col = lax.broadcasted_iota(jnp.int32, (1, LANES), 1) >= SSD_HEADS
    tris = (ri >= ci, ri <= ci)

    def xcat_of(j, r0):
        return jnp.concatenate([xm[j, pl.ds(r0, ck), :], xm[nx + j, pl.ds(r0, ck), :]], axis=0)

    def pass1(kc, carry):
        r0 = pl.multiple_of(kc * ck, ck)
        dt = _softplus(xs[pl.ds(src_row(kc), ck), nslab * LANES:(nslab + 1) * LANES] + dtb)
        la = dt * nega2
        cumf = la
        for sh in (1, 2, 4):
            cumf = cumf + jnp.where(ri >= sh, pltpu.roll(cumf, sh, 0), 0.0)
        for sh in (8, 16, 32, 64):
            cumf = cumf + jnp.concatenate([jnp.zeros((sh, LANES), F32), cumf[0:ck - sh, :]], axis=0)
        tot = cumf[ck - 1:ck, :]
        cum = jnp.where(dir1_col, tot - cumf + la, cumf)
        ldt = jnp.log2(dt)
        cumb[pl.ds(r0, ck), :] = cum
        srct[pl.ds(pl.multiple_of(kc * nh2, nh2), nh2), :] = (cum - ldt).T[0:nh2, :]
        wgt_t = jnp.exp2((tot - cum + ldt).T[0:nh2, :])
        etb[pl.ds(pl.multiple_of(kc * SUBLANES, SUBLANES), SUBLANES), :] = jnp.broadcast_to(
            jnp.exp2(tot), (SUBLANES, LANES))
        for g in range(ng):
            btf = sbtf[g, pl.ds(r0, ck), :]
            for pr in range(hpp):
                j = g * hpp + pr
                btw = []
                for d in range(2):
                    hc = (d * SSD_HEADS + 2 * j, d * SSD_HEADS + 2 * j + 1)
                    btw.append(jnp.concatenate([(btf * wgt_t[h:h + 1, :]).astype(BF16) for h in hc], axis=1))
                s = _dot(jnp.concatenate(btw, axis=0), xcat_of(j, r0)).astype(BF16)
                sbuf[j, pl.ds(r0, ck), :] = s[0:SSD_STATE, :]
                sbuf[nx + j, pl.ds(r0, ck), :] = s[SSD_STATE:2 * SSD_STATE, :]
        return carry

    lax.fori_loop(0, nck, pass1, 0, unroll=2)

    hst[...] = jnp.zeros(hst.shape, F32)

    def pass2(i, carry):
        for d in range(2):
            if d == 0:
                kc = i
            else:
                kc = jnp.where(i < nc_ctx, nc_ctx - 1 - i, nck - 1 - (i - nc_ctx))
            r0 = pl.multiple_of(kc * ck, ck)
            e = etb[pl.ds(pl.multiple_of(kc * SUBLANES, SUBLANES), 1), :]
            for j in range(nx):
                hc = (d * SSD_HEADS + 2 * j, d * SSD_HEADS + 2 * j + 1)
                et = jnp.where(lane_lo[0:1, :], jnp.broadcast_to(e[:, hc[0]:hc[0] + 1], (1, LANES)),
                               jnp.broadcast_to(e[:, hc[1]:hc[1] + 1], (1, LANES)))
                h = hst[d * nx + j]
                s = sbuf[d * nx + j, pl.ds(r0, ck), :].astype(F32)
                sbuf[d * nx + j, pl.ds(r0, ck), :] = h.astype(BF16)
                hst[d * nx + j] = h * et + s
        return carry

    lax.fori_loop(0, nck, pass2, 0)

    def pass3(kc, is_ctx):
        r0 = kc * ck if isinstance(kc, int) else pl.multiple_of(kc * ck, ck)
        s0 = kc * nh2 if isinstance(kc, int) else pl.multiple_of(kc * nh2, nh2)
        cum = cumb[pl.ds(r0, ck), :]
        src_t = srct[pl.ds(s0, nh2), :]
        for g in range(ng):
            cm = scc[g, pl.ds(r0, ck), :]
            cb = _dot(cm, sbt[g, pl.ds(r0, ck), :])
            for pr in range(hpp):
                j = g * hpp + pr
                dec = []
                ecs = []
                for d in range(2):
                    hc = (d * SSD_HEADS + 2 * j, d * SSD_HEADS + 2 * j + 1)
                    col = [jnp.broadcast_to(cum[:, h:h + 1], (ck, LANES)) for h in hc]
                    dec.append([jnp.exp2(jnp.where(tris[d], col[q] - src_t[hc[q]:hc[q] + 1, :], NEG_BIG))
                                for q in range(2)])
                    ecs.append(jnp.exp2(jnp.where(lane_lo, col[0], col[1])))
                ms = [(cb * (dec[0][q] + dec[1][q])).astype(BF16) for q in range(2)]
                y = yacc[j, pl.ds(r0, ck), :] + _dot(jnp.concatenate(ms, axis=1), xcat_of(j, r0))
                hin = jnp.concatenate([sbuf[j, pl.ds(r0, ck), :], sbuf[nx + j, pl.ds(r0, ck), :]], axis=1)
                yoff = _dot(cm, hin)
                y = y + yoff[:, 0:LANES] * ecs[0] + yoff[:, LANES:2 * LANES] * ecs[1]
                if is_ctx:
                    yc_ref[0, pl.ds(r0, ck), j * LANES:(j + 1) * LANES] = y
                else:
                    y0 = pl.multiple_of((kc - nc_ctx) * ck, ck)
                    ybuf[pl.ds(y0, ck), j * LANES:(j + 1) * LANES] = y

    if need_ctx:
        for kc in range(nc_ctx):
            pass3(kc, True)

    @pl.when(b > 0)
    def _():
        for cp in out_copies(b - 1):
            cp.wait()

    def lat_body(c, carry):
        pass3(nc_ctx + c, False)
        return carry

    lax.fori_loop(0, nc_lat, lat_body, 0, unroll=2)

    for cp in out_copies(b):
        cp.start()

    @pl.when(b == nb - 1)
    def _():
        for cp in out_copies(b):
            cp.wait()


def _ssd(xd_l, xd_c, cw, cb, dtb, alog, dsk, *, need_ctx):
    b, t_lat, width = xd_l.shape
    t_ctx = xd_c.shape[1]
    nx = SSD_HEADS * SSD_HEAD_DIM // LANES
    ng = SSD_GROUPS
    inner = nx * LANES
    tot = t_lat + t_ctx
    rows = t_lat // GRID_W
    kern = functools.partial(_ssd_kernel, t_lat=t_lat, t_ctx=t_ctx, need_ctx=need_ctx)
    full = lambda a: pl.BlockSpec(a.shape, lambda i: (0,) * a.ndim)
    hbm = pl.BlockSpec(memory_space=pl.ANY)
    out_shape = [jax.ShapeDtypeStruct((b, rows, GRID_W, inner), F32)]
    out_specs = [hbm]
    if need_ctx:
        out_shape.append(jax.ShapeDtypeStruct((b, t_ctx, inner), F32))
        out_specs.append(pl.BlockSpec((1, t_ctx, inner), lambda i: (i, 0, 0)))
    res = pl.pallas_call(
        kern,
        out_shape=tuple(out_shape),
        grid=(b,),
        in_specs=[hbm, hbm, full(cw), full(cb), full(dtb), full(alog), full(dsk)],
        out_specs=tuple(out_specs),
        scratch_shapes=[
            pltpu.VMEM((2, tot + 3 * SUBLANES, width), F32),
            pltpu.VMEM((t_lat, inner), F32),
            pltpu.SemaphoreType.DMA((2,)),
            pltpu.SemaphoreType.DMA((1,)),
            pltpu.VMEM((2 * nx, tot, LANES), BF16),
            pltpu.VMEM((ng, tot, LANES), BF16),
            pltpu.VMEM((ng, tot, LANES), BF16),
            pltpu.VMEM((ng, tot, LANES), F32),
            pltpu.VMEM((tot, LANES), F32),
            pltpu.VMEM((tot // SSD_CHUNK * 2 * SSD_HEADS, LANES), F32),
            pltpu.VMEM((tot // SSD_CHUNK * SUBLANES, LANES), F32),
            pltpu.VMEM((2 * nx, tot, LANES), BF16),
            pltpu.VMEM((2 * nx, SSD_STATE, LANES), F32),
            pltpu.VMEM((nx, tot, LANES), F32),
        ],
        compiler_params=_cparams("arbitrary"),
        name="ssd",
    )(xd_l.reshape(b, rows, GRID_W, width), xd_c, cw, cb, dtb, alog, dsk)
    y_l = res[0].reshape(b, t_lat, inner)
    return (y_l, res[1]) if need_ctx else (y_l, None)


def _outmlp_kernel(*args, final_norm, ff_chunk):
    if final_norm:
        (x_ref, lru_ref, ssd_ref, z_ref, mod_ref, sg_ref, n2_ref, wo_ref, w1_ref, w2_ref, fg_ref, o_ref) = args
    else:
        (x_ref, lru_ref, ssd_ref, z_ref, mod_ref, sg_ref, n2_ref, wo_ref, w1_ref, w2_ref, o_ref) = args
    x = x_ref[0]
    z = z_ref[0]
    lw = z.shape[-1]
    gw = lw // SSD_GROUPS
    parts = []
    for g in range(SSD_GROUPS):
        y = ssd_ref[0, :, g * gw:(g + 1) * gw] * _silu(z[:, g * gw:(g + 1) * gw])
        y = y * lax.rsqrt(jnp.mean(y * y, axis=-1, keepdims=True) + EPS)
        parts.append(y * sg_ref[:, g * gw:(g + 1) * gw])
    ssd_n = jnp.concatenate(parts, axis=1).astype(BF16)
    lru = lru_ref[0].astype(BF16)
    y = _dot(lru, wo_ref[0:lw, :]) + _dot(ssd_n, wo_ref[lw:2 * lw, :])
    x1 = x + mod_ref[0, 2:3, :] * y
    h = (_rmsnorm(x1, n2_ref[...]) * (1.0 + mod_ref[0, 4:5, :]) + mod_ref[0, 3:4, :]).astype(BF16)
    dff = w1_ref.shape[1]
    acc = jnp.zeros_like(x1)
    for j in range(dff // ff_chunk):
        a = _dot(h, w1_ref[:, j * ff_chunk:(j + 1) * ff_chunk])
        a = jnp.square(jnp.maximum(a, 0.0)).astype(BF16)
        acc = acc + _dot(a, w2_ref[j * ff_chunk:(j + 1) * ff_chunk, :])
    x2 = x1 + mod_ref[0, 5:6, :] * acc
    if final_norm:
        x2 = _rmsnorm(x2, fg_ref[...])
    o_ref[0] = x2


def _outmlp(x, lru, ssd, z, mod, mod_row, sg, n2, wo, w1, w2, fg, *, tm, ff_chunk=1024):
    b, t, d = x.shape
    lw = lru.shape[-1]
    final_norm = fg is not None
    kern = functools.partial(_outmlp_kernel, final_norm=final_norm, ff_chunk=ff_chunk)
    tok = lambda n: pl.BlockSpec((1, tm, n), lambda i, j: (i, j, 0))
    res = lambda a: pl.BlockSpec(a.shape, lambda i, j: (0, 0), pipeline_mode=pl.Buffered(1))
    row = lambda n: pl.BlockSpec((1, n), lambda i, j: (0, 0))
    ins = [x, lru, ssd, z, mod, sg, n2, wo, w1, w2]
    in_specs = [tok(d), tok(lw), tok(lw), tok(lw),
                pl.BlockSpec((1, N_MOD, d), lambda i, j: (mod_row(i), 0, 0)),
                row(lw), row(d), res(wo), res(w1), res(w2)]
    if final_norm:
        ins.append(fg)
        in_specs.append(row(d))
    return pl.pallas_call(
        kern,
        out_shape=jax.ShapeDtypeStruct((b, t, d), F32),
        grid=(b, t // tm),
        in_specs=in_specs,
        out_specs=tok(d),
        compiler_params=_cparams("arbitrary", "arbitrary"),
        name="outmlp",
    )(*ins)


def _pair_block_diag(w):
    two, h, k, _ = w.shape
    w = w.reshape(two, h // 2, 2, k, k)
    z = jnp.zeros_like(w[:, :, 0])
    top = jnp.concatenate([w[:, :, 0], z], axis=-1)
    bot = jnp.concatenate([z, w[:, :, 1]], axis=-1)
    return jnp.concatenate([top, bot], axis=-2)


def _pad_lanes(a):
    return jnp.pad(a, [(0, 0)] * (a.ndim - 1) + [(0, LANES - a.shape[-1])])


def kernel(x, c, ctx, c_ctx, ada_w, ada_b, norm1_g, norm2_g, w_in, lru_conv_w, lru_conv_b, lru_wa, lru_ba,
           lru_wx, lru_bx, lru_lambda, ssd_conv_w, ssd_conv_b, ssd_dt_bias, ssd_a_log, ssd_d, ssd_norm_g,
           w_out, mlp_w1, mlp_w2, final_g):
    bsz, t_lat, d = x.shape
    t_ctx = ctx.shape[1]
    depth = ada_w.shape[0]
    lw = lru_conv_w.shape[-1]
    nxbc = ssd_conv_w.shape[-1]
    ndt = 2 * SSD_HEADS
    nscan = lw + nxbc + ndt
    assert bsz + 1 <= MOD_ROWS and t_lat % (GRID_W * SUBLANES) == 0 and t_ctx % SSD_CHUNK == 0

    cvec = jnp.zeros((MOD_ROWS, d), F32).at[:bsz].set(c).at[bsz].set(c_ctx)
    mod = _modulation(cvec, ada_w, ada_b).reshape(depth, MOD_ROWS, N_MOD, d)

    lat_row = lambda i: i
    ctx_row = lambda i: bsz
    tm_lat = 512
    tm_ctx = t_ctx
    nslab = nxbc // LANES

    for l in range(depth):
        need_ctx = l < depth - 1
        wl = w_in[l]
        w_re = jnp.concatenate(
            [wl[:, :lw + nxbc], _pad_lanes(wl[:, lw + nxbc:nscan]), wl[:, nscan:]], axis=1).astype(BF16)
        g1 = norm1_g[l].reshape(1, d)
        nxd = nxbc + LANES
        lx_l, xd_l, lg_l, z_l = _inproj(x, mod[l], lat_row, g1, w_re, lw=lw, nxd=nxd, tm=tm_lat)
        lx_c, xd_c, lg_c, z_c = _inproj(ctx, mod[l], ctx_row, g1, w_re, lw=lw, nxd=nxd, tm=tm_ctx)

        lru_l, lru_c = _lru(
            lx_l, lx_c, lg_l, lg_c, lru_conv_w[l], lru_conv_b[l].reshape(1, lw),
            _pair_block_diag(0.5 * lru_wa[l]).astype(BF16), _pair_block_diag(0.5 * lru_wx[l]).astype(BF16),
            0.5 * lru_ba[l], 0.5 * lru_bx[l], lru_lambda[l], need_ctx=need_ctx)

        cw = 0.5 * ssd_conv_w[l].reshape(CONV_K, nslab, LANES).transpose(1, 0, 2)
        cb = 0.5 * ssd_conv_b[l].reshape(nslab, 1, LANES)
        dtb = _pad_lanes(ssd_dt_bias[l].reshape(1, ndt))
        alog = _pad_lanes(ssd_a_log[l].reshape(1, ndt))
        dsk = jnp.repeat(ssd_d[l], SSD_HEAD_DIM).reshape(lw // LANES, 1, LANES)
        ssd_l, ssd_c = _ssd(xd_l, xd_c, cw, cb, dtb, alog, dsk, need_ctx=need_ctx)

        sg = ssd_norm_g[l].reshape(1, lw)
        n2 = norm2_g[l].reshape(1, d)
        wo = w_out[l].astype(BF16)
        w1 = mlp_w1[l].astype(BF16)
        w2 = mlp_w2[l].astype(BF16)
        fg = final_g.reshape(1, d) if l == depth - 1 else None
        x = _outmlp(x, lru_l, ssd_l, z_l, mod[l], lat_row, sg, n2, wo, w1, w2, fg, tm=tm_lat)
        if need_ctx:
            ctx = _outmlp(ctx, lru_c, ssd_c, z_c, mod[l], ctx_row, sg, n2, wo, w1, w2, None, tm=tm_ctx)
    return x
```

```python
import functools
import math

import jax
import jax.numpy as jnp
from jax import lax
from jax.experimental import pallas as pl
from jax.experimental.pallas import tpu as pltpu

F32 = jnp.float32
BF16 = jnp.bfloat16

EPS = 1e-6
GRID_W = 64
LRU_BLOCK = 64
LRU_C = 8.0
SSD_HEAD_DIM = 64
SSD_HEADS = 8
SSD_GROUPS = 2
SSD_STATE = 128
SSD_CHUNK = 128
CONV_K = 4
N_MOD = 6
LANES = 128
SUBLANES = 8
MOD_ROWS = 16
VMEM_LIMIT = 56 * 1024 * 1024
NEG_BIG = -1e30


def _sigmoid(x):
    return 0.5 * (jnp.tanh(0.5 * x) + 1.0)


def _silu(x):
    return x * _sigmoid(x)


def _softplus(x):
    return jnp.maximum(x, 0.0) + jnp.log1p(jnp.exp(-jnp.abs(x)))


def _gelu_tanh(x):
    c = math.sqrt(2.0 / math.pi)
    return (0.5 * x) * (1.0 + jnp.tanh(x * (c + (c * 0.044715) * (x * x))))


def _rmsnorm(x, g):
    return x * lax.rsqrt(jnp.mean(x * x, axis=-1, keepdims=True) + EPS) * g


def _dot(a, b):
    return jnp.dot(a, b, preferred_element_type=F32)


def _cparams(*sem):
    return pltpu.CompilerParams(dimension_semantics=sem, vmem_limit_bytes=VMEM_LIMIT)


def _mod_kernel(c_ref, w_ref, b_ref, o_ref):
    s = _silu(c_ref[...]).astype(BF16)
    o_ref[0] = _dot(s, w_ref[0].astype(BF16)) + b_ref[0]


def _modulation(cvec, ada_w, ada_b):
    depth, d, n = ada_w.shape
    tn = 512
    return pl.pallas_call(
        _mod_kernel,
        out_shape=jax.ShapeDtypeStruct((depth, MOD_ROWS, n), F32),
        grid=(depth, n // tn),
        in_specs=[
            pl.BlockSpec((MOD_ROWS, d), lambda l, j: (0, 0)),
            pl.BlockSpec((1, d, tn), lambda l, j: (l, 0, j)),
            pl.BlockSpec((1, 1, tn), lambda l, j: (l, 0, j)),
        ],
        out_specs=pl.BlockSpec((1, MOD_ROWS, tn), lambda l, j: (l, 0, j)),
        compiler_params=_cparams("arbitrary", "arbitrary"),
        name="modulation",
    )(cvec, ada_w, ada_b.reshape(depth, 1, n))


def _inproj_kernel(x_ref, mod_ref, g_ref, w_ref, lx_ref, xd_ref, lg_ref, z_ref, *, lw, nxd):
    x = x_ref[0]
    h = _rmsnorm(x, g_ref[...]) * (1.0 + mod_ref[0, 1:2, :]) + mod_ref[0, 0:1, :]
    h = h.astype(BF16)
    lx_ref[0] = _dot(h, w_ref[:, 0:lw])
    xd_ref[0] = _dot(h, w_ref[:, lw:lw + nxd])
    o = lw + nxd
    lg_ref[0] = _dot(h, w_ref[:, o:o + lw])
    z_ref[0] = _dot(h, w_ref[:, o + lw:o + 2 * lw])


def _inproj(x, mod, mod_row, g, w, *, lw, nxd, tm):
    b, t, d = x.shape
    kern = functools.partial(_inproj_kernel, lw=lw, nxd=nxd)
    tok = lambda n: pl.BlockSpec((1, tm, n), lambda i, j: (i, j, 0))
    return pl.pallas_call(
        kern,
        out_shape=(
            jax.ShapeDtypeStruct((b, t, lw), F32),
            jax.ShapeDtypeStruct((b, t, nxd), F32),
            jax.ShapeDtypeStruct((b, t, lw), F32),
            jax.ShapeDtypeStruct((b, t, lw), F32),
        ),
        grid=(b, t // tm),
        in_specs=[
            tok(d),
            pl.BlockSpec((1, N_MOD, d), lambda i, j: (mod_row(i), 0, 0)),
            pl.BlockSpec((1, d), lambda i, j: (0, 0)),
            pl.BlockSpec(w.shape, lambda i, j: (0, 0), pipeline_mode=pl.Buffered(1)),
        ],
        out_specs=(tok(lw), tok(nxd), tok(lw), tok(lw)),
        compiler_params=_cparams("arbitrary", "arbitrary"),
        name="inproj",
    )(x, mod, g, w)


LRU_ROWS = 256


def _lru_seq(t, u_ref, lg_ref, out_ref, h0, refs, scr, need_out):
    cw_ref, cb_ref, wa_ref, wx_ref, ba_ref, bx_ref, lam_ref = refs
    upad, ubuf, uint, hbuf, pbuf, ybuf = scr
    seg = t // SUBLANES
    pitch = seg + 4
    nblk = max(t // LRU_ROWS, 1)
    rb = t // nblk
    nv = rb // SUBLANES

    zeros8 = jnp.zeros((SUBLANES, LANES), F32)
    upad[0:SUBLANES, :] = zeros8
    upad[SUBLANES:SUBLANES + t, :] = u_ref[...]
    upad[SUBLANES + t:2 * SUBLANES + t, :] = zeros8

    cw = cw_ref[...]
    n = seg + 2 * SUBLANES
    for s in range(SUBLANES):
        v = upad[s * seg:s * seg + n, :]
        acc = cb_ref[...] + cw[1:2, :] * v
        acc = acc + cw[0:1, :] * pltpu.roll(v, 1, 0)
        acc = acc + cw[2:3, :] * pltpu.roll(v, n - 1, 0)
        acc = acc + cw[3:4, :] * pltpu.roll(v, n - 2, 0)
        ubuf[s * pitch:s * pitch + seg, :] = acc[SUBLANES:SUBLANES + seg, :]

    for tau in range(seg):
        uint[tau * SUBLANES:(tau + 1) * SUBLANES, :] = ubuf[pl.ds(tau, SUBLANES, stride=pitch), :]

    row = lax.broadcasted_iota(jnp.int32, (SUBLANES, LANES), 0)
    finals = []
    cmats = []
    kdec = [(-0.5 * LRU_C) * _softplus(-lam_ref[d:d + 1, :]) for d in range(2)]

    def blk_body(i, carry):
        carry = list(carry)
        for d in range(2):
            h, p = carry[2 * d], carry[2 * d + 1]
            blk = i if d == 0 else nblk - 1 - i
            r0 = pl.multiple_of(blk * rb, rb)
            u = uint[pl.ds(r0, rb), :]
            ub = u.astype(BF16)
            log_a = kdec[d] * jnp.tanh(_dot(ub, wa_ref[d, 0]) + ba_ref[d:d + 1, :]) + kdec[d]
            gate = jnp.tanh(_dot(ub, wx_ref[d, 0]) + bx_ref[d:d + 1, :]) + 1.0
            a = jnp.exp(log_a)
            v = jnp.tanh(-log_a) * (1.0 + a * a)
            root = jnp.where(v > 0.0, v * lax.rsqrt(v), 0.0)
            bb = root * (gate * (0.5 * u))
            hs = [None] * nv
            ps = [None] * nv
            order = range(nv) if d == 0 else range(nv - 1, -1, -1)
            for k in order:
                ak = a[k * SUBLANES:(k + 1) * SUBLANES, :]
                h = ak * h + bb[k * SUBLANES:(k + 1) * SUBLANES, :]
                p = ak * p
                hs[k] = h
                ps[k] = p
            hbuf[d, pl.ds(r0, rb), :] = jnp.concatenate(hs, axis=0)
            pbuf[d, pl.ds(r0, rb), :] = jnp.concatenate(ps, axis=0)
            carry[2 * d], carry[2 * d + 1] = h, p
        return tuple(carry)

    ones8 = jnp.ones((SUBLANES, LANES), F32)
    hp = lax.fori_loop(0, nblk, blk_body, (zeros8, ones8, zeros8, ones8), unroll=2 if nblk % 2 == 0 else 1)
    for d in range(2):
        h, p = hp[2 * d], hp[2 * d + 1]
        c = h0[d]
        cmat = zeros8
        order = range(SUBLANES) if d == 0 else range(SUBLANES - 1, -1, -1)
        for s in order:
            cmat = jnp.where(row == s, jnp.broadcast_to(c, (SUBLANES, LANES)), cmat)
            c = h[s:s + 1, :] + p[s:s + 1, :] * c
        finals.append(c)
        cmats.append(cmat)

    if need_out:
        def out_body(i, carry):
            r0 = pl.multiple_of(i * rb, rb)
            y = hbuf[0, pl.ds(r0, rb), :].reshape(nv, SUBLANES, LANES)
            y = y + pbuf[0, pl.ds(r0, rb), :].reshape(nv, SUBLANES, LANES) * cmats[0][None]
            y = y + hbuf[1, pl.ds(r0, rb), :].reshape(nv, SUBLANES, LANES)
            y = y + pbuf[1, pl.ds(r0, rb), :].reshape(nv, SUBLANES, LANES) * cmats[1][None]
            ybuf[pl.ds(r0, rb), :] = y.reshape(rb, LANES)
            return carry

        lax.fori_loop(0, nblk, out_body, 0)
        for tau in range(seg):
            ubuf[pl.ds(tau, SUBLANES, stride=pitch), :] = ybuf[tau * SUBLANES:(tau + 1) * SUBLANES, :]
        for s in range(SUBLANES):
            out_ref[s * seg:(s + 1) * seg, :] = (
                ubuf[s * pitch:s * pitch + seg, :] * _gelu_tanh(lg_ref[s * seg:(s + 1) * seg, :]))
    return finals


def _lru_kernel(*args, t_lat, t_ctx, need_ctx):
    if need_ctx:
        (ul_ref, uc_ref, lgl_ref, lgc_ref, *refs, yl_ref, yc_ref, upad, ubuf, uint, hbuf, pbuf, ybuf) = args
    else:
        (ul_ref, uc_ref, lgl_ref, *refs, yl_ref, upad, ubuf, uint, hbuf, pbuf, ybuf) = args
        lgc_ref = yc_ref = None
    scr = (upad, ubuf, uint, hbuf, pbuf, ybuf)
    zero = jnp.zeros((1, LANES), F32)
    ctx_final = _lru_seq(t_ctx, uc_ref.at[0], lgc_ref.at[0] if need_ctx else None,
                         yc_ref.at[0] if need_ctx else None, (zero, zero), refs, scr, need_ctx)
    _lru_seq(t_lat, ul_ref.at[0], lgl_ref.at[0], yl_ref.at[0], ctx_final, refs, scr, True)


def _lru(lx_l, lx_c, lg_l, lg_c, conv_w, conv_b, wa_bd, wx_bd, ba, bx, lam, *, need_ctx):
    b, t_lat, w = lx_l.shape
    t_ctx = lx_c.shape[1]
    nt = w // LANES
    seq = lambda t: pl.BlockSpec((1, t, LANES), lambda i, j: (i, 0, j))
    par = lambda r: pl.BlockSpec((r, LANES), lambda i, j: (0, j))
    gate = pl.BlockSpec((2, 1, LANES, LANES), lambda i, j: (0, j, 0, 0))
    kern = functools.partial(_lru_kernel, t_lat=t_lat, t_ctx=t_ctx, need_ctx=need_ctx)
    ins = [lx_l, lx_c, lg_l] + ([lg_c] if need_ctx else [])
    in_specs = [seq(t_lat), seq(t_ctx), seq(t_lat)] + ([seq(t_ctx)] if need_ctx else [])
    ins += [conv_w, conv_b, wa_bd, wx_bd, ba, bx, lam]
    in_specs += [par(CONV_K), par(1), gate, gate, par(2), par(2), par(2)]
    out_shape = [jax.ShapeDtypeStruct((b, t_lat, w), F32)]
    out_specs = [seq(t_lat)]
    if need_ctx:
        out_shape.append(jax.ShapeDtypeStruct((b, t_ctx, w), F32))
        out_specs.append(seq(t_ctx))
    pitch_rows = SUBLANES * (t_lat // SUBLANES + 4) + SUBLANES
    res = pl.pallas_call(
        kern,
        out_shape=tuple(out_shape),
        grid=(b, nt),
        in_specs=in_specs,
        out_specs=tuple(out_specs),
        scratch_shapes=[
            pltpu.VMEM((t_lat + 2 * SUBLANES, LANES), F32),
            pltpu.VMEM((pitch_rows, LANES), F32),
            pltpu.VMEM((t_lat, LANES), F32),
            pltpu.VMEM((2, t_lat, LANES), F32),
            pltpu.VMEM((2, t_lat, LANES), F32),
            pltpu.VMEM((t_lat, LANES), F32),
        ],
        compiler_params=_cparams("arbitrary", "arbitrary"),
        name="rglru",
    )(*ins)
    return (res[0], res[1]) if need_ctx else (res[0], None)


def _ssd_kernel(*args, t_lat, t_ctx, need_ctx):
    if need_ctx:
        (xl_hbm, xc_hbm, cw_ref, cb_ref, dtb_ref, alog_ref, dsk_ref, yl_hbm, yc_ref,
         xg, ybuf, sem_in, sem_out, stg, xm, scc, sbt, cumb, srct, etb, sbuf, hst, yacc) = args
    else:
        (xl_hbm, xc_hbm, cw_ref, cb_ref, dtb_ref, alog_ref, dsk_ref, yl_hbm,
         xg, ybuf, sem_in, sem_out, stg, xm, scc, sbt, cumb, srct, etb, sbuf, hst, yacc) = args
        yc_ref = None
    ck = SSD_CHUNK
    rows = t_lat // GRID_W
    nc_ctx = t_ctx // ck
    nc_lat = t_lat // ck
    nx = SSD_HEADS * SSD_HEAD_DIM // LANES
    ng = SSD_GROUPS
    nslab = nx + 2 * ng
    halo = SUBLANES

    ri = lax.broadcasted_iota(jnp.int32, (ck, ck), 0)
    ci = lax.broadcasted_iota(jnp.int32, (ck, ck), 1)
    lane_lo = ci < SSD_HEAD_DIM

    b = pl.program_id(0)
    nb = pl.num_programs(0)
    slot = lax.rem(b, 2)
    ctx0 = halo
    lat0 = ctx0 + t_ctx + halo
    width = xg.shape[-1]

    def in_copies(bi, sl):
        cps = [pltpu.make_async_copy(xc_hbm.at[bi], xg.at[sl, pl.ds(ctx0, t_ctx), :], sem_in.at[sl])]
        for w in range(GRID_W):
            cps.append(pltpu.make_async_copy(
                xl_hbm.at[bi, :, w, :], xg.at[sl, pl.ds(lat0 + w * rows, rows), :], sem_in.at[sl]))
        return cps

    def out_copies(bi):
        return [pltpu.make_async_copy(ybuf.at[pl.ds(w * rows, rows), :], yl_hbm.at[bi, :, w, :], sem_out.at[0])
                for w in range(GRID_W)]

    @pl.when(b == 0)
    def _():
        zpad = jnp.zeros((halo, width), F32)
        for sl in range(2):
            for r in (0, ctx0 + t_ctx, lat0 + t_lat):
                xg[sl, r:r + halo, :] = zpad
        for cp in in_copies(0, 0):
            cp.start()

    @pl.when(b + 1 < nb)
    def _():
        for cp in in_copies(b + 1, 1 - slot):
            cp.start()

    for cp in in_copies(b, slot):
        cp.wait()

    xs = xg.at[slot]

    def src_row(k):
        return pl.multiple_of(jnp.where(k < nc_ctx, ctx0 + k * ck, lat0 + (k - nc_ctx) * ck), SUBLANES)

    def conv_silu(src0, j):
        stg[j] = xs[pl.ds(src0 - halo, ck + 2 * halo), j * LANES:(j + 1) * LANES]
        cw = cw_ref[j]
        half = cb_ref[j]
        for tap in range(CONV_K):
            half = half + cw[tap:tap + 1, :] * stg[j, halo + tap - 1:halo + tap - 1 + ck, :]
        return half * jnp.tanh(half) + half

    dtb = dtb_ref[...]
    lane_ok = lax.broadcasted_iota(jnp.int32, (1, LANES), 1) < 2 * SSD_HEADS
    nega2 = jnp.where(lane_ok, -jnp.exp(alog_ref[...]) * math.log2(math.e), 0.0)

    nck = nc_ctx + nc_lat
    nh2 = 2 * SSD_HEADS
    hpp = SSD_HEADS // ng // 2
    dir1_col = lax.broadcasted_iota(jnp.int32, (1, LANES), 1) >= SSD_HEADS
    tris = (ri >= ci, ri <= ci)

    def xcat_of(j, r0):
        return jnp.concatenate([xm[j, pl.ds(r0, ck), :], xm[nx + j, pl.ds(r0, ck), :]], axis=0)

    def pass1(kc, carry):
        r0 = pl.multiple_of(kc * ck, ck)
        src0 = src_row(kc)
        dt = _softplus(xs[pl.ds(src0, ck), nslab * LANES:(nslab + 1) * LANES] + dtb)
        la = dt * nega2
        cumf = la
        for sh in (1, 2, 4):
            cumf = cumf + jnp.where(ri >= sh, pltpu.roll(cumf, sh, 0), 0.0)
        for sh in (8, 16, 32, 64):
            cumf = cumf + jnp.concatenate([jnp.zeros((sh, LANES), F32), cumf[0:ck - sh, :]], axis=0)
        tot = cumf[ck - 1:ck, :]
        cum = jnp.where(dir1_col, tot - cumf + la, cumf)
        ldt = jnp.log2(dt)
        cumb[pl.ds(r0, ck), :] = cum
        srct[pl.ds(pl.multiple_of(kc * nh2, nh2), nh2), :] = (cum - ldt).T[0:nh2, :]
        wgt_t = jnp.exp2((tot - cum + ldt).T[0:nh2, :])
        etb[pl.ds(pl.multiple_of(kc * SUBLANES, SUBLANES), SUBLANES), :] = jnp.broadcast_to(
            jnp.exp2(tot), (SUBLANES, LANES))
        for g in range(ng):
            scc[g, pl.ds(r0, ck), :] = conv_silu(src0, nx + ng + g).astype(BF16)
            btf = conv_silu(src0, nx + g).T
            sbt[g, pl.ds(r0, ck), :] = btf.astype(BF16)
            for pr in range(hpp):
                j = g * hpp + pr
                x = conv_silu(src0, j)
                yacc[j, pl.ds(r0, ck), :] = dsk_ref[j] * x
                xb = x.astype(BF16)
                zb = jnp.zeros_like(xb)
                xlo = jnp.where(lane_lo, xb, zb)
                xhi = jnp.where(lane_lo, zb, xb)
                xm[j, pl.ds(r0, ck), :] = xlo
                xm[nx + j, pl.ds(r0, ck), :] = xhi
                btw = []
                for d in range(2):
                    hc = (d * SSD_HEADS + 2 * j, d * SSD_HEADS + 2 * j + 1)
                    btw.append(jnp.concatenate([(btf * wgt_t[h:h + 1, :]).astype(BF16) for h in hc], axis=1))
                s = _dot(jnp.concatenate(btw, axis=0), jnp.concatenate([xlo, xhi], axis=0)).astype(BF16)
                sbuf[j, pl.ds(r0, ck), :] = s[0:SSD_STATE, :]
                sbuf[nx + j, pl.ds(r0, ck), :] = s[SSD_STATE:2 * SSD_STATE, :]
        return carry

    lax.fori_loop(0, nck, pass1, 0, unroll=2)

    hst[...] = jnp.zeros(hst.shape, F32)

    def pass2(i, carry):
        for d in range(2):
            if d == 0:
                kc = i
            else:
                kc = jnp.where(i < nc_ctx, nc_ctx - 1 - i, nck - 1 - (i - nc_ctx))
            r0 = pl.multiple_of(kc * ck, ck)
            e = etb[pl.ds(pl.multiple_of(kc * SUBLANES, SUBLANES), 1), :]
            for j in range(nx):
                hc = (d * SSD_HEADS + 2 * j, d * SSD_HEADS + 2 * j + 1)
                et = jnp.where(lane_lo[0:1, :], jnp.broadcast_to(e[:, hc[0]:hc[0] + 1], (1, LANES)),
                               jnp.broadcast_to(e[:, hc[1]:hc[1] + 1], (1, LANES)))
                h = hst[d * nx + j]
                s = sbuf[d * nx + j, pl.ds(r0, ck), :].astype(F32)
                sbuf[d * nx + j, pl.ds(r0, ck), :] = h.astype(BF16)
                hst[d * nx + j] = h * et + s
        return carry

    lax.fori_loop(0, nck, pass2, 0)

    def pass3(kc, is_ctx):
        r0 = kc * ck if isinstance(kc, int) else pl.multiple_of(kc * ck, ck)
        s0 = kc * nh2 if isinstance(kc, int) else pl.multiple_of(kc * nh2, nh2)
        cum = cumb[pl.ds(r0, ck), :]
        src_t = srct[pl.ds(s0, nh2), :]
        for g in range(ng):
            cm = scc[g, pl.ds(r0, ck), :]
            cb = _dot(cm, sbt[g, pl.ds(r0, ck), :])
            for pr in range(hpp):
                j = g * hpp + pr
                dec = []
                ecs = []
                for d in range(2):
                    hc = (d * SSD_HEADS + 2 * j, d * SSD_HEADS + 2 * j + 1)
                    col = [jnp.broadcast_to(cum[:, h:h + 1], (ck, LANES)) for h in hc]
                    dec.append([jnp.exp2(jnp.where(tris[d], col[q] - src_t[hc[q]:hc[q] + 1, :], NEG_BIG))
                                for q in range(2)])
                    ecs.append(jnp.exp2(jnp.where(lane_lo, col[0], col[1])))
                ms = [(cb * (dec[0][q] + dec[1][q])).astype(BF16) for q in range(2)]
                y = yacc[j, pl.ds(r0, ck), :] + _dot(jnp.concatenate(ms, axis=1), xcat_of(j, r0))
                hin = jnp.concatenate([sbuf[j, pl.ds(r0, ck), :], sbuf[nx + j, pl.ds(r0, ck), :]], axis=1)
                yoff = _dot(cm, hin)
                y = y + yoff[:, 0:LANES] * ecs[0] + yoff[:, LANES:2 * LANES] * ecs[1]
                if is_ctx:
                    yc_ref[0, pl.ds(r0, ck), j * LANES:(j + 1) * LANES] = y
                else:
                    y0 = pl.multiple_of((kc - nc_ctx) * ck, ck)
                    ybuf[pl.ds(y0, ck), j * LANES:(j + 1) * LANES] = y

    if need_ctx:
        for kc in range(nc_ctx):
            pass3(kc, True)

    @pl.when(b > 0)
    def _():
        for cp in out_copies(b - 1):
            cp.wait()

    def lat_body(c, carry):
        pass3(nc_ctx + c, False)
        return carry

    lax.fori_loop(0, nc_lat, lat_body, 0, unroll=2)

    for cp in out_copies(b):
        cp.start()

    @pl.when(b == nb - 1)
    def _():
        for cp in out_copies(b):
            cp.wait()


def _ssd(xd_l, xd_c, cw, cb, dtb, alog, dsk, *, need_ctx):
    b, t_lat, width = xd_l.shape
    t_ctx = xd_c.shape[1]
    nx = SSD_HEADS * SSD_HEAD_DIM // LANES
    ng = SSD_GROUPS
    inner = nx * LANES
    tot = t_lat + t_ctx
    rows = t_lat // GRID_W
    kern = functools.partial(_ssd_kernel, t_lat=t_lat, t_ctx=t_ctx, need_ctx=need_ctx)
    full = lambda a: pl.BlockSpec(a.shape, lambda i: (0,) * a.ndim)
    hbm = pl.BlockSpec(memory_space=pl.ANY)
    out_shape = [jax.ShapeDtypeStruct((b, rows, GRID_W, inner), F32)]
    out_specs = [hbm]
    if need_ctx:
        out_shape.append(jax.ShapeDtypeStruct((b, t_ctx, inner), F32))
        out_specs.append(pl.BlockSpec((1, t_ctx, inner), lambda i: (i, 0, 0)))
    res = pl.pallas_call(
        kern,
        out_shape=tuple(out_shape),
        grid=(b,),
        in_specs=[hbm, hbm, full(cw), full(cb), full(dtb), full(alog), full(dsk)],
        out_specs=tuple(out_specs),
        scratch_shapes=[
            pltpu.VMEM((2, tot + 3 * SUBLANES, width), F32),
            pltpu.VMEM((t_lat, inner), F32),
            pltpu.SemaphoreType.DMA((2,)),
            pltpu.SemaphoreType.DMA((1,)),
            pltpu.VMEM((nx + 2 * ng, SSD_CHUNK + 2 * SUBLANES, LANES), F32),
            pltpu.VMEM((2 * nx, tot, LANES), BF16),
            pltpu.VMEM((ng, tot, LANES), BF16),
            pltpu.VMEM((ng, tot, LANES), BF16),
            pltpu.VMEM((tot, LANES), F32),
            pltpu.VMEM((tot // SSD_CHUNK * 2 * SSD_HEADS, LANES), F32),
            pltpu.VMEM((tot // SSD_CHUNK * SUBLANES, LANES), F32),
            pltpu.VMEM((2 * nx, tot, LANES), BF16),
            pltpu.VMEM((2 * nx, SSD_STATE, LANES), F32),
            pltpu.VMEM((nx, tot, LANES), F32),
        ],
        compiler_params=_cparams("arbitrary"),
        name="ssd",
    )(xd_l.reshape(b, rows, GRID_W, width), xd_c, cw, cb, dtb, alog, dsk)
    y_l = res[0].reshape(b, t_lat, inner)
    return (y_l, res[1]) if need_ctx else (y_l, None)


def _outmlp_kernel(*args, final_norm, ff_chunk):
    if final_norm:
        (x_ref, lru_ref, ssd_ref, z_ref, mod_ref, sg_ref, n2_ref, wo_ref, w1_ref, w2_ref, fg_ref, o_ref) = args
    else:
        (x_ref, lru_ref, ssd_ref, z_ref, mod_ref, sg_ref, n2_ref, wo_ref, w1_ref, w2_ref, o_ref) = args
    x = x_ref[0]
    z = z_ref[0]
    lw = z.shape[-1]
    gw = lw // SSD_GROUPS
    parts = []
    for g in range(SSD_GROUPS):
        y = ssd_ref[0, :, g * gw:(g + 1) * gw] * _silu(z[:, g * gw:(g + 1) * gw])
        y = y * lax.rsqrt(jnp.mean(y * y, axis=-1, keepdims=True) + EPS)
        parts.append(y * sg_ref[:, g * gw:(g + 1) * gw])
    ssd_n = jnp.concatenate(parts, axis=1).astype(BF16)
    lru = lru_ref[0].astype(BF16)
    y = _dot(lru, wo_ref[0:lw, :]) + _dot(ssd_n, wo_ref[lw:2 * lw, :])
    x1 = x + mod_ref[0, 2:3, :] * y
    h = (_rmsnorm(x1, n2_ref[...]) * (1.0 + mod_ref[0, 4:5, :]) + mod_ref[0, 3:4, :]).astype(BF16)
    dff = w1_ref.shape[1]
    acc = jnp.zeros_like(x1)
    for j in range(dff // ff_chunk):
        a = _dot(h, w1_ref[:, j * ff_chunk:(j + 1) * ff_chunk])
        a = jnp.square(jnp.maximum(a, 0.0)).astype(BF16)
        acc = acc + _dot(a, w2_ref[j * ff_chunk:(j + 1) * ff_chunk, :])
    x2 = x1 + mod_ref[0, 5:6, :] * acc
    if final_norm:
        x2 = _rmsnorm(x2, fg_ref[...])
    o_ref[0] = x2


def _outmlp(x, lru, ssd, z, mod, mod_row, sg, n2, wo, w1, w2, fg, *, tm, ff_chunk=1024):
    b, t, d = x.shape
    lw = lru.shape[-1]
    final_norm = fg is not None
    kern = functools.partial(_outmlp_kernel, final_norm=final_norm, ff_chunk=ff_chunk)
    tok = lambda n: pl.BlockSpec((1, tm, n), lambda i, j: (i, j, 0))
    res = lambda a: pl.BlockSpec(a.shape, lambda i, j: (0, 0), pipeline_mode=pl.Buffered(1))
    row = lambda n: pl.BlockSpec((1, n), lambda i, j: (0, 0))
    ins = [x, lru, ssd, z, mod, sg, n2, wo, w1, w2]
    in_specs = [tok(d), tok(lw), tok(lw), tok(lw),
                pl.BlockSpec((1, N_MOD, d), lambda i, j: (mod_row(i), 0, 0)),
                row(lw), row(d), res(wo), res(w1), res(w2)]
    if final_norm:
        ins.append(fg)
        in_specs.append(row(d))
    return pl.pallas_call(
        kern,
        out_shape=jax.ShapeDtypeStruct((b, t, d), F32),
        grid=(b, t // tm),
        in_specs=in_specs,
        out_specs=tok(d),
        compiler_params=_cparams("arbitrary", "arbitrary"),
        name="outmlp",
    )(*ins)


def _pair_block_diag(w):
    two, h, k, _ = w.shape
    w = w.reshape(two, h // 2, 2, k, k)
    z = jnp.zeros_like(w[:, :, 0])
    top = jnp.concatenate([w[:, :, 0], z], axis=-1)
    bot = jnp.concatenate([z, w[:, :, 1]], axis=-1)
    return jnp.concatenate([top, bot], axis=-2)


def _pad_lanes(a):
    return jnp.pad(a, [(0, 0)] * (a.ndim - 1) + [(0, LANES - a.shape[-1])])


def kernel(x, c, ctx, c_ctx, ada_w, ada_b, norm1_g, norm2_g, w_in, lru_conv_w, lru_conv_b, lru_wa, lru_ba,
           lru_wx, lru_bx, lru_lambda, ssd_conv_w, ssd_conv_b, ssd_dt_bias, ssd_a_log, ssd_d, ssd_norm_g,
           w_out, mlp_w1, mlp_w2, final_g):
    bsz, t_lat, d = x.shape
    t_ctx = ctx.shape[1]
    depth = ada_w.shape[0]
    lw = lru_conv_w.shape[-1]
    nxbc = ssd_conv_w.shape[-1]
    ndt = 2 * SSD_HEADS
    nscan = lw + nxbc + ndt
    assert bsz + 1 <= MOD_ROWS and t_lat % (GRID_W * SUBLANES) == 0 and t_ctx % SSD_CHUNK == 0

    cvec = jnp.zeros((MOD_ROWS, d), F32).at[:bsz].set(c).at[bsz].set(c_ctx)
    mod = _modulation(cvec, ada_w, ada_b).reshape(depth, MOD_ROWS, N_MOD, d)

    lat_row = lambda i: i
    ctx_row = lambda i: bsz
    tm_lat = 512
    tm_ctx = t_ctx
    nslab = nxbc // LANES

    for l in range(depth):
        need_ctx = l < depth - 1
        wl = w_in[l]
        w_re = jnp.concatenate(
            [wl[:, :lw + nxbc], _pad_lanes(wl[:, lw + nxbc:nscan]), wl[:, nscan:]], axis=1).astype(BF16)
        g1 = norm1_g[l].reshape(1, d)
        nxd = nxbc + LANES
        lx_l, xd_l, lg_l, z_l = _inproj(x, mod[l], lat_row, g1, w_re, lw=lw, nxd=nxd, tm=tm_lat)
        lx_c, xd_c, lg_c, z_c = _inproj(ctx, mod[l], ctx_row, g1, w_re, lw=lw, nxd=nxd, tm=tm_ctx)

        lru_l, lru_c = _lru(
            lx_l, lx_c, lg_l, lg_c, lru_conv_w[l], lru_conv_b[l].reshape(1, lw),
            _pair_block_diag(0.5 * lru_wa[l]).astype(BF16), _pair_block_diag(0.5 * lru_wx[l]).astype(BF16),
            0.5 * lru_ba[l], 0.5 * lru_bx[l], lru_lambda[l], need_ctx=need_ctx)

        cw = 0.5 * ssd_conv_w[l].reshape(CONV_K, nslab, LANES).transpose(1, 0, 2)
        cb = 0.5 * ssd_conv_b[l].reshape(nslab, 1, LANES)
        dtb = _pad_lanes(ssd_dt_bias[l].reshape(1, ndt))
        alog = _pad_lanes(ssd_a_log[l].reshape(1, ndt))
        dsk = jnp.repeat(ssd_d[l], SSD_HEAD_DIM).reshape(lw // LANES, 1, LANES)
        ssd_l, ssd_c = _ssd(xd_l, xd_c, cw, cb, dtb, alog, dsk, need_ctx=need_ctx)

        sg = ssd_norm_g[l].reshape(1, lw)
        n2 = norm2_g[l].reshape(1, d)
        wo = w_out[l].astype(BF16)
        w1 = mlp_w1[l].astype(BF16)
        w2 = mlp_w2[l].astype(BF16)
        fg = final_g.reshape(1, d) if l == depth - 1 else None
        x = _outmlp(x, lru_l, ssd_l, z_l, mod[l], lat_row, sg, n2, wo, w1, w2, fg, tm=tm_lat)
        if need_ctx:
            ctx = _outmlp(ctx, lru_c, ssd_c, z_c, mod[l], ctx_row, sg, n2, wo, w1, w2, None, tm=tm_ctx)
    return x
```

```python
import functools
import math

import jax
import jax.numpy as jnp
from jax import lax
from jax.experimental import pallas as pl
from jax.experimental.pallas import tpu as pltpu

F32 = jnp.float32
BF16 = jnp.bfloat16

EPS = 1e-6
GRID_W = 64
LRU_BLOCK = 64
LRU_C = 8.0
SSD_HEAD_DIM = 64
SSD_HEADS = 8
SSD_GROUPS = 2
SSD_STATE = 128
SSD_CHUNK = 128
CONV_K = 4
N_MOD = 6
LANES = 128
SUBLANES = 8
MOD_ROWS = 16
VMEM_LIMIT = 56 * 1024 * 1024
NEG_BIG = -1e30


def _sigmoid(x):
    return 0.5 * (jnp.tanh(0.5 * x) + 1.0)


def _silu(x):
    return x * _sigmoid(x)


def _softplus(x):
    return jnp.maximum(x, 0.0) + jnp.log1p(jnp.exp(-jnp.abs(x)))


def _gelu_tanh(x):
    c = math.sqrt(2.0 / math.pi)
    return (0.5 * x) * (1.0 + jnp.tanh(x * (c + (c * 0.044715) * (x * x))))


def _rmsnorm(x, g):
    return x * lax.rsqrt(jnp.mean(x * x, axis=-1, keepdims=True) + EPS) * g


def _dot(a, b):
    return jnp.dot(a, b, preferred_element_type=F32)


def _cparams(*sem):
    return pltpu.CompilerParams(dimension_semantics=sem, vmem_limit_bytes=VMEM_LIMIT)


def _mod_kernel(c_ref, w_ref, b_ref, o_ref):
    s = _silu(c_ref[...]).astype(BF16)
    o_ref[0] = _dot(s, w_ref[0].astype(BF16)) + b_ref[0]


def _modulation(cvec, ada_w, ada_b):
    depth, d, n = ada_w.shape
    tn = 512
    return pl.pallas_call(
        _mod_kernel,
        out_shape=jax.ShapeDtypeStruct((depth, MOD_ROWS, n), F32),
        grid=(depth, n // tn),
        in_specs=[
            pl.BlockSpec((MOD_ROWS, d), lambda l, j: (0, 0)),
            pl.BlockSpec((1, d, tn), lambda l, j: (l, 0, j)),
            pl.BlockSpec((1, 1, tn), lambda l, j: (l, 0, j)),
        ],
        out_specs=pl.BlockSpec((1, MOD_ROWS, tn), lambda l, j: (l, 0, j)),
        compiler_params=_cparams("arbitrary", "arbitrary"),
        name="modulation",
    )(cvec, ada_w, ada_b.reshape(depth, 1, n))


def _inproj_kernel(x_ref, mod_ref, g_ref, w_ref, lx_ref, xd_ref, lg_ref, z_ref, *, lw, nxd):
    x = x_ref[0]
    h = _rmsnorm(x, g_ref[...]) * (1.0 + mod_ref[0, 1:2, :]) + mod_ref[0, 0:1, :]
    h = h.astype(BF16)
    lx_ref[0] = _dot(h, w_ref[:, 0:lw])
    xd_ref[0] = _dot(h, w_ref[:, lw:lw + nxd])
    o = lw + nxd
    lg_ref[0] = _dot(h, w_ref[:, o:o + lw])
    z_ref[0] = _dot(h, w_ref[:, o + lw:o + 2 * lw])


def _inproj(x, mod, mod_row, g, w, layer, *, lw, nxd, tm):
    b, t, d = x.shape
    kern = functools.partial(_inproj_kernel, lw=lw, nxd=nxd)
    tok = lambda n: pl.BlockSpec((1, tm, n), lambda i, j: (i, j, 0))
    return pl.pallas_call(
        kern,
        out_shape=(
            jax.ShapeDtypeStruct((b, t, lw), F32),
            jax.ShapeDtypeStruct((b, t, nxd), F32),
            jax.ShapeDtypeStruct((b, t, lw), F32),
            jax.ShapeDtypeStruct((b, t, lw), F32),
        ),
        grid=(b, t // tm),
        in_specs=[
            tok(d),
            pl.BlockSpec((1, N_MOD, d), lambda i, j: (mod_row(i), 0, 0)),
            pl.BlockSpec((1, d), lambda i, j: (0, 0)),
            pl.BlockSpec((None,) + w.shape[1:], lambda i, j: (layer, 0, 0), pipeline_mode=pl.Buffered(1)),
        ],
        out_specs=(tok(lw), tok(nxd), tok(lw), tok(lw)),
        compiler_params=_cparams("arbitrary", "arbitrary"),
        name="inproj",
    )(x, mod, g, w)


LRU_ROWS = 256
LRU_TILES = 2


def _lru_seq(t, u_ref, out_ref, h0, refs, scr, need_out):
    cw_ref, cb_ref, wa_ref, wx_ref, ba_ref, bx_ref, lam_ref = refs
    upad, ubuf, uint, hbuf, pbuf, ybuf = scr
    seg = t // SUBLANES
    pitch = seg + 4
    nblk = max(t // LRU_ROWS, 1)
    rb = t // nblk
    nv = rb // SUBLANES
    tiles = [slice(c * LANES, (c + 1) * LANES) for c in range(LRU_TILES)]

    zeros8 = jnp.zeros((SUBLANES, LANES), F32)
    ones8 = jnp.ones((SUBLANES, LANES), F32)
    for c, ln in enumerate(tiles):
        upad[c, 0:SUBLANES, :] = zeros8
        upad[c, SUBLANES:SUBLANES + t, :] = u_ref[:, ln]
        upad[c, SUBLANES + t:2 * SUBLANES + t, :] = zeros8
        cw = cw_ref[:, ln]
        for s in range(SUBLANES):
            acc = cb_ref[:, ln]
            for tap in range(CONV_K):
                r = SUBLANES + s * seg + tap - 1
                acc = acc + cw[tap:tap + 1, :] * upad[c, r:r + seg, :]
            ubuf[c, s * pitch:s * pitch + seg, :] = acc
        for tau in range(seg):
            uint[c, tau * SUBLANES:(tau + 1) * SUBLANES, :] = ubuf[c, pl.ds(tau, SUBLANES, stride=pitch), :]

    row = lax.broadcasted_iota(jnp.int32, (SUBLANES, LANES), 0)
    kdec = [[(-0.5 * LRU_C) * _softplus(-lam_ref[d:d + 1, ln]) for d in range(2)] for ln in tiles]
    chains = [(c, d) for c in range(LRU_TILES) for d in range(2)]

    def blk_body(i, carry):
        carry = list(carry)
        for q, (c, d) in enumerate(chains):
            ln = tiles[c]
            h, p = carry[2 * q], carry[2 * q + 1]
            blk = i if d == 0 else nblk - 1 - i
            r0 = pl.multiple_of(blk * rb, rb)
            u = uint[c, pl.ds(r0, rb), :]
            ub = u.astype(BF16)
            log_a = kdec[c][d] * jnp.tanh(_dot(ub, wa_ref[d, c]) + ba_ref[d:d + 1, ln]) + kdec[c][d]
            gate = jnp.tanh(_dot(ub, wx_ref[d, c]) + bx_ref[d:d + 1, ln]) + 1.0
            a = jnp.exp(log_a)
            v = jnp.tanh(-log_a) * (1.0 + a * a)
            root = jnp.where(v > 0.0, v * lax.rsqrt(v), 0.0)
            bb = root * (gate * (0.5 * u))
            hs = [None] * nv
            ps = [None] * nv
            order = range(nv) if d == 0 else range(nv - 1, -1, -1)
            for k in order:
                ak = a[k * SUBLANES:(k + 1) * SUBLANES, :]
                h = ak * h + bb[k * SUBLANES:(k + 1) * SUBLANES, :]
                p = ak * p
                hs[k] = h
                ps[k] = p
            hbuf[q, pl.ds(r0, rb), :] = jnp.concatenate(hs, axis=0)
            pbuf[q, pl.ds(r0, rb), :] = jnp.concatenate(ps, axis=0)
            carry[2 * q], carry[2 * q + 1] = h, p
        return tuple(carry)

    hp = lax.fori_loop(0, nblk, blk_body, (zeros8, ones8) * len(chains))
    finals = []
    cmats = []
    for q, (c, d) in enumerate(chains):
        h, p = hp[2 * q], hp[2 * q + 1]
        st = h0[q]
        cmat = zeros8
        order = range(SUBLANES) if d == 0 else range(SUBLANES - 1, -1, -1)
        for s in order:
            cmat = jnp.where(row == s, jnp.broadcast_to(st, (SUBLANES, LANES)), cmat)
            st = h[s:s + 1, :] + p[s:s + 1, :] * st
        finals.append(st)
        cmats.append(cmat)

    if need_out:
        def out_body(i, carry):
            r0 = pl.multiple_of(i * rb, rb)
            for c in range(LRU_TILES):
                y = None
                for q in (2 * c, 2 * c + 1):
                    t1 = hbuf[q, pl.ds(r0, rb), :].reshape(nv, SUBLANES, LANES)
                    t2 = pbuf[q, pl.ds(r0, rb), :].reshape(nv, SUBLANES, LANES) * cmats[q][None]
                    y = t1 + t2 if y is None else y + t1 + t2
                ybuf[c, pl.ds(r0, rb), :] = y.reshape(rb, LANES)
            return carry

        lax.fori_loop(0, nblk, out_body, 0)
        for c, ln in enumerate(tiles):
            for tau in range(seg):
                ubuf[c, pl.ds(tau, SUBLANES, stride=pitch), :] = ybuf[c, tau * SUBLANES:(tau + 1) * SUBLANES, :]
            for s in range(SUBLANES):
                out_ref[s * seg:(s + 1) * seg, ln] = ubuf[c, s * pitch:s * pitch + seg, :]
    return finals


def _lru_kernel(*args, t_lat, t_ctx, need_ctx):
    if need_ctx:
        (ul_ref, uc_ref, *refs, yl_ref, yc_ref, upad, ubuf, uint, hbuf, pbuf, ybuf) = args
    else:
        (ul_ref, uc_ref, *refs, yl_ref, upad, ubuf, uint, hbuf, pbuf, ybuf) = args
        yc_ref = None
    scr = (upad, ubuf, uint, hbuf, pbuf, ybuf)
    zero = jnp.zeros((1, LANES), F32)
    ctx_final = _lru_seq(t_ctx, uc_ref.at[0], yc_ref.at[0] if need_ctx else None,
                         (zero,) * (2 * LRU_TILES), refs, scr, need_ctx)
    _lru_seq(t_lat, ul_ref.at[0], yl_ref.at[0], ctx_final, refs, scr, True)


def _lru(lx_l, lx_c, conv_w, conv_b, wa_bd, wx_bd, ba, bx, lam, *, need_ctx):
    b, t_lat, w = lx_l.shape
    t_ctx = lx_c.shape[1]
    cwid = LRU_TILES * LANES
    seq = lambda t: pl.BlockSpec((1, t, cwid), lambda i, j: (i, 0, j))
    par = lambda r: pl.BlockSpec((r, cwid), lambda i, j: (0, j))
    gate = pl.BlockSpec((2, LRU_TILES, LANES, LANES), lambda i, j: (0, j, 0, 0))
    kern = functools.partial(_lru_kernel, t_lat=t_lat, t_ctx=t_ctx, need_ctx=need_ctx)
    ins = [lx_l, lx_c, conv_w, conv_b, wa_bd, wx_bd, ba, bx, lam]
    in_specs = [seq(t_lat), seq(t_ctx), par(CONV_K), par(1), gate, gate, par(2), par(2), par(2)]
    out_shape = [jax.ShapeDtypeStruct((b, t_lat, w), F32)]
    out_specs = [seq(t_lat)]
    if need_ctx:
        out_shape.append(jax.ShapeDtypeStruct((b, t_ctx, w), F32))
        out_specs.append(seq(t_ctx))
    pitch_rows = SUBLANES * (t_lat // SUBLANES + 4) + SUBLANES
    res = pl.pallas_call(
        kern,
        out_shape=tuple(out_shape),
        grid=(b, w // cwid),
        in_specs=in_specs,
        out_specs=tuple(out_specs),
        scratch_shapes=[
            pltpu.VMEM((LRU_TILES, t_lat + 2 * SUBLANES, LANES), F32),
            pltpu.VMEM((LRU_TILES, pitch_rows, LANES), F32),
            pltpu.VMEM((LRU_TILES, t_lat, LANES), F32),
            pltpu.VMEM((2 * LRU_TILES, t_lat, LANES), F32),
            pltpu.VMEM((2 * LRU_TILES, t_lat, LANES), F32),
            pltpu.VMEM((LRU_TILES, t_lat, LANES), F32),
        ],
        compiler_params=_cparams("arbitrary", "arbitrary"),
        name="rglru",
    )(*ins)
    return (res[0], res[1]) if need_ctx else (res[0], None)


def _ssd_kernel(*args, t_lat, t_ctx, need_ctx):
    if need_ctx:
        (xl_hbm, xc_hbm, cw_ref, cb_ref, dtb_ref, alog_ref, dsk_ref, yl_hbm, yc_ref,
         xg, ybuf, sem_in, sem_out, stg, xm, scc, sbt, cumb, srct, etb, sbuf, hst, yacc) = args
    else:
        (xl_hbm, xc_hbm, cw_ref, cb_ref, dtb_ref, alog_ref, dsk_ref, yl_hbm,
         xg, ybuf, sem_in, sem_out, stg, xm, scc, sbt, cumb, srct, etb, sbuf, hst, yacc) = args
        yc_ref = None
    ck = SSD_CHUNK
    rows = t_lat // GRID_W
    nc_ctx = t_ctx // ck
    nc_lat = t_lat // ck
    nx = SSD_HEADS * SSD_HEAD_DIM // LANES
    ng = SSD_GROUPS
    nslab = nx + 2 * ng
    halo = SUBLANES

    ri = lax.broadcasted_iota(jnp.int32, (ck, ck), 0)
    ci = lax.broadcasted_iota(jnp.int32, (ck, ck), 1)
    lane_lo = ci < SSD_HEAD_DIM

    b = pl.program_id(0)
    nb = pl.num_programs(0)
    slot = lax.rem(b, 2)
    ctx0 = halo
    lat0 = ctx0 + t_ctx + halo
    width = xg.shape[-1]

    def in_copies(bi, sl):
        cps = [pltpu.make_async_copy(xc_hbm.at[bi], xg.at[sl, pl.ds(ctx0, t_ctx), :], sem_in.at[sl])]
        for w in range(GRID_W):
            cps.append(pltpu.make_async_copy(
                xl_hbm.at[bi, :, w, :], xg.at[sl, pl.ds(lat0 + w * rows, rows), :], sem_in.at[sl]))
        return cps

    def out_copies(bi):
        return [pltpu.make_async_copy(ybuf.at[pl.ds(w * rows, rows), :], yl_hbm.at[bi, :, w, :], sem_out.at[0])
                for w in range(GRID_W)]

    @pl.when(b == 0)
    def _():
        zpad = jnp.zeros((halo, width), F32)
        for sl in range(2):
            for r in (0, ctx0 + t_ctx, lat0 + t_lat):
                xg[sl, r:r + halo, :] = zpad
        for cp in in_copies(0, 0):
            cp.start()

    @pl.when(b + 1 < nb)
    def _():
        for cp in in_copies(b + 1, 1 - slot):
            cp.start()

    for cp in in_copies(b, slot):
        cp.wait()

    xs = xg.at[slot]

    def src_row(k):
        return pl.multiple_of(jnp.where(k < nc_ctx, ctx0 + k * ck, lat0 + (k - nc_ctx) * ck), SUBLANES)

    def conv_silu(src0, j):
        stg[j] = xs[pl.ds(src0 - halo, ck + 2 * halo), j * LANES:(j + 1) * LANES]
        cw = cw_ref[j]
        half = cb_ref[j]
        for tap in range(CONV_K):
            half = half + cw[tap:tap + 1, :] * stg[j, halo + tap - 1:halo + tap - 1 + ck, :]
        return half * jnp.tanh(half) + half

    dtb = dtb_ref[...]
    lane_ok = lax.broadcasted_iota(jnp.int32, (1, LANES), 1) < 2 * SSD_HEADS
    nega2 = jnp.where(lane_ok, -jnp.exp(alog_ref[...]) * math.log2(math.e), 0.0)

    nck = nc_ctx + nc_lat
    nh2 = 2 * SSD_HEADS
    hpp = SSD_HEADS // ng // 2
    dir1_col = lax.broadcasted_iota(jnp.int32, (1, LANES), 1) >= SSD_HEADS
    tris = (ri >= ci, ri <= ci)

    def xcat_of(j, r0):
        return jnp.concatenate([xm[j, pl.ds(r0, ck), :], xm[nx + j, pl.ds(r0, ck), :]], axis=0)

    def pass1(kc, carry):
        r0 = pl.multiple_of(kc * ck, ck)
        src0 = src_row(kc)
        dt = _softplus(xs[pl.ds(src0, ck), nslab * LANES:(nslab + 1) * LANES] + dtb)
        la = dt * nega2
        cumf = la
        for sh in (1, 2, 4):
            cumf = cumf + jnp.where(ri >= sh, pltpu.roll(cumf, sh, 0), 0.0)
        for sh in (8, 16, 32, 64):
            cumf = cumf + jnp.concatenate([jnp.zeros((sh, LANES), F32), cumf[0:ck - sh, :]], axis=0)
        tot = cumf[ck - 1:ck, :]
        cum = jnp.where(dir1_col, tot - cumf + la, cumf)
        ldt = jnp.log2(dt)
        cumb[pl.ds(r0, ck), :] = cum
        srct[pl.ds(pl.multiple_of(kc * nh2, nh2), nh2), :] = (cum - ldt).T[0:nh2, :]
        wgt_t = jnp.exp2((tot - cum + ldt).T[0:nh2, :])
        etb[pl.ds(pl.multiple_of(kc * SUBLANES, SUBLANES), SUBLANES), :] = jnp.broadcast_to(
            jnp.exp2(tot), (SUBLANES, LANES))
        for g in range(ng):
            scc[g, pl.ds(r0, ck), :] = conv_silu(src0, nx + ng + g).astype(BF16)
            btf = conv_silu(src0, nx + g).T
            sbt[g, pl.ds(r0, ck), :] = btf.astype(BF16)
            for pr in range(hpp):
                j = g * hpp + pr
                x = conv_silu(src0, j)
                yacc[j, pl.ds(r0, ck), :] = dsk_ref[j] * x
                xb = x.astype(BF16)
                zb = jnp.zeros_like(xb)
                xlo = jnp.where(lane_lo, xb, zb)
                xhi = jnp.where(lane_lo, zb, xb)
                xm[j, pl.ds(r0, ck), :] = xlo
                xm[nx + j, pl.ds(r0, ck), :] = xhi
                btw = []
                for d in range(2):
                    hc = (d * SSD_HEADS + 2 * j, d * SSD_HEADS + 2 * j + 1)
                    btw.append(jnp.concatenate([(btf * wgt_t[h:h + 1, :]).astype(BF16) for h in hc], axis=1))
                s = _dot(jnp.concatenate(btw, axis=0), jnp.concatenate([xlo, xhi], axis=0)).astype(BF16)
                sbuf[j, pl.ds(r0, ck), :] = s[0:SSD_STATE, :]
                sbuf[nx + j, pl.ds(r0, ck), :] = s[SSD_STATE:2 * SSD_STATE, :]
        return carry

    lax.fori_loop(0, nck, pass1, 0, unroll=2)

    hst[...] = jnp.zeros(hst.shape, F32)

    def pass2(i, carry):
        for d in range(2):
            if d == 0:
                kc = i
            else:
                kc = jnp.where(i < nc_ctx, nc_ctx - 1 - i, nck - 1 - (i - nc_ctx))
            r0 = pl.multiple_of(kc * ck, ck)
            e = etb[pl.ds(pl.multiple_of(kc * SUBLANES, SUBLANES), 1), :]
            for j in range(nx):
                hc = (d * SSD_HEADS + 2 * j, d * SSD_HEADS + 2 * j + 1)
                et = jnp.where(lane_lo[0:1, :], jnp.broadcast_to(e[:, hc[0]:hc[0] + 1], (1, LANES)),
                               jnp.broadcast_to(e[:, hc[1]:hc[1] + 1], (1, LANES)))
                h = hst[d * nx + j]
                s = sbuf[d * nx + j, pl.ds(r0, ck), :].astype(F32)
                sbuf[d * nx + j, pl.ds(r0, ck), :] = h.astype(BF16)
                hst[d * nx + j] = h * et + s
        return carry

    lax.fori_loop(0, nck, pass2, 0)

    def pass3(kc, is_ctx):
        r0 = kc * ck if isinstance(kc, int) else pl.multiple_of(kc * ck, ck)
        s0 = kc * nh2 if isinstance(kc, int) else pl.multiple_of(kc * nh2, nh2)
        cum = cumb[pl.ds(r0, ck), :]
        src_t = srct[pl.ds(s0, nh2), :]
        for g in range(ng):
            cm = scc[g, pl.ds(r0, ck), :]
            cb = _dot(cm, sbt[g, pl.ds(r0, ck), :])
            for pr in range(hpp):
                j = g * hpp + pr
                dec = []
                ecs = []
                for d in range(2):
                    hc = (d * SSD_HEADS + 2 * j, d * SSD_HEADS + 2 * j + 1)
                    col = [jnp.broadcast_to(cum[:, h:h + 1], (ck, LANES)) for h in hc]
                    dec.append([jnp.exp2(jnp.where(tris[d], col[q] - src_t[hc[q]:hc[q] + 1, :], NEG_BIG))
                                for q in range(2)])
                    ecs.append(jnp.exp2(jnp.where(lane_lo, col[0], col[1])))
                ms = [(cb * (dec[0][q] + dec[1][q])).astype(BF16) for q in range(2)]
                y = yacc[j, pl.ds(r0, ck), :] + _dot(jnp.concatenate(ms, axis=1), xcat_of(j, r0))
                hin = jnp.concatenate([sbuf[j, pl.ds(r0, ck), :], sbuf[nx + j, pl.ds(r0, ck), :]], axis=1)
                yoff = _dot(cm, hin)
                y = y + yoff[:, 0:LANES] * ecs[0] + yoff[:, LANES:2 * LANES] * ecs[1]
                if is_ctx:
                    yc_ref[0, pl.ds(r0, ck), j * LANES:(j + 1) * LANES] = y
                else:
                    y0 = pl.multiple_of((kc - nc_ctx) * ck, ck)
                    ybuf[pl.ds(y0, ck), j * LANES:(j + 1) * LANES] = y

    if need_ctx:
        for kc in range(nc_ctx):
            pass3(kc, True)

    @pl.when(b > 0)
    def _():
        for cp in out_copies(b - 1):
            cp.wait()

    def lat_body(c, carry):
        pass3(nc_ctx + c, False)
        return carry

    lax.fori_loop(0, nc_lat, lat_body, 0, unroll=2)

    for cp in out_copies(b):
        cp.start()

    @pl.when(b == nb - 1)
    def _():
        for cp in out_copies(b):
            cp.wait()


def _ssd(xd_l, xd_c, cw, cb, dtb, alog, dsk, *, need_ctx):
    b, t_lat, width = xd_l.shape
    t_ctx = xd_c.shape[1]
    nx = SSD_HEADS * SSD_HEAD_DIM // LANES
    ng = SSD_GROUPS
    inner = nx * LANES
    tot = t_lat + t_ctx
    rows = t_lat // GRID_W
    kern = functools.partial(_ssd_kernel, t_lat=t_lat, t_ctx=t_ctx, need_ctx=need_ctx)
    full = lambda a: pl.BlockSpec(a.shape, lambda i: (0,) * a.ndim)
    hbm = pl.BlockSpec(memory_space=pl.ANY)
    out_shape = [jax.ShapeDtypeStruct((b, rows, GRID_W, inner), F32)]
    out_specs = [hbm]
    if need_ctx:
        out_shape.append(jax.ShapeDtypeStruct((b, t_ctx, inner), F32))
        out_specs.append(pl.BlockSpec((1, t_ctx, inner), lambda i: (i, 0, 0)))
    res = pl.pallas_call(
        kern,
        out_shape=tuple(out_shape),
        grid=(b,),
        in_specs=[hbm, hbm, full(cw), full(cb), full(dtb), full(alog), full(dsk)],
        out_specs=tuple(out_specs),
        scratch_shapes=[
            pltpu.VMEM((2, tot + 3 * SUBLANES, width), F32),
            pltpu.VMEM((t_lat, inner), F32),
            pltpu.SemaphoreType.DMA((2,)),
            pltpu.SemaphoreType.DMA((1,)),
            pltpu.VMEM((nx + 2 * ng, SSD_CHUNK + 2 * SUBLANES, LANES), F32),
            pltpu.VMEM((2 * nx, tot, LANES), BF16),
            pltpu.VMEM((ng, tot, LANES), BF16),
            pltpu.VMEM((ng, tot, LANES), BF16),
            pltpu.VMEM((tot, LANES), F32),
            pltpu.VMEM((tot // SSD_CHUNK * 2 * SSD_HEADS, LANES), F32),
            pltpu.VMEM((tot // SSD_CHUNK * SUBLANES, LANES), F32),
            pltpu.VMEM((2 * nx, tot, LANES), BF16),
            pltpu.VMEM((2 * nx, SSD_STATE, LANES), F32),
            pltpu.VMEM((nx, tot, LANES), F32),
        ],
        compiler_params=_cparams("arbitrary"),
        name="ssd",
    )(xd_l.reshape(b, rows, GRID_W, width), xd_c, cw, cb, dtb, alog, dsk)
    y_l = res[0].reshape(b, t_lat, inner)
    return (y_l, res[1]) if need_ctx else (y_l, None)


def _outmlp_kernel(*args, final_norm, ff_chunk):
    if final_norm:
        (x_ref, lru_ref, lg_ref, ssd_ref, z_ref, mod_ref, sg_ref, n2_ref, wo_ref, w1_ref, w2_ref, fg_ref,
         o_ref) = args
    else:
        (x_ref, lru_ref, lg_ref, ssd_ref, z_ref, mod_ref, sg_ref, n2_ref, wo_ref, w1_ref, w2_ref, o_ref) = args
    x = x_ref[0]
    z = z_ref[0]
    lw = z.shape[-1]
    gw = lw // SSD_GROUPS
    parts = []
    for g in range(SSD_GROUPS):
        y = ssd_ref[0, :, g * gw:(g + 1) * gw] * _silu(z[:, g * gw:(g + 1) * gw])
        y = y * lax.rsqrt(jnp.mean(y * y, axis=-1, keepdims=True) + EPS)
        parts.append(y * sg_ref[:, g * gw:(g + 1) * gw])
    ssd_n = jnp.concatenate(parts, axis=1).astype(BF16)
    lru = (lru_ref[0] * _gelu_tanh(lg_ref[0])).astype(BF16)
    y = _dot(lru, wo_ref[0:lw, :]) + _dot(ssd_n, wo_ref[lw:2 * lw, :])
    x1 = x + mod_ref[0, 2:3, :] * y
    h = (_rmsnorm(x1, n2_ref[...]) * (1.0 + mod_ref[0, 4:5, :]) + mod_ref[0, 3:4, :]).astype(BF16)
    dff = w1_ref.shape[1]
    acc = jnp.zeros_like(x1)
    for j in range(dff // ff_chunk):
        a = _dot(h, w1_ref[:, j * ff_chunk:(j + 1) * ff_chunk])
        a = jnp.square(jnp.maximum(a, 0.0)).astype(BF16)
        acc = acc + _dot(a, w2_ref[j * ff_chunk:(j + 1) * ff_chunk, :])
    x2 = x1 + mod_ref[0, 5:6, :] * acc
    if final_norm:
        x2 = _rmsnorm(x2, fg_ref[...])
    o_ref[0] = x2


def _outmlp(x, lru, lg, ssd, z, mod, mod_row, sg, n2, wo, w1, w2, layer, fg, *, tm, ff_chunk=1024):
    b, t, d = x.shape
    lw = lru.shape[-1]
    final_norm = fg is not None
    kern = functools.partial(_outmlp_kernel, final_norm=final_norm, ff_chunk=ff_chunk)
    tok = lambda n: pl.BlockSpec((1, tm, n), lambda i, j: (i, j, 0))
    res = lambda a: pl.BlockSpec((None,) + a.shape[1:], lambda i, j: (layer, 0, 0), pipeline_mode=pl.Buffered(1))
    row = lambda n: pl.BlockSpec((1, n), lambda i, j: (0, 0))
    ins = [x, lru, lg, ssd, z, mod, sg, n2, wo, w1, w2]
    in_specs = [tok(d), tok(lw), tok(lw), tok(lw), tok(lw),
                pl.BlockSpec((1, N_MOD, d), lambda i, j: (mod_row(i), 0, 0)),
                row(lw), row(d), res(wo), res(w1), res(w2)]
    if final_norm:
        ins.append(fg)
        in_specs.append(row(d))
    return pl.pallas_call(
        kern,
        out_shape=jax.ShapeDtypeStruct((b, t, d), F32),
        grid=(b, t // tm),
        in_specs=in_specs,
        out_specs=tok(d),
        compiler_params=_cparams("arbitrary", "arbitrary"),
        name="outmlp",
    )(*ins)


def _pair_block_diag(w):
    two, h, k, _ = w.shape
    w = w.reshape(two, h // 2, 2, k, k)
    z = jnp.zeros_like(w[:, :, 0])
    top = jnp.concatenate([w[:, :, 0], z], axis=-1)
    bot = jnp.concatenate([z, w[:, :, 1]], axis=-1)
    return jnp.concatenate([top, bot], axis=-2)


def _pad_lanes(a):
    return jnp.pad(a, [(0, 0)] * (a.ndim - 1) + [(0, LANES - a.shape[-1])])


def kernel(x, c, ctx, c_ctx, ada_w, ada_b, norm1_g, norm2_g, w_in, lru_conv_w, lru_conv_b, lru_wa, lru_ba,
           lru_wx, lru_bx, lru_lambda, ssd_conv_w, ssd_conv_b, ssd_dt_bias, ssd_a_log, ssd_d, ssd_norm_g,
           w_out, mlp_w1, mlp_w2, final_g):
    bsz, t_lat, d = x.shape
    t_ctx = ctx.shape[1]
    depth = ada_w.shape[0]
    lw = lru_conv_w.shape[-1]
    nxbc = ssd_conv_w.shape[-1]
    ndt = 2 * SSD_HEADS
    nscan = lw + nxbc + ndt
    assert bsz + 1 <= MOD_ROWS and t_lat % (GRID_W * SUBLANES) == 0 and t_ctx % SSD_CHUNK == 0

    cvec = jnp.zeros((MOD_ROWS, d), F32).at[:bsz].set(c).at[bsz].set(c_ctx)
    mod = _modulation(cvec, ada_w, ada_b).reshape(depth, MOD_ROWS, N_MOD, d)

    lat_row = lambda i: i
    ctx_row = lambda i: bsz
    tm_lat = 512
    tm_ctx = t_ctx
    nslab = nxbc // LANES
    nxd = nxbc + LANES

    w_re = jnp.concatenate(
        [w_in[:, :, :lw + nxbc], _pad_lanes(w_in[:, :, lw + nxbc:nscan]), w_in[:, :, nscan:]], axis=2).astype(BF16)
    wo = w_out.astype(BF16)
    w1 = mlp_w1.astype(BF16)
    w2 = mlp_w2.astype(BF16)

    for l in range(depth):
        need_ctx = l < depth - 1
        g1 = norm1_g[l].reshape(1, d)
        lx_l, xd_l, lg_l, z_l = _inproj(x, mod[l], lat_row, g1, w_re, l, lw=lw, nxd=nxd, tm=tm_lat)
        lx_c, xd_c, lg_c, z_c = _inproj(ctx, mod[l], ctx_row, g1, w_re, l, lw=lw, nxd=nxd, tm=tm_ctx)

        lru_l, lru_c = _lru(
            lx_l, lx_c, lru_conv_w[l], lru_conv_b[l].reshape(1, lw),
            _pair_block_diag(0.5 * lru_wa[l]).astype(BF16), _pair_block_diag(0.5 * lru_wx[l]).astype(BF16),
            0.5 * lru_ba[l], 0.5 * lru_bx[l], lru_lambda[l], need_ctx=need_ctx)

        cw = 0.5 * ssd_conv_w[l].reshape(CONV_K, nslab, LANES).transpose(1, 0, 2)
        cb = 0.5 * ssd_conv_b[l].reshape(nslab, 1, LANES)
        dtb = _pad_lanes(ssd_dt_bias[l].reshape(1, ndt))
        alog = _pad_lanes(ssd_a_log[l].reshape(1, ndt))
        dsk = jnp.repeat(ssd_d[l], SSD_HEAD_DIM).reshape(lw // LANES, 1, LANES)
        ssd_l, ssd_c = _ssd(xd_l, xd_c, cw, cb, dtb, alog, dsk, need_ctx=need_ctx)

        sg = ssd_norm_g[l].reshape(1, lw)
        n2 = norm2_g[l].reshape(1, d)
        fg = final_g.reshape(1, d) if l == depth - 1 else None
        x = _outmlp(x, lru_l, lg_l, ssd_l, z_l, mod[l], lat_row, sg, n2, wo, w1, w2, l, fg, tm=tm_lat)
        if need_ctx:
            ctx = _outmlp(ctx, lru_c, lg_c, ssd_c, z_c, mod[l], ctx_row, sg, n2, wo, w1, w2, l, None,
                          tm=tm_ctx)
    return x
```

```python
import functools
import math

import jax
import jax.numpy as jnp
from jax import lax
from jax.experimental import pallas as pl
from jax.experimental.pallas import tpu as pltpu

F32 = jnp.float32
BF16 = jnp.bfloat16

EPS = 1e-6
GRID_W = 64
LRU_BLOCK = 64
LRU_C = 8.0
SSD_HEAD_DIM = 64
SSD_HEADS = 8
SSD_GROUPS = 2
SSD_STATE = 128
SSD_CHUNK = 128
CONV_K = 4
N_MOD = 6
LANES = 128
SUBLANES = 8
MOD_ROWS = 16
VMEM_LIMIT = 56 * 1024 * 1024
NEG_BIG = -1e30


def _sigmoid(x):
    return 0.5 * (jnp.tanh(0.5 * x) + 1.0)


def _silu(x):
    return x * _sigmoid(x)


def _softplus(x):
    return jnp.maximum(x, 0.0) + jnp.log1p(jnp.exp(-jnp.abs(x)))


def _gelu_tanh(x):
    c = math.sqrt(2.0 / math.pi)
    return (0.5 * x) * (1.0 + jnp.tanh(x * (c + (c * 0.044715) * (x * x))))


def _rmsnorm(x, g):
    return x * lax.rsqrt(jnp.mean(x * x, axis=-1, keepdims=True) + EPS) * g


def _dot(a, b):
    return jnp.dot(a, b, preferred_element_type=F32)


def _cparams(*sem):
    return pltpu.CompilerParams(dimension_semantics=sem, vmem_limit_bytes=VMEM_LIMIT)


def _mod_kernel(c_ref, w_ref, b_ref, o_ref):
    s = _silu(c_ref[...]).astype(BF16)
    o_ref[0] = _dot(s, w_ref[0].astype(BF16)) + b_ref[0]


def _modulation(cvec, ada_w, ada_b):
    depth, d, n = ada_w.shape
    tn = 1536
    return pl.pallas_call(
        _mod_kernel,
        out_shape=jax.ShapeDtypeStruct((depth, MOD_ROWS, n), F32),
        grid=(depth, n // tn),
        in_specs=[
            pl.BlockSpec((MOD_ROWS, d), lambda l, j: (0, 0)),
            pl.BlockSpec((1, d, tn), lambda l, j: (l, 0, j)),
            pl.BlockSpec((1, 1, tn), lambda l, j: (l, 0, j)),
        ],
        out_specs=pl.BlockSpec((1, MOD_ROWS, tn), lambda l, j: (l, 0, j)),
        compiler_params=_cparams("arbitrary", "arbitrary"),
        name="modulation",
    )(cvec, ada_w, ada_b.reshape(depth, 1, n))


def _inproj_kernel(x_ref, mod_ref, g_ref, w_ref, lx_ref, xd_ref, lg_ref, z_ref, *, lw, nxd):
    x = x_ref[0]
    h = _rmsnorm(x, g_ref[...]) * (1.0 + mod_ref[0, 1:2, :]) + mod_ref[0, 0:1, :]
    h = h.astype(BF16)
    lx_ref[0] = _dot(h, w_ref[:, 0:lw])
    xd_ref[0] = _dot(h, w_ref[:, lw:lw + nxd])
    o = lw + nxd
    lg_ref[0] = _dot(h, w_ref[:, o:o + lw])
    z_ref[0] = _dot(h, w_ref[:, o + lw:o + 2 * lw])


def _inproj(x, mod, mod_row, g, w, layer, *, lw, nxd, tm):
    b, t, d = x.shape
    kern = functools.partial(_inproj_kernel, lw=lw, nxd=nxd)
    tok = lambda n: pl.BlockSpec((1, tm, n), lambda i, j: (i, j, 0))
    return pl.pallas_call(
        kern,
        out_shape=(
            jax.ShapeDtypeStruct((b, t, lw), F32),
            jax.ShapeDtypeStruct((b, t, nxd), F32),
            jax.ShapeDtypeStruct((b, t, lw), F32),
            jax.ShapeDtypeStruct((b, t, lw), F32),
        ),
        grid=(b, t // tm),
        in_specs=[
            tok(d),
            pl.BlockSpec((1, N_MOD, d), lambda i, j: (mod_row(i), 0, 0)),
            pl.BlockSpec((1, d), lambda i, j: (0, 0)),
            pl.BlockSpec((None,) + w.shape[1:], lambda i, j: (layer, 0, 0), pipeline_mode=pl.Buffered(1)),
        ],
        out_specs=(tok(lw), tok(nxd), tok(lw), tok(lw)),
        compiler_params=_cparams("arbitrary", "arbitrary"),
        name="inproj",
    )(x, mod, g, w)


LRU_ROWS = 256
LRU_TILES = 2


def _lru_seq(t, u_ref, out_ref, h0, refs, scr, need_out):
    cw_ref, cb_ref, wa_ref, wx_ref, ba_ref, bx_ref, lam_ref = refs
    upad, ubuf, uint, hbuf, pbuf, ybuf = scr
    seg = t // SUBLANES
    pitch = seg + 4
    nblk = max(t // LRU_ROWS, 1)
    rb = t // nblk
    nv = rb // SUBLANES
    tiles = [slice(c * LANES, (c + 1) * LANES) for c in range(LRU_TILES)]

    zeros8 = jnp.zeros((SUBLANES, LANES), F32)
    ones8 = jnp.ones((SUBLANES, LANES), F32)
    for c, ln in enumerate(tiles):
        upad[c, 0:SUBLANES, :] = zeros8
        upad[c, SUBLANES:SUBLANES + t, :] = u_ref[:, ln]
        upad[c, SUBLANES + t:2 * SUBLANES + t, :] = zeros8
        cw = cw_ref[:, ln]
        for s in range(SUBLANES):
            acc = cb_ref[:, ln]
            for tap in range(CONV_K):
                r = SUBLANES + s * seg + tap - 1
                acc = acc + cw[tap:tap + 1, :] * upad[c, r:r + seg, :]
            ubuf[c, s * pitch:s * pitch + seg, :] = acc
        for tau in range(seg):
            uint[c, tau * SUBLANES:(tau + 1) * SUBLANES, :] = ubuf[c, pl.ds(tau, SUBLANES, stride=pitch), :]

    row = lax.broadcasted_iota(jnp.int32, (SUBLANES, LANES), 0)
    kdec = [[(-0.5 * LRU_C) * _softplus(-lam_ref[d:d + 1, ln]) for d in range(2)] for ln in tiles]
    chains = [(c, d) for c in range(LRU_TILES) for d in range(2)]

    def blk_body(i, carry):
        carry = list(carry)
        for q, (c, d) in enumerate(chains):
            ln = tiles[c]
            h, p = carry[2 * q], carry[2 * q + 1]
            blk = i if d == 0 else nblk - 1 - i
            r0 = pl.multiple_of(blk * rb, rb)
            u = uint[c, pl.ds(r0, rb), :]
            ub = u.astype(BF16)
            log_a = kdec[c][d] * jnp.tanh(_dot(ub, wa_ref[d, c]) + ba_ref[d:d + 1, ln]) + kdec[c][d]
            gate = jnp.tanh(_dot(ub, wx_ref[d, c]) + bx_ref[d:d + 1, ln]) + 1.0
            a = jnp.exp(log_a)
            v = jnp.tanh(-log_a) * (1.0 + a * a)
            root = jnp.where(v > 0.0, v * lax.rsqrt(v), 0.0)
            bb = root * (gate * (0.5 * u))
            hs = [None] * nv
            ps = [None] * nv
            order = range(nv) if d == 0 else range(nv - 1, -1, -1)
            for k in order:
                ak = a[k * SUBLANES:(k + 1) * SUBLANES, :]
                h = ak * h + bb[k * SUBLANES:(k + 1) * SUBLANES, :]
                p = ak * p
                hs[k] = h
                ps[k] = p
            hbuf[q, pl.ds(r0, rb), :] = jnp.concatenate(hs, axis=0)
            pbuf[q, pl.ds(r0, rb), :] = jnp.concatenate(ps, axis=0)
            carry[2 * q], carry[2 * q + 1] = h, p
        return tuple(carry)

    hp = lax.fori_loop(0, nblk, blk_body, (zeros8, ones8) * len(chains))
    finals = []
    cmats = []
    for q, (c, d) in enumerate(chains):
        h, p = hp[2 * q], hp[2 * q + 1]
        st = h0[q]
        cmat = zeros8
        order = range(SUBLANES) if d == 0 else range(SUBLANES - 1, -1, -1)
        for s in order:
            cmat = jnp.where(row == s, jnp.broadcast_to(st, (SUBLANES, LANES)), cmat)
            st = h[s:s + 1, :] + p[s:s + 1, :] * st
        finals.append(st)
        cmats.append(cmat)

    if need_out:
        def out_body(i, carry):
            r0 = pl.multiple_of(i * rb, rb)
            for c in range(LRU_TILES):
                y = None
                for q in (2 * c, 2 * c + 1):
                    t1 = hbuf[q, pl.ds(r0, rb), :].reshape(nv, SUBLANES, LANES)
                    t2 = pbuf[q, pl.ds(r0, rb), :].reshape(nv, SUBLANES, LANES) * cmats[q][None]
                    y = t1 + t2 if y is None else y + t1 + t2
                ybuf[c, pl.ds(r0, rb), :] = y.reshape(rb, LANES)
            return carry

        lax.fori_loop(0, nblk, out_body, 0)
        for c, ln in enumerate(tiles):
            for tau in range(seg):
                ubuf[c, pl.ds(tau, SUBLANES, stride=pitch), :] = ybuf[c, tau * SUBLANES:(tau + 1) * SUBLANES, :]
            for s in range(SUBLANES):
                out_ref[s * seg:(s + 1) * seg, ln] = ubuf[c, s * pitch:s * pitch + seg, :]
    return finals


def _lru_kernel(*args, t_lat, t_ctx, need_ctx):
    if need_ctx:
        (ul_ref, uc_ref, *refs, yl_ref, yc_ref, upad, ubuf, uint, hbuf, pbuf, ybuf) = args
    else:
        (ul_ref, uc_ref, *refs, yl_ref, upad, ubuf, uint, hbuf, pbuf, ybuf) = args
        yc_ref = None
    scr = (upad, ubuf, uint, hbuf, pbuf, ybuf)
    zero = jnp.zeros((1, LANES), F32)
    ctx_final = _lru_seq(t_ctx, uc_ref.at[0], yc_ref.at[0] if need_ctx else None,
                         (zero,) * (2 * LRU_TILES), refs, scr, need_ctx)
    _lru_seq(t_lat, ul_ref.at[0], yl_ref.at[0], ctx_final, refs, scr, True)


def _lru(lx_l, lx_c, conv_w, conv_b, wa_bd, wx_bd, ba, bx, lam, *, need_ctx):
    b, t_lat, w = lx_l.shape
    t_ctx = lx_c.shape[1]
    cwid = LRU_TILES * LANES
    seq = lambda t: pl.BlockSpec((1, t, cwid), lambda i, j: (i, 0, j))
    par = lambda r: pl.BlockSpec((r, cwid), lambda i, j: (0, j))
    gate = pl.BlockSpec((2, LRU_TILES, LANES, LANES), lambda i, j: (0, j, 0, 0))
    kern = functools.partial(_lru_kernel, t_lat=t_lat, t_ctx=t_ctx, need_ctx=need_ctx)
    ins = [lx_l, lx_c, conv_w, conv_b, wa_bd, wx_bd, ba, bx, lam]
    in_specs = [seq(t_lat), seq(t_ctx), par(CONV_K), par(1), gate, gate, par(2), par(2), par(2)]
    out_shape = [jax.ShapeDtypeStruct((b, t_lat, w), F32)]
    out_specs = [seq(t_lat)]
    if need_ctx:
        out_shape.append(jax.ShapeDtypeStruct((b, t_ctx, w), F32))
        out_specs.append(seq(t_ctx))
    pitch_rows = SUBLANES * (t_lat // SUBLANES + 4) + SUBLANES
    res = pl.pallas_call(
        kern,
        out_shape=tuple(out_shape),
        grid=(b, w // cwid),
        in_specs=in_specs,
        out_specs=tuple(out_specs),
        scratch_shapes=[
            pltpu.VMEM((LRU_TILES, t_lat + 2 * SUBLANES, LANES), F32),
            pltpu.VMEM((LRU_TILES, pitch_rows, LANES), F32),
            pltpu.VMEM((LRU_TILES, t_lat, LANES), F32),
            pltpu.VMEM((2 * LRU_TILES, t_lat, LANES), F32),
            pltpu.VMEM((2 * LRU_TILES, t_lat, LANES), F32),
            pltpu.VMEM((LRU_TILES, t_lat, LANES), F32),
        ],
        compiler_params=_cparams("arbitrary", "arbitrary"),
        name="rglru",
    )(*ins)
    return (res[0], res[1]) if need_ctx else (res[0], None)


def _ssd_kernel(*args, t_lat, t_ctx, need_ctx):
    if need_ctx:
        (xl_hbm, xc_hbm, cw_ref, cb_ref, dtb_ref, alog_ref, dsk_ref, yl_hbm, yc_ref,
         xg, ybuf, sem_in, sem_out, stg, xm, scc, sbt, cumb, srct, etb, sbuf, hst, yacc) = args
    else:
        (xl_hbm, xc_hbm, cw_ref, cb_ref, dtb_ref, alog_ref, dsk_ref, yl_hbm,
         xg, ybuf, sem_in, sem_out, stg, xm, scc, sbt, cumb, srct, etb, sbuf, hst, yacc) = args
        yc_ref = None
    ck = SSD_CHUNK
    rows = t_lat // GRID_W
    nc_ctx = t_ctx // ck
    nc_lat = t_lat // ck
    nx = SSD_HEADS * SSD_HEAD_DIM // LANES
    ng = SSD_GROUPS
    nslab = nx + 2 * ng
    halo = SUBLANES

    ri = lax.broadcasted_iota(jnp.int32, (ck, ck), 0)
    ci = lax.broadcasted_iota(jnp.int32, (ck, ck), 1)
    lane_lo = ci < SSD_HEAD_DIM

    b = pl.program_id(0)
    nb = pl.num_programs(0)
    slot = lax.rem(b, 2)
    ctx0 = halo
    lat0 = ctx0 + t_ctx + halo
    width = xg.shape[-1]

    def in_copies(bi, sl):
        cps = [pltpu.make_async_copy(xc_hbm.at[bi], xg.at[sl, pl.ds(ctx0, t_ctx), :], sem_in.at[sl])]
        for w in range(GRID_W):
            cps.append(pltpu.make_async_copy(
                xl_hbm.at[bi, :, w, :], xg.at[sl, pl.ds(lat0 + w * rows, rows), :], sem_in.at[sl]))
        return cps

    def out_copies(bi):
        return [pltpu.make_async_copy(ybuf.at[pl.ds(w * rows, rows), :], yl_hbm.at[bi, :, w, :], sem_out.at[0])
                for w in range(GRID_W)]

    @pl.when(b == 0)
    def _():
        zpad = jnp.zeros((halo, width), F32)
        for sl in range(2):
            for r in (0, ctx0 + t_ctx, lat0 + t_lat):
                xg[sl, r:r + halo, :] = zpad
        for cp in in_copies(0, 0):
            cp.start()

    @pl.when(b + 1 < nb)
    def _():
        for cp in in_copies(b + 1, 1 - slot):
            cp.start()

    for cp in in_copies(b, slot):
        cp.wait()

    xs = xg.at[slot]

    def src_row(k):
        return pl.multiple_of(jnp.where(k < nc_ctx, ctx0 + k * ck, lat0 + (k - nc_ctx) * ck), SUBLANES)

    def conv_silu(src0, j):
        stg[j] = xs[pl.ds(src0 - halo, ck + 2 * halo), j * LANES:(j + 1) * LANES]
        cw = cw_ref[j]
        half = cb_ref[j]
        for tap in range(CONV_K):
            half = half + cw[tap:tap + 1, :] * stg[j, halo + tap - 1:halo + tap - 1 + ck, :]
        return half * jnp.tanh(half) + half

    dtb = dtb_ref[...]
    lane_ok = lax.broadcasted_iota(jnp.int32, (1, LANES), 1) < 2 * SSD_HEADS
    nega2 = jnp.where(lane_ok, -jnp.exp(alog_ref[...]) * math.log2(math.e), 0.0)

    nck = nc_ctx + nc_lat
    nh2 = 2 * SSD_HEADS
    hpp = SSD_HEADS // ng // 2
    dir1_col = lax.broadcasted_iota(jnp.int32, (1, LANES), 1) >= SSD_HEADS
    tris = (ri >= ci, ri <= ci)

    def xcat_of(j, r0):
        return jnp.concatenate([xm[j, pl.ds(r0, ck), :], xm[nx + j, pl.ds(r0, ck), :]], axis=0)

    def pass1(kc, carry):
        r0 = pl.multiple_of(kc * ck, ck)
        src0 = src_row(kc)
        dt = _softplus(xs[pl.ds(src0, ck), nslab * LANES:(nslab + 1) * LANES] + dtb)
        la = dt * nega2
        cumf = la
        for sh in (1, 2, 4):
            cumf = cumf + jnp.where(ri >= sh, pltpu.roll(cumf, sh, 0), 0.0)
        for sh in (8, 16, 32, 64):
            cumf = cumf + jnp.concatenate([jnp.zeros((sh, LANES), F32), cumf[0:ck - sh, :]], axis=0)
        tot = cumf[ck - 1:ck, :]
        cum = jnp.where(dir1_col, tot - cumf + la, cumf)
        ldt = jnp.log2(dt)
        cumb[pl.ds(r0, ck), :] = cum
        srct[pl.ds(pl.multiple_of(kc * nh2, nh2), nh2), :] = (cum - ldt).T[0:nh2, :]
        wgt_t = jnp.exp2((tot - cum + ldt).T[0:nh2, :])
        etb[pl.ds(pl.multiple_of(kc * SUBLANES, SUBLANES), SUBLANES), :] = jnp.broadcast_to(
            jnp.exp2(tot), (SUBLANES, LANES))
        for g in range(ng):
            scc[g, pl.ds(r0, ck), :] = conv_silu(src0, nx + ng + g).astype(BF16)
            btf = conv_silu(src0, nx + g).T
            sbt[g, pl.ds(r0, ck), :] = btf.astype(BF16)
            for pr in range(hpp):
                j = g * hpp + pr
                x = conv_silu(src0, j)
                yacc[j, pl.ds(r0, ck), :] = dsk_ref[j] * x
                xb = x.astype(BF16)
                zb = jnp.zeros_like(xb)
                xlo = jnp.where(lane_lo, xb, zb)
                xhi = jnp.where(lane_lo, zb, xb)
                xm[j, pl.ds(r0, ck), :] = xlo
                xm[nx + j, pl.ds(r0, ck), :] = xhi
                btw = []
                for d in range(2):
                    hc = (d * SSD_HEADS + 2 * j, d * SSD_HEADS + 2 * j + 1)
                    btw.append(jnp.concatenate([(btf * wgt_t[h:h + 1, :]).astype(BF16) for h in hc], axis=1))
                s = _dot(jnp.concatenate(btw, axis=0), jnp.concatenate([xlo, xhi], axis=0)).astype(BF16)
                sbuf[j, pl.ds(r0, ck), :] = s[0:SSD_STATE, :]
                sbuf[nx + j, pl.ds(r0, ck), :] = s[SSD_STATE:2 * SSD_STATE, :]
        return carry

    lax.fori_loop(0, nck, pass1, 0, unroll=2)

    hst[...] = jnp.zeros(hst.shape, F32)

    def pass2(i, carry):
        for d in range(2):
            if d == 0:
                kc = i
            else:
                kc = jnp.where(i < nc_ctx, nc_ctx - 1 - i, nck - 1 - (i - nc_ctx))
            r0 = pl.multiple_of(kc * ck, ck)
            e = etb[pl.ds(pl.multiple_of(kc * SUBLANES, SUBLANES), 1), :]
            for j in range(nx):
                hc = (d * SSD_HEADS + 2 * j, d * SSD_HEADS + 2 * j + 1)
                et = jnp.where(lane_lo[0:1, :], jnp.broadcast_to(e[:, hc[0]:hc[0] + 1], (1, LANES)),
                               jnp.broadcast_to(e[:, hc[1]:hc[1] + 1], (1, LANES)))
                h = hst[d * nx + j]
                s = sbuf[d * nx + j, pl.ds(r0, ck), :].astype(F32)
                sbuf[d * nx + j, pl.ds(r0, ck), :] = h.astype(BF16)
                hst[d * nx + j] = h * et + s
        return carry

    lax.fori_loop(0, nck, pass2, 0)

    def pass3(kc, is_ctx):
        r0 = kc * ck if isinstance(kc, int) else pl.multiple_of(kc * ck, ck)
        s0 = kc * nh2 if isinstance(kc, int) else pl.multiple_of(kc * nh2, nh2)
        cum = cumb[pl.ds(r0, ck), :]
        src_t = srct[pl.ds(s0, nh2), :]
        for g in range(ng):
            cm = scc[g, pl.ds(r0, ck), :]
            cb = _dot(cm, sbt[g, pl.ds(r0, ck), :])
            for pr in range(hpp):
                j = g * hpp + pr
                dec = []
                ecs = []
                for d in range(2):
                    hc = (d * SSD_HEADS + 2 * j, d * SSD_HEADS + 2 * j + 1)
                    col = [jnp.broadcast_to(cum[:, h:h + 1], (ck, LANES)) for h in hc]
                    dec.append([jnp.exp2(jnp.where(tris[d], col[q] - src_t[hc[q]:hc[q] + 1, :], NEG_BIG))
                                for q in range(2)])
                    ecs.append(jnp.exp2(jnp.where(lane_lo, col[0], col[1])))
                ms = [(cb * (dec[0][q] + dec[1][q])).astype(BF16) for q in range(2)]
                y = yacc[j, pl.ds(r0, ck), :] + _dot(jnp.concatenate(ms, axis=1), xcat_of(j, r0))
                hin = jnp.concatenate([sbuf[j, pl.ds(r0, ck), :], sbuf[nx + j, pl.ds(r0, ck), :]], axis=1)
                yoff = _dot(cm, hin)
                y = y + yoff[:, 0:LANES] * ecs[0] + yoff[:, LANES:2 * LANES] * ecs[1]
                if is_ctx:
                    yc_ref[0, pl.ds(r0, ck), j * LANES:(j + 1) * LANES] = y
                else:
                    y0 = pl.multiple_of((kc - nc_ctx) * ck, ck)
                    ybuf[pl.ds(y0, ck), j * LANES:(j + 1) * LANES] = y

    if need_ctx:
        for kc in range(nc_ctx):
            pass3(kc, True)

    @pl.when(b > 0)
    def _():
        for cp in out_copies(b - 1):
            cp.wait()

    def lat_body(c, carry):
        pass3(nc_ctx + c, False)
        return carry

    lax.fori_loop(0, nc_lat, lat_body, 0, unroll=2)

    for cp in out_copies(b):
        cp.start()

    @pl.when(b == nb - 1)
    def _():
        for cp in out_copies(b):
            cp.wait()


def _ssd(xd_l, xd_c, cw, cb, dtb, alog, dsk, *, need_ctx):
    b, t_lat, width = xd_l.shape
    t_ctx = xd_c.shape[1]
    nx = SSD_HEADS * SSD_HEAD_DIM // LANES
    ng = SSD_GROUPS
    inner = nx * LANES
    tot = t_lat + t_ctx
    rows = t_lat // GRID_W
    kern = functools.partial(_ssd_kernel, t_lat=t_lat, t_ctx=t_ctx, need_ctx=need_ctx)
    full = lambda a: pl.BlockSpec(a.shape, lambda i: (0,) * a.ndim)
    hbm = pl.BlockSpec(memory_space=pl.ANY)
    out_shape = [jax.ShapeDtypeStruct((b, rows, GRID_W, inner), F32)]
    out_specs = [hbm]
    if need_ctx:
        out_shape.append(jax.ShapeDtypeStruct((b, t_ctx, inner), F32))
        out_specs.append(pl.BlockSpec((1, t_ctx, inner), lambda i: (i, 0, 0)))
    res = pl.pallas_call(
        kern,
        out_shape=tuple(out_shape),
        grid=(b,),
        in_specs=[hbm, hbm, full(cw), full(cb), full(dtb), full(alog), full(dsk)],
        out_specs=tuple(out_specs),
        scratch_shapes=[
            pltpu.VMEM((2, tot + 3 * SUBLANES, width), F32),
            pltpu.VMEM((t_lat, inner), F32),
            pltpu.SemaphoreType.DMA((2,)),
            pltpu.SemaphoreType.DMA((1,)),
            pltpu.VMEM((nx + 2 * ng, SSD_CHUNK + 2 * SUBLANES, LANES), F32),
            pltpu.VMEM((2 * nx, tot, LANES), BF16),
            pltpu.VMEM((ng, tot, LANES), BF16),
            pltpu.VMEM((ng, tot, LANES), BF16),
            pltpu.VMEM((tot, LANES), F32),
            pltpu.VMEM((tot // SSD_CHUNK * 2 * SSD_HEADS, LANES), F32),
            pltpu.VMEM((tot // SSD_CHUNK * SUBLANES, LANES), F32),
            pltpu.VMEM((2 * nx, tot, LANES), BF16),
            pltpu.VMEM((2 * nx, SSD_STATE, LANES), F32),
            pltpu.VMEM((nx, tot, LANES), F32),
        ],
        compiler_params=_cparams("arbitrary"),
        name="ssd",
    )(xd_l.reshape(b, rows, GRID_W, width), xd_c, cw, cb, dtb, alog, dsk)
    y_l = res[0].reshape(b, t_lat, inner)
    return (y_l, res[1]) if need_ctx else (y_l, None)


def _outmlp_kernel(*args, final_norm, ff_chunk):
    if final_norm:
        (x_ref, lru_ref, lg_ref, ssd_ref, z_ref, mod_ref, sg_ref, n2_ref, wo_ref, w1_ref, w2_ref, fg_ref,
         o_ref) = args
    else:
        (x_ref, lru_ref, lg_ref, ssd_ref, z_ref, mod_ref, sg_ref, n2_ref, wo_ref, w1_ref, w2_ref, o_ref) = args
    x = x_ref[0]
    z = z_ref[0]
    lw = z.shape[-1]
    gw = lw // SSD_GROUPS
    parts = []
    for g in range(SSD_GROUPS):
        y = ssd_ref[0, :, g * gw:(g + 1) * gw] * _silu(z[:, g * gw:(g + 1) * gw])
        y = y * lax.rsqrt(jnp.mean(y * y, axis=-1, keepdims=True) + EPS)
        parts.append(y * sg_ref[:, g * gw:(g + 1) * gw])
    ssd_n = jnp.concatenate(parts, axis=1).astype(BF16)
    lru = (lru_ref[0] * _gelu_tanh(lg_ref[0])).astype(BF16)
    y = _dot(lru, wo_ref[0:lw, :]) + _dot(ssd_n, wo_ref[lw:2 * lw, :])
    x1 = x + mod_ref[0, 2:3, :] * y
    h = (_rmsnorm(x1, n2_ref[...]) * (1.0 + mod_ref[0, 4:5, :]) + mod_ref[0, 3:4, :]).astype(BF16)
    dff = w1_ref.shape[1]
    acc = jnp.zeros_like(x1)
    for j in range(dff // ff_chunk):
        a = _dot(h, w1_ref[:, j * ff_chunk:(j + 1) * ff_chunk])
        a = jnp.square(jnp.maximum(a, 0.0)).astype(BF16)
        acc = acc + _dot(a, w2_ref[j * ff_chunk:(j + 1) * ff_chunk, :])
    x2 = x1 + mod_ref[0, 5:6, :] * acc
    if final_norm:
        x2 = _rmsnorm(x2, fg_ref[...])
    o_ref[0] = x2


def _outmlp(x, lru, lg, ssd, z, mod, mod_row, sg, n2, wo, w1, w2, layer, fg, *, tm, ff_chunk=1024):
    b, t, d = x.shape
    lw = lru.shape[-1]
    final_norm = fg is not None
    kern = functools.partial(_outmlp_kernel, final_norm=final_norm, ff_chunk=ff_chunk)
    tok = lambda n: pl.BlockSpec((1, tm, n), lambda i, j: (i, j, 0))
    res = lambda a: pl.BlockSpec((None,) + a.shape[1:], lambda i, j: (layer, 0, 0), pipeline_mode=pl.Buffered(1))
    row = lambda n: pl.BlockSpec((1, n), lambda i, j: (0, 0))
    ins = [x, lru, lg, ssd, z, mod, sg, n2, wo, w1, w2]
    in_specs = [tok(d), tok(lw), tok(lw), tok(lw), tok(lw),
                pl.BlockSpec((1, N_MOD, d), lambda i, j: (mod_row(i), 0, 0)),
                row(lw), row(d), res(wo), res(w1), res(w2)]
    if final_norm:
        ins.append(fg)
        in_specs.append(row(d))
    return pl.pallas_call(
        kern,
        out_shape=jax.ShapeDtypeStruct((b, t, d), F32),
        grid=(b, t // tm),
        in_specs=in_specs,
        out_specs=tok(d),
        compiler_params=_cparams("arbitrary", "arbitrary"),
        name="outmlp",
    )(*ins)


def _pair_block_diag(w):
    two, h, k, _ = w.shape
    w = w.reshape(two, h // 2, 2, k, k)
    z = jnp.zeros_like(w[:, :, 0])
    top = jnp.concatenate([w[:, :, 0], z], axis=-1)
    bot = jnp.concatenate([z, w[:, :, 1]], axis=-1)
    return jnp.concatenate([top, bot], axis=-2)


def _pad_lanes(a):
    return jnp.pad(a, [(0, 0)] * (a.ndim - 1) + [(0, LANES - a.shape[-1])])


def kernel(x, c, ctx, c_ctx, ada_w, ada_b, norm1_g, norm2_g, w_in, lru_conv_w, lru_conv_b, lru_wa, lru_ba,
           lru_wx, lru_bx, lru_lambda, ssd_conv_w, ssd_conv_b, ssd_dt_bias, ssd_a_log, ssd_d, ssd_norm_g,
           w_out, mlp_w1, mlp_w2, final_g):
    bsz, t_lat, d = x.shape
    t_ctx = ctx.shape[1]
    depth = ada_w.shape[0]
    lw = lru_conv_w.shape[-1]
    nxbc = ssd_conv_w.shape[-1]
    ndt = 2 * SSD_HEADS
    nscan = lw + nxbc + ndt
    assert bsz + 1 <= MOD_ROWS and t_lat % (GRID_W * SUBLANES) == 0 and t_ctx % SSD_CHUNK == 0

    cvec = jnp.zeros((MOD_ROWS, d), F32).at[:bsz].set(c).at[bsz].set(c_ctx)
    mod = _modulation(cvec, ada_w, ada_b).reshape(depth, MOD_ROWS, N_MOD, d)

    lat_row = lambda i: i
    ctx_row = lambda i: bsz
    tm_lat = 512
    tm_ctx = t_ctx
    nslab = nxbc // LANES
    nxd = nxbc + LANES

    w_re = jnp.concatenate(
        [w_in[:, :, :lw + nxbc], _pad_lanes(w_in[:, :, lw + nxbc:nscan]), w_in[:, :, nscan:]], axis=2).astype(BF16)
    wo = w_out.astype(BF16)
    w1 = mlp_w1.astype(BF16)
    w2 = mlp_w2.astype(BF16)

    for l in range(depth):
        need_ctx = l < depth - 1
        g1 = norm1_g[l].reshape(1, d)
        lx_l, xd_l, lg_l, z_l = _inproj(x, mod[l], lat_row, g1, w_re, l, lw=lw, nxd=nxd, tm=2 * tm_lat)
        lx_c, xd_c, lg_c, z_c = _inproj(ctx, mod[l], ctx_row, g1, w_re, l, lw=lw, nxd=nxd, tm=tm_ctx)

        lru_l, lru_c = _lru(
            lx_l, lx_c, lru_conv_w[l], lru_conv_b[l].reshape(1, lw),
            _pair_block_diag(0.5 * lru_wa[l]).astype(BF16), _pair_block_diag(0.5 * lru_wx[l]).astype(BF16),
            0.5 * lru_ba[l], 0.5 * lru_bx[l], lru_lambda[l], need_ctx=need_ctx)

        cw = 0.5 * ssd_conv_w[l].reshape(CONV_K, nslab, LANES).transpose(1, 0, 2)
        cb = 0.5 * ssd_conv_b[l].reshape(nslab, 1, LANES)
        dtb = _pad_lanes(ssd_dt_bias[l].reshape(1, ndt))
        alog = _pad_lanes(ssd_a_log[l].reshape(1, ndt))
        dsk = jnp.repeat(ssd_d[l], SSD_HEAD_DIM).reshape(lw // LANES, 1, LANES)
        ssd_l, ssd_c = _ssd(xd_l, xd_c, cw, cb, dtb, alog, dsk, need_ctx=need_ctx)

        sg = ssd_norm_g[l].reshape(1, lw)
        n2 = norm2_g[l].reshape(1, d)
        fg = final_g.reshape(1, d) if l == depth - 1 else None
        x = _outmlp(x, lru_l, lg_l, ssd_l, z_l, mod[l], lat_row, sg, n2, wo, w1, w2, l, fg, tm=tm_lat)
        if need_ctx:
            ctx = _outmlp(ctx, lru_c, lg_c, ssd_c, z_c, mod[l], ctx_row, sg, n2, wo, w1, w2, l, None,
                          tm=tm_ctx)
    return x
```

```python
import functools
import math

import jax
import jax.numpy as jnp
from jax import lax
from jax.experimental import pallas as pl
from jax.experimental.pallas import tpu as pltpu

F32 = jnp.float32
BF16 = jnp.bfloat16

EPS = 1e-6
GRID_W = 64
LRU_BLOCK = 64
LRU_C = 8.0
SSD_HEAD_DIM = 64
SSD_HEADS = 8
SSD_GROUPS = 2
SSD_STATE = 128
SSD_CHUNK = 128
CONV_K = 4
N_MOD = 6
LANES = 128
SUBLANES = 8
MOD_ROWS = 16
VMEM_LIMIT = 56 * 1024 * 1024
NEG_BIG = -1e30


def _sigmoid(x):
    return 0.5 * (jnp.tanh(0.5 * x) + 1.0)


def _silu(x):
    return x * _sigmoid(x)


def _softplus(x):
    return jnp.maximum(x, 0.0) + jnp.log1p(jnp.exp(-jnp.abs(x)))


def _gelu_tanh(x):
    c = math.sqrt(2.0 / math.pi)
    return (0.5 * x) * (1.0 + jnp.tanh(x * (c + (c * 0.044715) * (x * x))))


def _rmsnorm(x, g):
    return x * lax.rsqrt(jnp.mean(x * x, axis=-1, keepdims=True) + EPS) * g


def _dot(a, b):
    return jnp.dot(a, b, preferred_element_type=F32)


def _cparams(*sem):
    return pltpu.CompilerParams(dimension_semantics=sem, vmem_limit_bytes=VMEM_LIMIT)


def _mod_kernel(c_ref, w_ref, b_ref, o_ref):
    s = _silu(c_ref[...]).astype(BF16)
    o_ref[0] = _dot(s, w_ref[0].astype(BF16)) + b_ref[0]


def _modulation(cvec, ada_w, ada_b):
    depth, d, n = ada_w.shape
    tn = 1536
    return pl.pallas_call(
        _mod_kernel,
        out_shape=jax.ShapeDtypeStruct((depth, MOD_ROWS, n), F32),
        grid=(depth, n // tn),
        in_specs=[
            pl.BlockSpec((MOD_ROWS, d), lambda l, j: (0, 0)),
            pl.BlockSpec((1, d, tn), lambda l, j: (l, 0, j)),
            pl.BlockSpec((1, 1, tn), lambda l, j: (l, 0, j)),
        ],
        out_specs=pl.BlockSpec((1, MOD_ROWS, tn), lambda l, j: (l, 0, j)),
        compiler_params=_cparams("arbitrary", "arbitrary"),
        name="modulation",
    )(cvec, ada_w, ada_b.reshape(depth, 1, n))


def _inproj_kernel(x_ref, mod_ref, g_ref, w_ref, lx_ref, xd_ref, lg_ref, z_ref, *, lw, nxd):
    x = x_ref[0]
    h = _rmsnorm(x, g_ref[...]) * (1.0 + mod_ref[0, 1:2, :]) + mod_ref[0, 0:1, :]
    h = h.astype(BF16)
    lx_ref[0] = _dot(h, w_ref[:, 0:lw])
    xd_ref[0] = _dot(h, w_ref[:, lw:lw + nxd])
    o = lw + nxd
    lg_ref[0] = _dot(h, w_ref[:, o:o + lw])
    z_ref[0] = _dot(h, w_ref[:, o + lw:o + 2 * lw])


def _inproj(x, mod, mod_row, g, w, layer, *, lw, nxd, tm):
    b, t, d = x.shape
    kern = functools.partial(_inproj_kernel, lw=lw, nxd=nxd)
    tok = lambda n: pl.BlockSpec((1, tm, n), lambda i, j: (i, j, 0))
    return pl.pallas_call(
        kern,
        out_shape=(
            jax.ShapeDtypeStruct((b, t, lw), F32),
            jax.ShapeDtypeStruct((b, t, nxd), F32),
            jax.ShapeDtypeStruct((b, t, lw), F32),
            jax.ShapeDtypeStruct((b, t, lw), F32),
        ),
        grid=(b, t // tm),
        in_specs=[
            tok(d),
            pl.BlockSpec((1, N_MOD, d), lambda i, j: (mod_row(i), 0, 0)),
            pl.BlockSpec((1, d), lambda i, j: (0, 0)),
            pl.BlockSpec((None,) + w.shape[1:], lambda i, j: (layer, 0, 0), pipeline_mode=pl.Buffered(1)),
        ],
        out_specs=(tok(lw), tok(nxd), tok(lw), tok(lw)),
        compiler_params=_cparams("arbitrary", "arbitrary"),
        name="inproj",
    )(x, mod, g, w)


LRU_ROWS = 256
LRU_TILES = 2


def _lru_seq(t, u_ref, out_ref, h0, refs, scr, need_out):
    cw_ref, cb_ref, wa_ref, wx_ref, ba_ref, bx_ref, lam_ref = refs
    upad, ubuf, uint, hbuf, pbuf, ybuf = scr
    seg = t // SUBLANES
    pitch = seg + 4
    nblk = max(t // LRU_ROWS, 1)
    rb = t // nblk
    nv = rb // SUBLANES
    tiles = [slice(c * LANES, (c + 1) * LANES) for c in range(LRU_TILES)]

    zeros8 = jnp.zeros((SUBLANES, LANES), F32)
    ones8 = jnp.ones((SUBLANES, LANES), F32)
    for c, ln in enumerate(tiles):
        upad[c, 0:SUBLANES, :] = zeros8
        upad[c, SUBLANES:SUBLANES + t, :] = u_ref[:, ln]
        upad[c, SUBLANES + t:2 * SUBLANES + t, :] = zeros8
        cw = cw_ref[:, ln]
        for s in range(SUBLANES):
            acc = cb_ref[:, ln]
            for tap in range(CONV_K):
                r = SUBLANES + s * seg + tap - 1
                acc = acc + cw[tap:tap + 1, :] * upad[c, r:r + seg, :]
            ubuf[c, s * pitch:s * pitch + seg, :] = acc
        for tau in range(seg):
            uint[c, tau * SUBLANES:(tau + 1) * SUBLANES, :] = ubuf[c, pl.ds(tau, SUBLANES, stride=pitch), :]

    row = lax.broadcasted_iota(jnp.int32, (SUBLANES, LANES), 0)
    kdec = [[(0.5 * LRU_C) * _softplus(-lam_ref[d:d + 1, ln]) for d in range(2)] for ln in tiles]
    chains = [(c, d) for c in range(LRU_TILES) for d in range(2)]

    def blk_body(i, carry):
        carry = list(carry)
        for q, (c, d) in enumerate(chains):
            ln = tiles[c]
            h, p = carry[2 * q], carry[2 * q + 1]
            blk = i if d == 0 else nblk - 1 - i
            r0 = pl.multiple_of(blk * rb, rb)
            u = uint[c, pl.ds(r0, rb), :]
            ub = u.astype(BF16)
            nla = kdec[c][d] * jnp.tanh(_dot(ub, wa_ref[d, c]) + ba_ref[d:d + 1, ln]) + kdec[c][d]
            gate = jnp.tanh(_dot(ub, wx_ref[d, c]) + bx_ref[d:d + 1, ln]) + 1.0
            a = jnp.exp2(nla * (-math.log2(math.e)))
            v = jnp.tanh(nla) * (1.0 + a * a)
            root = jnp.where(v > 0.0, v * lax.rsqrt(v), 0.0)
            bb = root * (gate * u)
            hs = [None] * nv
            ps = [None] * nv
            order = range(nv) if d == 0 else range(nv - 1, -1, -1)
            for k in order:
                ak = a[k * SUBLANES:(k + 1) * SUBLANES, :]
                h = ak * h + bb[k * SUBLANES:(k + 1) * SUBLANES, :]
                p = ak * p
                hs[k] = h
                ps[k] = p
            hbuf[q, pl.ds(r0, rb), :] = jnp.concatenate(hs, axis=0)
            pbuf[q, pl.ds(r0, rb), :] = jnp.concatenate(ps, axis=0)
            carry[2 * q], carry[2 * q + 1] = h, p
        return tuple(carry)

    hp = lax.fori_loop(0, nblk, blk_body, (zeros8, ones8) * len(chains))
    finals = []
    cmats = []
    for q, (c, d) in enumerate(chains):
        h, p = hp[2 * q], hp[2 * q + 1]
        st = h0[q]
        cmat = zeros8
        order = range(SUBLANES) if d == 0 else range(SUBLANES - 1, -1, -1)
        for s in order:
            cmat = jnp.where(row == s, jnp.broadcast_to(st, (SUBLANES, LANES)), cmat)
            st = h[s:s + 1, :] + p[s:s + 1, :] * st
        finals.append(st)
        cmats.append(cmat)

    if need_out:
        def out_body(i, carry):
            r0 = pl.multiple_of(i * rb, rb)
            for c in range(LRU_TILES):
                y = None
                for q in (2 * c, 2 * c + 1):
                    t1 = hbuf[q, pl.ds(r0, rb), :].reshape(nv, SUBLANES, LANES)
                    t2 = pbuf[q, pl.ds(r0, rb), :].reshape(nv, SUBLANES, LANES) * cmats[q][None]
                    y = t1 + t2 if y is None else y + t1 + t2
                ybuf[c, pl.ds(r0, rb), :] = y.reshape(rb, LANES)
            return carry

        lax.fori_loop(0, nblk, out_body, 0)
        for c, ln in enumerate(tiles):
            for tau in range(seg):
                ubuf[c, pl.ds(tau, SUBLANES, stride=pitch), :] = ybuf[c, tau * SUBLANES:(tau + 1) * SUBLANES, :]
            for s in range(SUBLANES):
                out_ref[s * seg:(s + 1) * seg, ln] = ubuf[c, s * pitch:s * pitch + seg, :]
    return finals


def _lru_kernel(*args, t_lat, t_ctx, need_ctx):
    if need_ctx:
        (ul_ref, uc_ref, *refs, yl_ref, yc_ref, upad, ubuf, uint, hbuf, pbuf, ybuf) = args
    else:
        (ul_ref, uc_ref, *refs, yl_ref, upad, ubuf, uint, hbuf, pbuf, ybuf) = args
        yc_ref = None
    scr = (upad, ubuf, uint, hbuf, pbuf, ybuf)
    zero = jnp.zeros((1, LANES), F32)
    ctx_final = _lru_seq(t_ctx, uc_ref.at[0], yc_ref.at[0] if need_ctx else None,
                         (zero,) * (2 * LRU_TILES), refs, scr, need_ctx)
    _lru_seq(t_lat, ul_ref.at[0], yl_ref.at[0], ctx_final, refs, scr, True)


def _lru(lx_l, lx_c, conv_w, conv_b, wa_bd, wx_bd, ba, bx, lam, *, need_ctx):
    b, t_lat, w = lx_l.shape
    t_ctx = lx_c.shape[1]
    cwid = LRU_TILES * LANES
    seq = lambda t: pl.BlockSpec((1, t, cwid), lambda i, j: (i, 0, j))
    par = lambda r: pl.BlockSpec((r, cwid), lambda i, j: (0, j))
    gate = pl.BlockSpec((2, LRU_TILES, LANES, LANES), lambda i, j: (0, j, 0, 0))
    kern = functools.partial(_lru_kernel, t_lat=t_lat, t_ctx=t_ctx, need_ctx=need_ctx)
    ins = [lx_l, lx_c, conv_w, conv_b, wa_bd, wx_bd, ba, bx, lam]
    in_specs = [seq(t_lat), seq(t_ctx), par(CONV_K), par(1), gate, gate, par(2), par(2), par(2)]
    out_shape = [jax.ShapeDtypeStruct((b, t_lat, w), F32)]
    out_specs = [seq(t_lat)]
    if need_ctx:
        out_shape.append(jax.ShapeDtypeStruct((b, t_ctx, w), F32))
        out_specs.append(seq(t_ctx))
    pitch_rows = SUBLANES * (t_lat // SUBLANES + 4) + SUBLANES
    res = pl.pallas_call(
        kern,
        out_shape=tuple(out_shape),
        grid=(b, w // cwid),
        in_specs=in_specs,
        out_specs=tuple(out_specs),
        scratch_shapes=[
            pltpu.VMEM((LRU_TILES, t_lat + 2 * SUBLANES, LANES), F32),
            pltpu.VMEM((LRU_TILES, pitch_rows, LANES), F32),
            pltpu.VMEM((LRU_TILES, t_lat, LANES), F32),
            pltpu.VMEM((2 * LRU_TILES, t_lat, LANES), F32),
            pltpu.VMEM((2 * LRU_TILES, t_lat, LANES), F32),
            pltpu.VMEM((LRU_TILES, t_lat, LANES), F32),
        ],
        compiler_params=_cparams("arbitrary", "arbitrary"),
        name="rglru",
    )(*ins)
    return (res[0], res[1]) if need_ctx else (res[0], None)


def _ssd_kernel(*args, t_lat, t_ctx, need_ctx):
    if need_ctx:
        (xl_hbm, xc_hbm, cw_ref, cb_ref, dtb_ref, alog_ref, dsk_ref, yl_hbm, yc_ref,
         xg, ybuf, sem_in, sem_out, stg, xm, scc, sbt, cumb, srct, etb, sbuf, hst, yacc) = args
    else:
        (xl_hbm, xc_hbm, cw_ref, cb_ref, dtb_ref, alog_ref, dsk_ref, yl_hbm,
         xg, ybuf, sem_in, sem_out, stg, xm, scc, sbt, cumb, srct, etb, sbuf, hst, yacc) = args
        yc_ref = None
    ck = SSD_CHUNK
    rows = t_lat // GRID_W
    nc_ctx = t_ctx // ck
    nc_lat = t_lat // ck
    nx = SSD_HEADS * SSD_HEAD_DIM // LANES
    ng = SSD_GROUPS
    nslab = nx + 2 * ng
    halo = SUBLANES

    ri = lax.broadcasted_iota(jnp.int32, (ck, ck), 0)
    ci = lax.broadcasted_iota(jnp.int32, (ck, ck), 1)
    lane_lo = ci < SSD_HEAD_DIM

    b = pl.program_id(0)
    nb = pl.num_programs(0)
    slot = lax.rem(b, 2)
    ctx0 = halo
    lat0 = ctx0 + t_ctx + halo
    width = xg.shape[-1]

    def in_copies(bi, sl):
        cps = [pltpu.make_async_copy(xc_hbm.at[bi], xg.at[sl, pl.ds(ctx0, t_ctx), :], sem_in.at[sl])]
        for w in range(GRID_W):
            cps.append(pltpu.make_async_copy(
                xl_hbm.at[bi, :, w, :], xg.at[sl, pl.ds(lat0 + w * rows, rows), :], sem_in.at[sl]))
        return cps

    def out_copies(bi):
        return [pltpu.make_async_copy(ybuf.at[pl.ds(w * rows, rows), :], yl_hbm.at[bi, :, w, :], sem_out.at[0])
                for w in range(GRID_W)]

    @pl.when(b == 0)
    def _():
        zpad = jnp.zeros((halo, width), F32)
        for sl in range(2):
            for r in (0, ctx0 + t_ctx, lat0 + t_lat):
                xg[sl, r:r + halo, :] = zpad
        for cp in in_copies(0, 0):
            cp.start()

    @pl.when(b + 1 < nb)
    def _():
        for cp in in_copies(b + 1, 1 - slot):
            cp.start()

    for cp in in_copies(b, slot):
        cp.wait()

    xs = xg.at[slot]

    def src_row(k):
        return pl.multiple_of(jnp.where(k < nc_ctx, ctx0 + k * ck, lat0 + (k - nc_ctx) * ck), SUBLANES)

    def conv_silu(src0, j):
        stg[j] = xs[pl.ds(src0 - halo, ck + 2 * halo), j * LANES:(j + 1) * LANES]
        cw = cw_ref[j]
        half = cb_ref[j]
        for tap in range(CONV_K):
            half = half + cw[tap:tap + 1, :] * stg[j, halo + tap - 1:halo + tap - 1 + ck, :]
        return half * jnp.tanh(half) + half

    dtb = dtb_ref[...]
    lane_ok = lax.broadcasted_iota(jnp.int32, (1, LANES), 1) < 2 * SSD_HEADS
    nega2 = jnp.where(lane_ok, -jnp.exp(alog_ref[...]) * math.log2(math.e), 0.0)

    nck = nc_ctx + nc_lat
    nh2 = 2 * SSD_HEADS
    hpp = SSD_HEADS // ng // 2
    dir1_col = lax.broadcasted_iota(jnp.int32, (1, LANES), 1) >= SSD_HEADS
    tris = (ri >= ci, ri <= ci)

    def xcat_of(j, r0):
        return jnp.concatenate([xm[j, pl.ds(r0, ck), :], xm[nx + j, pl.ds(r0, ck), :]], axis=0)

    def pass1(kc, carry):
        r0 = pl.multiple_of(kc * ck, ck)
        src0 = src_row(kc)
        dt = _softplus(xs[pl.ds(src0, ck), nslab * LANES:(nslab + 1) * LANES] + dtb)
        la = dt * nega2
        cumf = la
        for sh in (1, 2, 4):
            cumf = cumf + jnp.where(ri >= sh, pltpu.roll(cumf, sh, 0), 0.0)
        for sh in (8, 16, 32, 64):
            cumf = cumf + jnp.concatenate([jnp.zeros((sh, LANES), F32), cumf[0:ck - sh, :]], axis=0)
        tot = cumf[ck - 1:ck, :]
        cum = jnp.where(dir1_col, tot - cumf + la, cumf)
        ldt = jnp.log2(dt)
        cumb[pl.ds(r0, ck), :] = cum
        srct[pl.ds(pl.multiple_of(kc * nh2, nh2), nh2), :] = (cum - ldt).T[0:nh2, :]
        wgt_t = jnp.exp2((tot - cum + ldt).T[0:nh2, :])
        etb[pl.ds(pl.multiple_of(kc * SUBLANES, SUBLANES), SUBLANES), :] = jnp.broadcast_to(
            jnp.exp2(tot), (SUBLANES, LANES))
        for g in range(ng):
            scc[g, pl.ds(r0, ck), :] = conv_silu(src0, nx + ng + g).astype(BF16)
            btf = conv_silu(src0, nx + g).T
            sbt[g, pl.ds(r0, ck), :] = btf.astype(BF16)
            for pr in range(hpp):
                j = g * hpp + pr
                x = conv_silu(src0, j)
                yacc[j, pl.ds(r0, ck), :] = dsk_ref[j] * x
                xb = x.astype(BF16)
                zb = jnp.zeros_like(xb)
                xlo = jnp.where(lane_lo, xb, zb)
                xhi = jnp.where(lane_lo, zb, xb)
                xm[j, pl.ds(r0, ck), :] = xlo
                xm[nx + j, pl.ds(r0, ck), :] = xhi
                btw = []
                for d in range(2):
                    hc = (d * SSD_HEADS + 2 * j, d * SSD_HEADS + 2 * j + 1)
                    btw.append(jnp.concatenate([(btf * wgt_t[h:h + 1, :]).astype(BF16) for h in hc], axis=1))
                s = _dot(jnp.concatenate(btw, axis=0), jnp.concatenate([xlo, xhi], axis=0)).astype(BF16)
                sbuf[j, pl.ds(r0, ck), :] = s[0:SSD_STATE, :]
                sbuf[nx + j, pl.ds(r0, ck), :] = s[SSD_STATE:2 * SSD_STATE, :]
        return carry

    lax.fori_loop(0, nck, pass1, 0, unroll=2)

    hst[...] = jnp.zeros(hst.shape, F32)

    def pass2(i, carry):
        for d in range(2):
            if d == 0:
                kc = i
            else:
                kc = jnp.where(i < nc_ctx, nc_ctx - 1 - i, nck - 1 - (i - nc_ctx))
            r0 = pl.multiple_of(kc * ck, ck)
            e = etb[pl.ds(pl.multiple_of(kc * SUBLANES, SUBLANES), 1), :]
            for j in range(nx):
                hc = (d * SSD_HEADS + 2 * j, d * SSD_HEADS + 2 * j + 1)
                et = jnp.where(lane_lo[0:1, :], jnp.broadcast_to(e[:, hc[0]:hc[0] + 1], (1, LANES)),
                               jnp.broadcast_to(e[:, hc[1]:hc[1] + 1], (1, LANES)))
                h = hst[d * nx + j]
                s = sbuf[d * nx + j, pl.ds(r0, ck), :].astype(F32)
                sbuf[d * nx + j, pl.ds(r0, ck), :] = h.astype(BF16)
                hst[d * nx + j] = h * et + s
        return carry

    lax.fori_loop(0, nck, pass2, 0)

    def pass3(kc, is_ctx):
        r0 = kc * ck if isinstance(kc, int) else pl.multiple_of(kc * ck, ck)
        s0 = kc * nh2 if isinstance(kc, int) else pl.multiple_of(kc * nh2, nh2)
        cum = cumb[pl.ds(r0, ck), :]
        src_t = srct[pl.ds(s0, nh2), :]
        for g in range(ng):
            cm = scc[g, pl.ds(r0, ck), :]
            cb = _dot(cm, sbt[g, pl.ds(r0, ck), :])
            for pr in range(hpp):
                j = g * hpp + pr
                dec = []
                ecs = []
                for d in range(2):
                    hc = (d * SSD_HEADS + 2 * j, d * SSD_HEADS + 2 * j + 1)
                    col = [jnp.broadcast_to(cum[:, h:h + 1], (ck, LANES)) for h in hc]
                    dec.append([jnp.exp2(jnp.where(tris[d], col[q] - src_t[hc[q]:hc[q] + 1, :], NEG_BIG))
                                for q in range(2)])
                    ecs.append(jnp.exp2(jnp.where(lane_lo, col[0], col[1])))
                ms = [(cb * (dec[0][q] + dec[1][q])).astype(BF16) for q in range(2)]
                y = yacc[j, pl.ds(r0, ck), :] + _dot(jnp.concatenate(ms, axis=1), xcat_of(j, r0))
                hin = jnp.concatenate([sbuf[j, pl.ds(r0, ck), :], sbuf[nx + j, pl.ds(r0, ck), :]], axis=1)
                yoff = _dot(cm, hin)
                y = y + yoff[:, 0:LANES] * ecs[0] + yoff[:, LANES:2 * LANES] * ecs[1]
                if is_ctx:
                    yc_ref[0, pl.ds(r0, ck), j * LANES:(j + 1) * LANES] = y
                else:
                    y0 = pl.multiple_of((kc - nc_ctx) * ck, ck)
                    ybuf[pl.ds(y0, ck), j * LANES:(j + 1) * LANES] = y

    if need_ctx:
        for kc in range(nc_ctx):
            pass3(kc, True)

    @pl.when(b > 0)
    def _():
        for cp in out_copies(b - 1):
            cp.wait()

    def lat_body(c, carry):
        pass3(nc_ctx + c, False)
        return carry

    lax.fori_loop(0, nc_lat, lat_body, 0, unroll=2)

    for cp in out_copies(b):
        cp.start()

    @pl.when(b == nb - 1)
    def _():
        for cp in out_copies(b):
            cp.wait()


def _ssd(xd_l, xd_c, cw, cb, dtb, alog, dsk, *, need_ctx):
    b, t_lat, width = xd_l.shape
    t_ctx = xd_c.shape[1]
    nx = SSD_HEADS * SSD_HEAD_DIM // LANES
    ng = SSD_GROUPS
    inner = nx * LANES
    tot = t_lat + t_ctx
    rows = t_lat // GRID_W
    kern = functools.partial(_ssd_kernel, t_lat=t_lat, t_ctx=t_ctx, need_ctx=need_ctx)
    full = lambda a: pl.BlockSpec(a.shape, lambda i: (0,) * a.ndim)
    hbm = pl.BlockSpec(memory_space=pl.ANY)
    out_shape = [jax.ShapeDtypeStruct((b, rows, GRID_W, inner), F32)]
    out_specs = [hbm]
    if need_ctx:
        out_shape.append(jax.ShapeDtypeStruct((b, t_ctx, inner), F32))
        out_specs.append(pl.BlockSpec((1, t_ctx, inner), lambda i: (i, 0, 0)))
    res = pl.pallas_call(
        kern,
        out_shape=tuple(out_shape),
        grid=(b,),
        in_specs=[hbm, hbm, full(cw), full(cb), full(dtb), full(alog), full(dsk)],
        out_specs=tuple(out_specs),
        scratch_shapes=[
            pltpu.VMEM((2, tot + 3 * SUBLANES, width), F32),
            pltpu.VMEM((t_lat, inner), F32),
            pltpu.SemaphoreType.DMA((2,)),
            pltpu.SemaphoreType.DMA((1,)),
            pltpu.VMEM((nx + 2 * ng, SSD_CHUNK + 2 * SUBLANES, LANES), F32),
            pltpu.VMEM((2 * nx, tot, LANES), BF16),
            pltpu.VMEM((ng, tot, LANES), BF16),
            pltpu.VMEM((ng, tot, LANES), BF16),
            pltpu.VMEM((tot, LANES), F32),
            pltpu.VMEM((tot // SSD_CHUNK * 2 * SSD_HEADS, LANES), F32),
            pltpu.VMEM((tot // SSD_CHUNK * SUBLANES, LANES), F32),
            pltpu.VMEM((2 * nx, tot, LANES), BF16),
            pltpu.VMEM((2 * nx, SSD_STATE, LANES), F32),
            pltpu.VMEM((nx, tot, LANES), F32),
        ],
        compiler_params=_cparams("arbitrary"),
        name="ssd",
    )(xd_l.reshape(b, rows, GRID_W, width), xd_c, cw, cb, dtb, alog, dsk)
    y_l = res[0].reshape(b, t_lat, inner)
    return (y_l, res[1]) if need_ctx else (y_l, None)


def _outmlp_kernel(*args, final_norm, ff_chunk, row_groups):
    if final_norm:
        (x_ref, lru_ref, lg_ref, ssd_ref, z_ref, mod_ref, sg_ref, n2_ref, wo_ref, w1_ref, w2_ref, fg_ref,
         o_ref) = args
    else:
        (x_ref, lru_ref, lg_ref, ssd_ref, z_ref, mod_ref, sg_ref, n2_ref, wo_ref, w1_ref, w2_ref, o_ref) = args
    lw = z_ref.shape[-1]
    gw = lw // SSD_GROUPS
    dff = w1_ref.shape[1]
    tm = x_ref.shape[1]
    for r0 in range(0, tm, tm // row_groups):
        rs = slice(r0, r0 + tm // row_groups)
        parts = []
        for g in range(SSD_GROUPS):
            y = ssd_ref[0, rs, g * gw:(g + 1) * gw] * _silu(z_ref[0, rs, g * gw:(g + 1) * gw])
            y = y * lax.rsqrt(jnp.mean(y * y, axis=-1, keepdims=True) + EPS)
            parts.append(y * sg_ref[:, g * gw:(g + 1) * gw])
        ssd_n = jnp.concatenate(parts, axis=1).astype(BF16)
        lru = (lru_ref[0, rs, :] * _gelu_tanh(lg_ref[0, rs, :])).astype(BF16)
        y = _dot(lru, wo_ref[0:lw, :]) + _dot(ssd_n, wo_ref[lw:2 * lw, :])
        x1 = x_ref[0, rs, :] + mod_ref[0, 2:3, :] * y
        h = (_rmsnorm(x1, n2_ref[...]) * (1.0 + mod_ref[0, 4:5, :]) + mod_ref[0, 3:4, :]).astype(BF16)
        acc = jnp.zeros_like(x1)
        for j in range(dff // ff_chunk):
            a = _dot(h, w1_ref[:, j * ff_chunk:(j + 1) * ff_chunk])
            a = jnp.square(jnp.maximum(a, 0.0)).astype(BF16)
            acc = acc + _dot(a, w2_ref[j * ff_chunk:(j + 1) * ff_chunk, :])
        x2 = x1 + mod_ref[0, 5:6, :] * acc
        if final_norm:
            x2 = _rmsnorm(x2, fg_ref[...])
        o_ref[0, rs, :] = x2


def _outmlp(x, lru, lg, ssd, z, mod, mod_row, sg, n2, wo, w1, w2, layer, fg, *, tm, ff_chunk=1024, row_groups=1):
    b, t, d = x.shape
    lw = lru.shape[-1]
    final_norm = fg is not None
    kern = functools.partial(_outmlp_kernel, final_norm=final_norm, ff_chunk=ff_chunk, row_groups=row_groups)
    tok = lambda n: pl.BlockSpec((1, tm, n), lambda i, j: (i, j, 0))
    res = lambda a: pl.BlockSpec((None,) + a.shape[1:], lambda i, j: (layer, 0, 0), pipeline_mode=pl.Buffered(1))
    row = lambda n: pl.BlockSpec((1, n), lambda i, j: (0, 0))
    ins = [x, lru, lg, ssd, z, mod, sg, n2, wo, w1, w2]
    in_specs = [tok(d), tok(lw), tok(lw), tok(lw), tok(lw),
                pl.BlockSpec((1, N_MOD, d), lambda i, j: (mod_row(i), 0, 0)),
                row(lw), row(d), res(wo), res(w1), res(w2)]
    if final_norm:
        ins.append(fg)
        in_specs.append(row(d))
    return pl.pallas_call(
        kern,
        out_shape=jax.ShapeDtypeStruct((b, t, d), F32),
        grid=(b, t // tm),
        in_specs=in_specs,
        out_specs=tok(d),
        compiler_params=_cparams("arbitrary", "arbitrary"),
        name="outmlp",
    )(*ins)


def _pair_block_diag(w):
    two, h, k, _ = w.shape
    w = w.reshape(two, h // 2, 2, k, k)
    z = jnp.zeros_like(w[:, :, 0])
    top = jnp.concatenate([w[:, :, 0], z], axis=-1)
    bot = jnp.concatenate([z, w[:, :, 1]], axis=-1)
    return jnp.concatenate([top, bot], axis=-2)


def _pad_lanes(a):
    return jnp.pad(a, [(0, 0)] * (a.ndim - 1) + [(0, LANES - a.shape[-1])])


def kernel(x, c, ctx, c_ctx, ada_w, ada_b, norm1_g, norm2_g, w_in, lru_conv_w, lru_conv_b, lru_wa, lru_ba,
           lru_wx, lru_bx, lru_lambda, ssd_conv_w, ssd_conv_b, ssd_dt_bias, ssd_a_log, ssd_d, ssd_norm_g,
           w_out, mlp_w1, mlp_w2, final_g):
    bsz, t_lat, d = x.shape
    t_ctx = ctx.shape[1]
    depth = ada_w.shape[0]
    lw = lru_conv_w.shape[-1]
    nxbc = ssd_conv_w.shape[-1]
    ndt = 2 * SSD_HEADS
    nscan = lw + nxbc + ndt
    assert bsz + 1 <= MOD_ROWS and t_lat % (GRID_W * SUBLANES) == 0 and t_ctx % SSD_CHUNK == 0

    cvec = jnp.zeros((MOD_ROWS, d), F32).at[:bsz].set(c).at[bsz].set(c_ctx)
    mod = _modulation(cvec, ada_w, ada_b).reshape(depth, MOD_ROWS, N_MOD, d)

    lat_row = lambda i: i
    ctx_row = lambda i: bsz
    tm_lat = 512
    tm_ctx = t_ctx
    nslab = nxbc // LANES
    nxd = nxbc + LANES

    w_re = jnp.concatenate(
        [w_in[:, :, :lw + nxbc], _pad_lanes(w_in[:, :, lw + nxbc:nscan]), w_in[:, :, nscan:]], axis=2).astype(BF16)
    wo = w_out.astype(BF16)
    w1 = mlp_w1.astype(BF16)
    w2 = mlp_w2.astype(BF16)

    for l in range(depth):
        need_ctx = l < depth - 1
        g1 = norm1_g[l].reshape(1, d)
        lx_l, xd_l, lg_l, z_l = _inproj(x, mod[l], lat_row, g1, w_re, l, lw=lw, nxd=nxd, tm=2 * tm_lat)
        lx_c, xd_c, lg_c, z_c = _inproj(ctx, mod[l], ctx_row, g1, w_re, l, lw=lw, nxd=nxd, tm=tm_ctx)

        lru_l, lru_c = _lru(
            lx_l, lx_c, 0.5 * lru_conv_w[l], 0.5 * lru_conv_b[l].reshape(1, lw),
            _pair_block_diag(lru_wa[l]).astype(BF16), _pair_block_diag(lru_wx[l]).astype(BF16),
            0.5 * lru_ba[l], 0.5 * lru_bx[l], lru_lambda[l], need_ctx=need_ctx)

        cw = 0.5 * ssd_conv_w[l].reshape(CONV_K, nslab, LANES).transpose(1, 0, 2)
        cb = 0.5 * ssd_conv_b[l].reshape(nslab, 1, LANES)
        dtb = _pad_lanes(ssd_dt_bias[l].reshape(1, ndt))
        alog = _pad_lanes(ssd_a_log[l].reshape(1, ndt))
        dsk = jnp.repeat(ssd_d[l], SSD_HEAD_DIM).reshape(lw // LANES, 1, LANES)
        ssd_l, ssd_c = _ssd(xd_l, xd_c, cw, cb, dtb, alog, dsk, need_ctx=need_ctx)

        sg = ssd_norm_g[l].reshape(1, lw)
        n2 = norm2_g[l].reshape(1, d)
        fg = final_g.reshape(1, d) if l == depth - 1 else None
        x = _outmlp(x, lru_l, lg_l, ssd_l, z_l, mod[l], lat_row, sg, n2, wo, w1, w2, l, fg, tm=tm_lat)
        if need_ctx:
            ctx = _outmlp(ctx, lru_c, lg_c, ssd_c, z_c, mod[l], ctx_row, sg, n2, wo, w1, w2, l, None,
                          tm=tm_ctx)
    return x
```

```python
import functools
import math

import jax
import jax.numpy as jnp
from jax import lax
from jax.experimental import pallas as pl
from jax.experimental.pallas import tpu as pltpu

F32 = jnp.float32
BF16 = jnp.bfloat16

EPS = 1e-6
GRID_W = 64
LRU_BLOCK = 64
LRU_C = 8.0
SSD_HEAD_DIM = 64
SSD_HEADS = 8
SSD_GROUPS = 2
SSD_STATE = 128
SSD_CHUNK = 128
CONV_K = 4
N_MOD = 6
LANES = 128
SUBLANES = 8
MOD_ROWS = 16
VMEM_LIMIT = 56 * 1024 * 1024
NEG_BIG = -1e30


def _sigmoid(x):
    return 0.5 * (jnp.tanh(0.5 * x) + 1.0)


def _silu(x):
    return x * _sigmoid(x)


def _softplus(x):
    return jnp.maximum(x, 0.0) + jnp.log1p(jnp.exp(-jnp.abs(x)))


def _gelu_tanh(x):
    c = math.sqrt(2.0 / math.pi)
    return (0.5 * x) * (1.0 + jnp.tanh(x * (c + (c * 0.044715) * (x * x))))


def _rmsnorm(x, g):
    return x * lax.rsqrt(jnp.mean(x * x, axis=-1, keepdims=True) + EPS) * g


def _dot(a, b):
    return jnp.dot(a, b, preferred_element_type=F32)


def _cparams(*sem):
    return pltpu.CompilerParams(dimension_semantics=sem, vmem_limit_bytes=VMEM_LIMIT)


def _mod_kernel(c_ref, w_ref, b_ref, o_ref):
    s = _silu(c_ref[...]).astype(BF16)
    o_ref[0] = _dot(s, w_ref[0].astype(BF16)) + b_ref[0]


def _modulation(cvec, ada_w, ada_b):
    depth, d, n = ada_w.shape
    tn = 1536
    return pl.pallas_call(
        _mod_kernel,
        out_shape=jax.ShapeDtypeStruct((depth, MOD_ROWS, n), F32),
        grid=(depth, n // tn),
        in_specs=[
            pl.BlockSpec((MOD_ROWS, d), lambda l, j: (0, 0)),
            pl.BlockSpec((1, d, tn), lambda l, j: (l, 0, j)),
            pl.BlockSpec((1, 1, tn), lambda l, j: (l, 0, j)),
        ],
        out_specs=pl.BlockSpec((1, MOD_ROWS, tn), lambda l, j: (l, 0, j)),
        compiler_params=_cparams("arbitrary", "arbitrary"),
        name="modulation",
    )(cvec, ada_w, ada_b.reshape(depth, 1, n))


def _inproj_kernel(x_ref, mod_ref, g_ref, w_ref, lx_ref, xd_ref, lg_ref, z_ref, *, lw, nxd):
    x = x_ref[0]
    h = _rmsnorm(x, g_ref[...]) * (1.0 + mod_ref[0, 1:2, :]) + mod_ref[0, 0:1, :]
    h = h.astype(BF16)
    lx_ref[0] = _dot(h, w_ref[:, 0:lw])
    xd_ref[0] = _dot(h, w_ref[:, lw:lw + nxd])
    o = lw + nxd
    lg_ref[0] = _dot(h, w_ref[:, o:o + lw])
    z_ref[0] = _dot(h, w_ref[:, o + lw:o + 2 * lw])


def _inproj(x, mod, mod_row, g, w, layer, *, lw, nxd, tm):
    b, t, d = x.shape
    kern = functools.partial(_inproj_kernel, lw=lw, nxd=nxd)
    tok = lambda n: pl.BlockSpec((1, tm, n), lambda i, j: (i, j, 0))
    return pl.pallas_call(
        kern,
        out_shape=(
            jax.ShapeDtypeStruct((b, t, lw), F32),
            jax.ShapeDtypeStruct((b, t, nxd), F32),
            jax.ShapeDtypeStruct((b, t, lw), F32),
            jax.ShapeDtypeStruct((b, t, lw), F32),
        ),
        grid=(b, t // tm),
        in_specs=[
            tok(d),
            pl.BlockSpec((1, N_MOD, d), lambda i, j: (mod_row(i), 0, 0)),
            pl.BlockSpec((1, d), lambda i, j: (0, 0)),
            pl.BlockSpec((None,) + w.shape[1:], lambda i, j: (layer, 0, 0), pipeline_mode=pl.Buffered(1)),
        ],
        out_specs=(tok(lw), tok(nxd), tok(lw), tok(lw)),
        compiler_params=_cparams("arbitrary", "arbitrary"),
        name="inproj",
    )(x, mod, g, w)


LRU_ROWS = 256
LRU_TILES = 2


def _lru_seq(t, u_ref, write_out, h0, refs, scr, *, ntile, straight_line=False):
    cw_ref, cb_ref, wa_ref, wx_ref, ba_ref, bx_ref, lam_ref = refs
    upad, ubuf, uint, hbuf, pbuf, ybuf = scr
    seg = t // SUBLANES
    pitch = seg + 4
    nblk = max(t // LRU_ROWS, 1)
    rb = t // nblk
    nv = rb // SUBLANES
    tiles = [slice(c * LANES, (c + 1) * LANES) for c in range(ntile)]
    need_out = write_out is not None

    zeros8 = jnp.zeros((SUBLANES, LANES), F32)
    ones8 = jnp.ones((SUBLANES, LANES), F32)
    for c, ln in enumerate(tiles):
        upad[c, 0:SUBLANES, :] = zeros8
        upad[c, SUBLANES:SUBLANES + t, :] = u_ref[:, ln]
        upad[c, SUBLANES + t:2 * SUBLANES + t, :] = zeros8
        cw = cw_ref[:, ln]
        for s in range(SUBLANES):
            acc = cb_ref[:, ln]
            for tap in range(CONV_K):
                r = SUBLANES + s * seg + tap - 1
                acc = acc + cw[tap:tap + 1, :] * upad[c, r:r + seg, :]
            ubuf[c, s * pitch:s * pitch + seg, :] = acc
        for tau in range(seg):
            uint[c, tau * SUBLANES:(tau + 1) * SUBLANES, :] = ubuf[c, pl.ds(tau, SUBLANES, stride=pitch), :]

    row = lax.broadcasted_iota(jnp.int32, (SUBLANES, LANES), 0)
    kdec = [[(0.5 * LRU_C) * _softplus(-lam_ref[d:d + 1, ln]) for d in range(2)] for ln in tiles]
    chains = [(c, d) for c in range(ntile) for d in range(2)]

    def row0(blk):
        return blk * rb if isinstance(blk, int) else pl.multiple_of(blk * rb, rb)

    def blk_body(i, carry):
        carry = list(carry)
        for q, (c, d) in enumerate(chains):
            ln = tiles[c]
            h, p = carry[2 * q], carry[2 * q + 1]
            r0 = row0(i if d == 0 else nblk - 1 - i)
            u = uint[c, pl.ds(r0, rb), :]
            ub = u.astype(BF16)
            nla = kdec[c][d] * jnp.tanh(_dot(ub, wa_ref[d, c]) + ba_ref[d:d + 1, ln]) + kdec[c][d]
            gate = jnp.tanh(_dot(ub, wx_ref[d, c]) + bx_ref[d:d + 1, ln]) + 1.0
            a = jnp.exp2(nla * (-math.log2(math.e)))
            v = jnp.tanh(nla) * (1.0 + a * a)
            root = jnp.where(v > 0.0, v * lax.rsqrt(v), 0.0)
            bb = root * (gate * u)
            hs = [None] * nv
            ps = [None] * nv
            order = range(nv) if d == 0 else range(nv - 1, -1, -1)
            for k in order:
                ak = a[k * SUBLANES:(k + 1) * SUBLANES, :]
                h = ak * h + bb[k * SUBLANES:(k + 1) * SUBLANES, :]
                p = ak * p
                hs[k] = h
                ps[k] = p
            hbuf[q, pl.ds(r0, rb), :] = jnp.concatenate(hs, axis=0)
            pbuf[q, pl.ds(r0, rb), :] = jnp.concatenate(ps, axis=0)
            carry[2 * q], carry[2 * q + 1] = h, p
        return tuple(carry)

    def loop(body, init):
        if straight_line:
            for i in range(nblk):
                init = body(i, init)
            return init
        return lax.fori_loop(0, nblk, body, init)

    hp = loop(blk_body, (zeros8, ones8) * len(chains))
    finals = []
    cmats = []
    for q, (c, d) in enumerate(chains):
        h, p = hp[2 * q], hp[2 * q + 1]
        st = h0[q]
        cmat = zeros8
        order = range(SUBLANES) if d == 0 else range(SUBLANES - 1, -1, -1)
        for s in order:
            cmat = jnp.where(row == s, jnp.broadcast_to(st, (SUBLANES, LANES)), cmat)
            st = h[s:s + 1, :] + p[s:s + 1, :] * st
        finals.append(st)
        cmats.append(cmat)

    if need_out:
        def out_body(i, carry):
            r0 = row0(i)
            for c in range(ntile):
                y = None
                for q in (2 * c, 2 * c + 1):
                    t1 = hbuf[q, pl.ds(r0, rb), :].reshape(nv, SUBLANES, LANES)
                    t2 = pbuf[q, pl.ds(r0, rb), :].reshape(nv, SUBLANES, LANES) * cmats[q][None]
                    y = t1 + t2 if y is None else y + t1 + t2
                ybuf[c, pl.ds(r0, rb), :] = y.reshape(rb, LANES)
            return carry

        loop(out_body, 0)
        for c in range(ntile):
            for tau in range(seg):
                ubuf[c, pl.ds(tau, SUBLANES, stride=pitch), :] = ybuf[c, tau * SUBLANES:(tau + 1) * SUBLANES, :]
            for s in range(SUBLANES):
                write_out(c, s, ubuf[c, s * pitch:s * pitch + seg, :])
    return finals


def _lru_scratch(t, ntile):
    pitch_rows = SUBLANES * (t // SUBLANES + 4) + SUBLANES
    return [
        pltpu.VMEM((ntile, t + 2 * SUBLANES, LANES), F32),
        pltpu.VMEM((ntile, pitch_rows, LANES), F32),
        pltpu.VMEM((ntile, t, LANES), F32),
        pltpu.VMEM((2 * ntile, t, LANES), F32),
        pltpu.VMEM((2 * ntile, t, LANES), F32),
        pltpu.VMEM((ntile, t, LANES), F32),
    ]


def _lru_ctx_kernel(*args, t_ctx, need_ctx):
    if need_ctx:
        (uc_ref, *refs, hf_ref, yc_ref, upad, ubuf, uint, hbuf, pbuf, ybuf) = args
    else:
        (uc_ref, *refs, hf_ref, upad, ubuf, uint, hbuf, pbuf, ybuf) = args
    seg = t_ctx // SUBLANES

    def write_out(c, s, rows):
        yc_ref[0, s * seg:(s + 1) * seg, c * LANES:(c + 1) * LANES] = rows

    zero = jnp.zeros((1, LANES), F32)
    finals = _lru_seq(t_ctx, uc_ref.at[0], write_out if need_ctx else None, (zero,) * (2 * LRU_TILES), refs,
                      (upad, ubuf, uint, hbuf, pbuf, ybuf), ntile=LRU_TILES)
    for q, st in enumerate(finals):
        c, d = divmod(q, 2)
        hf_ref[0, d:d + 1, c * LANES:(c + 1) * LANES] = st


def _lru_ctx(lx_c, conv_w, conv_b, wa_bd, wx_bd, ba, bx, lam, *, need_ctx):
    b, t_ctx, w = lx_c.shape
    cwid = LRU_TILES * LANES
    seq = lambda t: pl.BlockSpec((1, t, cwid), lambda i, j: (i, 0, j))
    par = lambda r: pl.BlockSpec((r, cwid), lambda i, j: (0, j))
    gate = pl.BlockSpec((2, LRU_TILES, LANES, LANES), lambda i, j: (0, j, 0, 0))
    kern = functools.partial(_lru_ctx_kernel, t_ctx=t_ctx, need_ctx=need_ctx)
    out_shape = [jax.ShapeDtypeStruct((b, 2, w), F32)]
    out_specs = [seq(2)]
    if need_ctx:
        out_shape.append(jax.ShapeDtypeStruct((b, t_ctx, w), F32))
        out_specs.append(seq(t_ctx))
    res = pl.pallas_call(
        kern,
        out_shape=tuple(out_shape),
        grid=(b, w // cwid),
        in_specs=[seq(t_ctx), par(CONV_K), par(1), gate, gate, par(2), par(2), par(2)],
        out_specs=tuple(out_specs),
        scratch_shapes=_lru_scratch(t_ctx, LRU_TILES),
        compiler_params=_cparams("arbitrary", "arbitrary"),
        name="rglru_ctx",
    )(lx_c, conv_w, conv_b, wa_bd, wx_bd, ba, bx, lam)
    return (res[0], res[1]) if need_ctx else (res[0], None)


def _ssd_kernel(*args, t_lat, t_ctx, need_ctx):
    if need_ctx:
        (xl_hbm, xc_hbm, cw_ref, cb_ref, dtb_ref, alog_ref, dsk_ref, yl_hbm, yc_ref,
         xg, ybuf, sem_in, sem_out, stg, xm, scc, sbt, cumb, srct, etb, sbuf, hst, yacc) = args
    else:
        (xl_hbm, xc_hbm, cw_ref, cb_ref, dtb_ref, alog_ref, dsk_ref, yl_hbm,
         xg, ybuf, sem_in, sem_out, stg, xm, scc, sbt, cumb, srct, etb, sbuf, hst, yacc) = args
        yc_ref = None
    ck = SSD_CHUNK
    rows = t_lat // GRID_W
    nc_ctx = t_ctx // ck
    nc_lat = t_lat // ck
    nx = SSD_HEADS * SSD_HEAD_DIM // LANES
    ng = SSD_GROUPS
    nslab = nx + 2 * ng
    halo = SUBLANES

    ri = lax.broadcasted_iota(jnp.int32, (ck, ck), 0)
    ci = lax.broadcasted_iota(jnp.int32, (ck, ck), 1)
    lane_lo = ci < SSD_HEAD_DIM

    b = pl.program_id(0)
    nb = pl.num_programs(0)
    slot = lax.rem(b, 2)
    ctx0 = halo
    lat0 = ctx0 + t_ctx + halo
    width = xg.shape[-1]

    def in_copies(bi, sl):
        cps = [pltpu.make_async_copy(xc_hbm.at[bi], xg.at[sl, pl.ds(ctx0, t_ctx), :], sem_in.at[sl])]
        for w in range(GRID_W):
            cps.append(pltpu.make_async_copy(
                xl_hbm.at[bi, :, w, :], xg.at[sl, pl.ds(lat0 + w * rows, rows), :], sem_in.at[sl]))
        return cps

    def out_copies(bi):
        return [pltpu.make_async_copy(ybuf.at[pl.ds(w * rows, rows), :], yl_hbm.at[bi, :, w, :], sem_out.at[0])
                for w in range(GRID_W)]

    @pl.when(b == 0)
    def _():
        zpad = jnp.zeros((halo, width), F32)
        for sl in range(2):
            for r in (0, ctx0 + t_ctx, lat0 + t_lat):
                xg[sl, r:r + halo, :] = zpad
        for cp in in_copies(0, 0):
            cp.start()

    @pl.when(b + 1 < nb)
    def _():
        for cp in in_copies(b + 1, 1 - slot):
            cp.start()

    for cp in in_copies(b, slot):
        cp.wait()

    xs = xg.at[slot]

    def src_row(k):
        return pl.multiple_of(jnp.where(k < nc_ctx, ctx0 + k * ck, lat0 + (k - nc_ctx) * ck), SUBLANES)

    def conv_silu(src0, j):
        stg[j] = xs[pl.ds(src0 - halo, ck + 2 * halo), j * LANES:(j + 1) * LANES]
        cw = cw_ref[j]
        half = cb_ref[j]
        for tap in range(CONV_K):
            half = half + cw[tap:tap + 1, :] * stg[j, halo + tap - 1:halo + tap - 1 + ck, :]
        return half * jnp.tanh(half) + half

    dtb = dtb_ref[...]
    lane_ok = lax.broadcasted_iota(jnp.int32, (1, LANES), 1) < 2 * SSD_HEADS
    nega2 = jnp.where(lane_ok, -jnp.exp(alog_ref[...]) * math.log2(math.e), 0.0)

    nck = nc_ctx + nc_lat
    nh2 = 2 * SSD_HEADS
    hpp = SSD_HEADS // ng // 2
    dir1_col = lax.broadcasted_iota(jnp.int32, (1, LANES), 1) >= SSD_HEADS
    tris = (ri >= ci, ri <= ci)

    def xcat_of(j, r0):
        return jnp.concatenate([xm[j, pl.ds(r0, ck), :], xm[nx + j, pl.ds(r0, ck), :]], axis=0)

    def pass1(kc, carry):
        r0 = pl.multiple_of(kc * ck, ck)
        src0 = src_row(kc)
        dt = _softplus(xs[pl.ds(src0, ck), nslab * LANES:(nslab + 1) * LANES] + dtb)
        la = dt * nega2
        cumf = la
        for sh in (1, 2, 4):
            cumf = cumf + jnp.where(ri >= sh, pltpu.roll(cumf, sh, 0), 0.0)
        for sh in (8, 16, 32, 64):
            cumf = cumf + jnp.concatenate([jnp.zeros((sh, LANES), F32), cumf[0:ck - sh, :]], axis=0)
        tot = cumf[ck - 1:ck, :]
        cum = jnp.where(dir1_col, tot - cumf + la, cumf)
        ldt = jnp.log2(dt)
        cumb[pl.ds(r0, ck), :] = cum
        srct[pl.ds(pl.multiple_of(kc * nh2, nh2), nh2), :] = (cum - ldt).T[0:nh2, :]
        wgt_t = jnp.exp2((tot - cum + ldt).T[0:nh2, :])
        etb[pl.ds(pl.multiple_of(kc * SUBLANES, SUBLANES), SUBLANES), :] = jnp.broadcast_to(
            jnp.exp2(tot), (SUBLANES, LANES))
        for g in range(ng):
            scc[g, pl.ds(r0, ck), :] = conv_silu(src0, nx + ng + g).astype(BF16)
            btf = conv_silu(src0, nx + g).T
            sbt[g, pl.ds(r0, ck), :] = btf.astype(BF16)
            for pr in range(hpp):
                j = g * hpp + pr
                x = conv_silu(src0, j)
                yacc[j, pl.ds(r0, ck), :] = dsk_ref[j] * x
                xb = x.astype(BF16)
                zb = jnp.zeros_like(xb)
                xlo = jnp.where(lane_lo, xb, zb)
                xhi = jnp.where(lane_lo, zb, xb)
                xm[j, pl.ds(r0, ck), :] = xlo
                xm[nx + j, pl.ds(r0, ck), :] = xhi
                btw = []
                for d in range(2):
                    hc = (d * SSD_HEADS + 2 * j, d * SSD_HEADS + 2 * j + 1)
                    btw.append(jnp.concatenate([(btf * wgt_t[h:h + 1, :]).astype(BF16) for h in hc], axis=1))
                s = _dot(jnp.concatenate(btw, axis=0), jnp.concatenate([xlo, xhi], axis=0)).astype(BF16)
                sbuf[j, pl.ds(r0, ck), :] = s[0:SSD_STATE, :]
                sbuf[nx + j, pl.ds(r0, ck), :] = s[SSD_STATE:2 * SSD_STATE, :]
        return carry

    lax.fori_loop(0, nck, pass1, 0, unroll=2)

    hst[...] = jnp.zeros(hst.shape, F32)

    def pass2(i, carry):
        for d in range(2):
            if d == 0:
                kc = i
            else:
                kc = jnp.where(i < nc_ctx, nc_ctx - 1 - i, nck - 1 - (i - nc_ctx))
            r0 = pl.multiple_of(kc * ck, ck)
            e = etb[pl.ds(pl.multiple_of(kc * SUBLANES, SUBLANES), 1), :]
            for j in range(nx):
                hc = (d * SSD_HEADS + 2 * j, d * SSD_HEADS + 2 * j + 1)
                et = jnp.where(lane_lo[0:1, :], jnp.broadcast_to(e[:, hc[0]:hc[0] + 1], (1, LANES)),
                               jnp.broadcast_to(e[:, hc[1]:hc[1] + 1], (1, LANES)))
                h = hst[d * nx + j]
                s = sbuf[d * nx + j, pl.ds(r0, ck), :].astype(F32)
                sbuf[d * nx + j, pl.ds(r0, ck), :] = h.astype(BF16)
                hst[d * nx + j] = h * et + s
        return carry

    lax.fori_loop(0, nck, pass2, 0)

    def pass3(kc, is_ctx):
        r0 = kc * ck if isinstance(kc, int) else pl.multiple_of(kc * ck, ck)
        s0 = kc * nh2 if isinstance(kc, int) else pl.multiple_of(kc * nh2, nh2)
        cum = cumb[pl.ds(r0, ck), :]
        src_t = srct[pl.ds(s0, nh2), :]
        for g in range(ng):
            cm = scc[g, pl.ds(r0, ck), :]
            cb = _dot(cm, sbt[g, pl.ds(r0, ck), :])
            for pr in range(hpp):
                j = g * hpp + pr
                dec = []
                ecs = []
                for d in range(2):
                    hc = (d * SSD_HEADS + 2 * j, d * SSD_HEADS + 2 * j + 1)
                    col = [jnp.broadcast_to(cum[:, h:h + 1], (ck, LANES)) for h in hc]
                    dec.append([jnp.exp2(jnp.where(tris[d], col[q] - src_t[hc[q]:hc[q] + 1, :], NEG_BIG))
                                for q in range(2)])
                    ecs.append(jnp.exp2(jnp.where(lane_lo, col[0], col[1])))
                ms = [(cb * (dec[0][q] + dec[1][q])).astype(BF16) for q in range(2)]
                y = yacc[j, pl.ds(r0, ck), :] + _dot(jnp.concatenate(ms, axis=1), xcat_of(j, r0))
                hin = jnp.concatenate([sbuf[j, pl.ds(r0, ck), :], sbuf[nx + j, pl.ds(r0, ck), :]], axis=1)
                yoff = _dot(cm, hin)
                y = y + yoff[:, 0:LANES] * ecs[0] + yoff[:, LANES:2 * LANES] * ecs[1]
                if is_ctx:
                    yc_ref[0, pl.ds(r0, ck), j * LANES:(j + 1) * LANES] = y
                else:
                    y0 = pl.multiple_of((kc - nc_ctx) * ck, ck)
                    ybuf[pl.ds(y0, ck), j * LANES:(j + 1) * LANES] = y

    if need_ctx:
        for kc in range(nc_ctx):
            pass3(kc, True)

    @pl.when(b > 0)
    def _():
        for cp in out_copies(b - 1):
            cp.wait()

    def lat_body(c, carry):
        pass3(nc_ctx + c, False)
        return carry

    lax.fori_loop(0, nc_lat, lat_body, 0, unroll=2)

    for cp in out_copies(b):
        cp.start()

    @pl.when(b == nb - 1)
    def _():
        for cp in out_copies(b):
            cp.wait()


def _ssd(xd_l, xd_c, cw, cb, dtb, alog, dsk, *, need_ctx):
    b, t_lat, width = xd_l.shape
    t_ctx = xd_c.shape[1]
    nx = SSD_HEADS * SSD_HEAD_DIM // LANES
    ng = SSD_GROUPS
    inner = nx * LANES
    tot = t_lat + t_ctx
    rows = t_lat // GRID_W
    kern = functools.partial(_ssd_kernel, t_lat=t_lat, t_ctx=t_ctx, need_ctx=need_ctx)
    full = lambda a: pl.BlockSpec(a.shape, lambda i: (0,) * a.ndim)
    hbm = pl.BlockSpec(memory_space=pl.ANY)
    out_shape = [jax.ShapeDtypeStruct((b, rows, GRID_W, inner), F32)]
    out_specs = [hbm]
    if need_ctx:
        out_shape.append(jax.ShapeDtypeStruct((b, t_ctx, inner), F32))
        out_specs.append(pl.BlockSpec((1, t_ctx, inner), lambda i: (i, 0, 0)))
    res = pl.pallas_call(
        kern,
        out_shape=tuple(out_shape),
        grid=(b,),
        in_specs=[hbm, hbm, full(cw), full(cb), full(dtb), full(alog), full(dsk)],
        out_specs=tuple(out_specs),
        scratch_shapes=[
            pltpu.VMEM((2, tot + 3 * SUBLANES, width), F32),
            pltpu.VMEM((t_lat, inner), F32),
            pltpu.SemaphoreType.DMA((2,)),
            pltpu.SemaphoreType.DMA((1,)),
            pltpu.VMEM((nx + 2 * ng, SSD_CHUNK + 2 * SUBLANES, LANES), F32),
            pltpu.VMEM((2 * nx, tot, LANES), BF16),
            pltpu.VMEM((ng, tot, LANES), BF16),
            pltpu.VMEM((ng, tot, LANES), BF16),
            pltpu.VMEM((tot, LANES), F32),
            pltpu.VMEM((tot // SSD_CHUNK * 2 * SSD_HEADS, LANES), F32),
            pltpu.VMEM((tot // SSD_CHUNK * SUBLANES, LANES), F32),
            pltpu.VMEM((2 * nx, tot, LANES), BF16),
            pltpu.VMEM((2 * nx, SSD_STATE, LANES), F32),
            pltpu.VMEM((nx, tot, LANES), F32),
        ],
        compiler_params=_cparams("arbitrary"),
        name="ssd",
    )(xd_l.reshape(b, rows, GRID_W, width), xd_c, cw, cb, dtb, alog, dsk)
    y_l = res[0].reshape(b, t_lat, inner)
    return (y_l, res[1]) if need_ctx else (y_l, None)


MLP_FF_CHUNK = 1024


def _outmlp_body(lru_raw, refs, fg_ref, o_ref):
    x_ref, lg_ref, ssd_ref, z_ref, mod_ref, sg_ref, n2_ref, wo_ref, w1_ref, w2_ref = refs
    lw = z_ref.shape[-1]
    gw = lw // SSD_GROUPS
    dff = w1_ref.shape[1]
    parts = []
    for g in range(SSD_GROUPS):
        y = ssd_ref[0, :, g * gw:(g + 1) * gw] * _silu(z_ref[0, :, g * gw:(g + 1) * gw])
        y = y * lax.rsqrt(jnp.mean(y * y, axis=-1, keepdims=True) + EPS)
        parts.append(y * sg_ref[:, g * gw:(g + 1) * gw])
    ssd_n = jnp.concatenate(parts, axis=1).astype(BF16)
    lru = (lru_raw * _gelu_tanh(lg_ref[0])).astype(BF16)
    y = _dot(lru, wo_ref[0:lw, :]) + _dot(ssd_n, wo_ref[lw:2 * lw, :])
    x1 = x_ref[0] + mod_ref[0, 2:3, :] * y
    h = (_rmsnorm(x1, n2_ref[...]) * (1.0 + mod_ref[0, 4:5, :]) + mod_ref[0, 3:4, :]).astype(BF16)
    acc = jnp.zeros_like(x1)
    for j in range(dff // MLP_FF_CHUNK):
        a = _dot(h, w1_ref[:, j * MLP_FF_CHUNK:(j + 1) * MLP_FF_CHUNK])
        a = jnp.square(jnp.maximum(a, 0.0)).astype(BF16)
        acc = acc + _dot(a, w2_ref[j * MLP_FF_CHUNK:(j + 1) * MLP_FF_CHUNK, :])
    x2 = x1 + mod_ref[0, 5:6, :] * acc
    if fg_ref is not None:
        x2 = _rmsnorm(x2, fg_ref[...])
    o_ref[0] = x2


def _outmlp_kernel(*args, final_norm):
    x_ref, lru_ref, *rest = args
    if final_norm:
        *refs, fg_ref, o_ref = rest
    else:
        *refs, o_ref = rest
        fg_ref = None
    _outmlp_body(lru_ref[0], (x_ref, *refs), fg_ref, o_ref)


def _outmlp(x, lru, lg, ssd, z, mod, mod_row, sg, n2, wo, w1, w2, layer, fg, *, tm):
    b, t, d = x.shape
    lw = lru.shape[-1]
    final_norm = fg is not None
    kern = functools.partial(_outmlp_kernel, final_norm=final_norm)
    tok = lambda n: pl.BlockSpec((1, tm, n), lambda i, j: (i, j, 0))
    res = lambda a: pl.BlockSpec((None,) + a.shape[1:], lambda i, j: (layer, 0, 0), pipeline_mode=pl.Buffered(1))
    row = lambda n: pl.BlockSpec((1, n), lambda i, j: (0, 0))
    ins = [x, lru, lg, ssd, z, mod, sg, n2, wo, w1, w2]
    in_specs = [tok(d), tok(lw), tok(lw), tok(lw), tok(lw),
                pl.BlockSpec((1, N_MOD, d), lambda i, j: (mod_row(i), 0, 0)),
                row(lw), row(d), res(wo), res(w1), res(w2)]
    if final_norm:
        ins.append(fg)
        in_specs.append(row(d))
    return pl.pallas_call(
        kern,
        out_shape=jax.ShapeDtypeStruct((b, t, d), F32),
        grid=(b, t // tm),
        in_specs=in_specs,
        out_specs=tok(d),
        compiler_params=_cparams("arbitrary", "arbitrary"),
        name="outmlp",
    )(*ins)


def _lru_lat_kernel(lx_ref, h0_ref, *rest, t_lat):
    *refs, y_ref, upad, ubuf, uint, hbuf, pbuf, ybuf = rest
    seg = t_lat // SUBLANES

    def write_out(c, s, rows):
        y_ref[0, s * seg:(s + 1) * seg, c * LANES:(c + 1) * LANES] = rows

    h0 = [h0_ref[0, d:d + 1, c * LANES:(c + 1) * LANES] for c in range(LRU_TILES) for d in range(2)]
    _lru_seq(t_lat, lx_ref.at[0], write_out, h0, refs,
             (upad, ubuf, uint, hbuf, pbuf, ybuf), ntile=LRU_TILES, straight_line=True)


def _lru_lat(lx, h0, conv_w, conv_b, wa_bd, wx_bd, ba, bx, lam):
    b, t, w = lx.shape
    cwid = LRU_TILES * LANES
    seq = lambda r: pl.BlockSpec((1, r, cwid), lambda i, j: (i, 0, j))
    par = lambda r: pl.BlockSpec((r, cwid), lambda i, j: (0, j))
    gate = pl.BlockSpec((2, LRU_TILES, LANES, LANES), lambda i, j: (0, j, 0, 0))
    return pl.pallas_call(
        functools.partial(_lru_lat_kernel, t_lat=t),
        out_shape=jax.ShapeDtypeStruct((b, t, w), F32),
        grid=(b, w // cwid),
        in_specs=[seq(t), seq(2), par(CONV_K), par(1), gate, gate, par(2), par(2), par(2)],
        out_specs=seq(t),
        scratch_shapes=_lru_scratch(t, LRU_TILES),
        compiler_params=_cparams("arbitrary", "arbitrary"),
        name="rglru_lat",
    )(lx, h0, conv_w, conv_b, wa_bd, wx_bd, ba, bx, lam)


def _pair_block_diag(w):
    two, h, k, _ = w.shape
    w = w.reshape(two, h // 2, 2, k, k)
    z = jnp.zeros_like(w[:, :, 0])
    top = jnp.concatenate([w[:, :, 0], z], axis=-1)
    bot = jnp.concatenate([z, w[:, :, 1]], axis=-1)
    return jnp.concatenate([top, bot], axis=-2)


def _pad_lanes(a):
    return jnp.pad(a, [(0, 0)] * (a.ndim - 1) + [(0, LANES - a.shape[-1])])


def kernel(x, c, ctx, c_ctx, ada_w, ada_b, norm1_g, norm2_g, w_in, lru_conv_w, lru_conv_b, lru_wa, lru_ba,
           lru_wx, lru_bx, lru_lambda, ssd_conv_w, ssd_conv_b, ssd_dt_bias, ssd_a_log, ssd_d, ssd_norm_g,
           w_out, mlp_w1, mlp_w2, final_g):
    bsz, t_lat, d = x.shape
    t_ctx = ctx.shape[1]
    depth = ada_w.shape[0]
    lw = lru_conv_w.shape[-1]
    nxbc = ssd_conv_w.shape[-1]
    ndt = 2 * SSD_HEADS
    nscan = lw + nxbc + ndt
    assert bsz + 1 <= MOD_ROWS and t_lat % (GRID_W * SUBLANES) == 0 and t_ctx % SSD_CHUNK == 0

    cvec = jnp.zeros((MOD_ROWS, d), F32).at[:bsz].set(c).at[bsz].set(c_ctx)
    mod = _modulation(cvec, ada_w, ada_b).reshape(depth, MOD_ROWS, N_MOD, d)

    lat_row = lambda i: i
    ctx_row = lambda i: bsz
    tm_lat = 512
    tm_ctx = t_ctx
    nslab = nxbc // LANES
    nxd = nxbc + LANES

    w_re = jnp.concatenate(
        [w_in[:, :, :lw + nxbc], _pad_lanes(w_in[:, :, lw + nxbc:nscan]), w_in[:, :, nscan:]], axis=2).astype(BF16)
    wo = w_out.astype(BF16)
    w1 = mlp_w1.astype(BF16)
    w2 = mlp_w2.astype(BF16)

    for l in range(depth):
        need_ctx = l < depth - 1
        g1 = norm1_g[l].reshape(1, d)
        lx_l, xd_l, lg_l, z_l = _inproj(x, mod[l], lat_row, g1, w_re, l, lw=lw, nxd=nxd, tm=2 * tm_lat)
        lx_c, xd_c, lg_c, z_c = _inproj(ctx, mod[l], ctx_row, g1, w_re, l, lw=lw, nxd=nxd, tm=tm_ctx)

        lru_par = (0.5 * lru_conv_w[l], 0.5 * lru_conv_b[l].reshape(1, lw),
                   _pair_block_diag(lru_wa[l]).astype(BF16), _pair_block_diag(lru_wx[l]).astype(BF16),
                   0.5 * lru_ba[l], 0.5 * lru_bx[l], lru_lambda[l])
        h0_c, lru_c = _lru_ctx(lx_c, *lru_par, need_ctx=need_ctx)

        cw = 0.5 * ssd_conv_w[l].reshape(CONV_K, nslab, LANES).transpose(1, 0, 2)
        cb = 0.5 * ssd_conv_b[l].reshape(nslab, 1, LANES)
        dtb = _pad_lanes(ssd_dt_bias[l].reshape(1, ndt))
        alog = _pad_lanes(ssd_a_log[l].reshape(1, ndt))
        dsk = jnp.repeat(ssd_d[l], SSD_HEAD_DIM).reshape(lw // LANES, 1, LANES)
        ssd_l, ssd_c = _ssd(xd_l, xd_c, cw, cb, dtb, alog, dsk, need_ctx=need_ctx)

        sg = ssd_norm_g[l].reshape(1, lw)
        n2 = norm2_g[l].reshape(1, d)
        fg = final_g.reshape(1, d) if l == depth - 1 else None
        lru_l = _lru_lat(lx_l, h0_c, *lru_par)
        x = _outmlp(x, lru_l, lg_l, ssd_l, z_l, mod[l], lat_row, sg, n2, wo, w1, w2, l, fg, tm=tm_lat)
        if need_ctx:
            ctx = _outmlp(ctx, lru_c, lg_c, ssd_c, z_c, mod[l], ctx_row, sg, n2, wo, w1, w2, l, None,
                          tm=tm_ctx)
    return x
```

```python
import functools
import math

import jax
import jax.numpy as jnp
from jax import lax
from jax.experimental import pallas as pl
from jax.experimental.pallas import tpu as pltpu

F32 = jnp.float32
BF16 = jnp.bfloat16

EPS = 1e-6
GRID_W = 64
LRU_C = 8.0
SSD_HEAD_DIM = 64
SSD_HEADS = 8
SSD_GROUPS = 2
SSD_STATE = 128
SSD_CHUNK = 128
CONV_K = 4
N_MOD = 6
LANES = 128
SUBLANES = 8
MOD_ROWS = 16
VMEM_LIMIT = 56 * 1024 * 1024
NEG_BIG = -1e30
MOD_COLS = 1536
INPROJ_ROWS = 1024
MLP_ROWS = 512


def _sigmoid(x):
    return 0.5 * (jnp.tanh(0.5 * x) + 1.0)


def _silu(x):
    return x * _sigmoid(x)


def _softplus(x):
    return jnp.maximum(x, 0.0) + jnp.log1p(jnp.exp(-jnp.abs(x)))


def _gelu_tanh(x):
    c = math.sqrt(2.0 / math.pi)
    return (0.5 * x) * (1.0 + jnp.tanh(x * (c + (c * 0.044715) * (x * x))))


def _rmsnorm(x, g):
    return x * lax.rsqrt(jnp.mean(x * x, axis=-1, keepdims=True) + EPS) * g


def _dot(a, b):
    return jnp.dot(a, b, preferred_element_type=F32)


def _cparams(*sem):
    return pltpu.CompilerParams(dimension_semantics=sem, vmem_limit_bytes=VMEM_LIMIT)


def _mod_kernel(c_ref, w_ref, b_ref, o_ref):
    s = _silu(c_ref[...]).astype(BF16)
    o_ref[0] = _dot(s, w_ref[0].astype(BF16)) + b_ref[0]


def _modulation(cvec, ada_w, ada_b):
    depth, d, n = ada_w.shape
    tn = MOD_COLS
    return pl.pallas_call(
        _mod_kernel,
        out_shape=jax.ShapeDtypeStruct((depth, MOD_ROWS, n), F32),
        grid=(depth, n // tn),
        in_specs=[
            pl.BlockSpec((MOD_ROWS, d), lambda l, j: (0, 0)),
            pl.BlockSpec((1, d, tn), lambda l, j: (l, 0, j)),
            pl.BlockSpec((1, 1, tn), lambda l, j: (l, 0, j)),
        ],
        out_specs=pl.BlockSpec((1, MOD_ROWS, tn), lambda l, j: (l, 0, j)),
        compiler_params=_cparams("arbitrary", "arbitrary"),
        name="modulation",
    )(cvec, ada_w, ada_b.reshape(depth, 1, n))


def _inproj_kernel(x_ref, mod_ref, g_ref, w_ref, lx_ref, xd_ref, lg_ref, z_ref, *, lw, nxd):
    x = x_ref[0]
    h = _rmsnorm(x, g_ref[...]) * (1.0 + mod_ref[0, 1:2, :]) + mod_ref[0, 0:1, :]
    h = h.astype(BF16)
    lx_ref[0] = _dot(h, w_ref[:, 0:lw])
    xd_ref[0] = _dot(h, w_ref[:, lw:lw + nxd])
    o = lw + nxd
    lg_ref[0] = _dot(h, w_ref[:, o:o + lw])
    z_ref[0] = _dot(h, w_ref[:, o + lw:o + 2 * lw])


def _inproj(x, mod, mod_row, g, w, layer, *, lw, nxd, tm):
    b, t, d = x.shape
    kern = functools.partial(_inproj_kernel, lw=lw, nxd=nxd)
    tok = lambda n: pl.BlockSpec((1, tm, n), lambda i, j: (i, j, 0))
    return pl.pallas_call(
        kern,
        out_shape=(
            jax.ShapeDtypeStruct((b, t, lw), F32),
            jax.ShapeDtypeStruct((b, t, nxd), F32),
            jax.ShapeDtypeStruct((b, t, lw), F32),
            jax.ShapeDtypeStruct((b, t, lw), F32),
        ),
        grid=(b, t // tm),
        in_specs=[
            tok(d),
            pl.BlockSpec((1, N_MOD, d), lambda i, j: (mod_row(i), 0, 0)),
            pl.BlockSpec((1, d), lambda i, j: (0, 0)),
            pl.BlockSpec((None,) + w.shape[1:], lambda i, j: (layer, 0, 0), pipeline_mode=pl.Buffered(1)),
        ],
        out_specs=(tok(lw), tok(nxd), tok(lw), tok(lw)),
        compiler_params=_cparams("arbitrary", "arbitrary"),
        name="inproj",
    )(x, mod, g, w)


LRU_ROWS = 256
LRU_TILES = 2


def _lru_seq(t, u_ref, write_out, h0, refs, scr, *, ntile, straight_line=False):
    cw_ref, cb_ref, wa_ref, wx_ref, ba_ref, bx_ref, lam_ref = refs
    upad, ubuf, uint, hbuf, pbuf, ybuf = scr
    seg = t // SUBLANES
    pitch = seg + 4
    nblk = max(t // LRU_ROWS, 1)
    rb = t // nblk
    nv = rb // SUBLANES
    tiles = [slice(c * LANES, (c + 1) * LANES) for c in range(ntile)]
    need_out = write_out is not None

    zeros8 = jnp.zeros((SUBLANES, LANES), F32)
    ones8 = jnp.ones((SUBLANES, LANES), F32)
    for c, ln in enumerate(tiles):
        upad[c, 0:SUBLANES, :] = zeros8
        upad[c, SUBLANES:SUBLANES + t, :] = u_ref[:, ln]
        upad[c, SUBLANES + t:2 * SUBLANES + t, :] = zeros8
        cw = cw_ref[:, ln]
        for s in range(SUBLANES):
            acc = cb_ref[:, ln]
            for tap in range(CONV_K):
                r = SUBLANES + s * seg + tap - 1
                acc = acc + cw[tap:tap + 1, :] * upad[c, r:r + seg, :]
            ubuf[c, s * pitch:s * pitch + seg, :] = acc
        for tau in range(seg):
            uint[c, tau * SUBLANES:(tau + 1) * SUBLANES, :] = ubuf[c, pl.ds(tau, SUBLANES, stride=pitch), :]

    row = lax.broadcasted_iota(jnp.int32, (SUBLANES, LANES), 0)
    kdec = [[(0.5 * LRU_C) * _softplus(-lam_ref[d:d + 1, ln]) for d in range(2)] for ln in tiles]
    chains = [(c, d) for c in range(ntile) for d in range(2)]

    def row0(blk):
        return blk * rb if isinstance(blk, int) else pl.multiple_of(blk * rb, rb)

    def blk_body(i, carry):
        carry = list(carry)
        for q, (c, d) in enumerate(chains):
            ln = tiles[c]
            h, p = carry[2 * q], carry[2 * q + 1]
            r0 = row0(i if d == 0 else nblk - 1 - i)
            u = uint[c, pl.ds(r0, rb), :]
            ub = u.astype(BF16)
            nla = kdec[c][d] * jnp.tanh(_dot(ub, wa_ref[d, c]) + ba_ref[d:d + 1, ln]) + kdec[c][d]
            gate = jnp.tanh(_dot(ub, wx_ref[d, c]) + bx_ref[d:d + 1, ln]) + 1.0
            a = jnp.exp2(nla * (-math.log2(math.e)))
            v = jnp.tanh(nla) * (1.0 + a * a)
            root = jnp.where(v > 0.0, v * lax.rsqrt(v), 0.0)
            bb = root * (gate * u)
            hs = [None] * nv
            ps = [None] * nv
            order = range(nv) if d == 0 else range(nv - 1, -1, -1)
            for k in order:
                ak = a[k * SUBLANES:(k + 1) * SUBLANES, :]
                h = ak * h + bb[k * SUBLANES:(k + 1) * SUBLANES, :]
                p = ak * p
                hs[k] = h
                ps[k] = p
            hbuf[q, pl.ds(r0, rb), :] = jnp.concatenate(hs, axis=0)
            pbuf[q, pl.ds(r0, rb), :] = jnp.concatenate(ps, axis=0)
            carry[2 * q], carry[2 * q + 1] = h, p
        return tuple(carry)

    def loop(body, init):
        if straight_line:
            for i in range(nblk):
                init = body(i, init)
            return init
        return lax.fori_loop(0, nblk, body, init)

    hp = loop(blk_body, (zeros8, ones8) * len(chains))
    finals = []
    cmats = []
    for q, (c, d) in enumerate(chains):
        h, p = hp[2 * q], hp[2 * q + 1]
        st = h0[q]
        cmat = zeros8
        order = range(SUBLANES) if d == 0 else range(SUBLANES - 1, -1, -1)
        for s in order:
            cmat = jnp.where(row == s, jnp.broadcast_to(st, (SUBLANES, LANES)), cmat)
            st = h[s:s + 1, :] + p[s:s + 1, :] * st
        finals.append(st)
        cmats.append(cmat)

    if need_out:
        def out_body(i, carry):
            r0 = row0(i)
            for c in range(ntile):
                y = None
                for q in (2 * c, 2 * c + 1):
                    t1 = hbuf[q, pl.ds(r0, rb), :].reshape(nv, SUBLANES, LANES)
                    t2 = pbuf[q, pl.ds(r0, rb), :].reshape(nv, SUBLANES, LANES) * cmats[q][None]
                    y = t1 + t2 if y is None else y + t1 + t2
                ybuf[c, pl.ds(r0, rb), :] = y.reshape(rb, LANES)
            return carry

        loop(out_body, 0)
        for c in range(ntile):
            for tau in range(seg):
                ubuf[c, pl.ds(tau, SUBLANES, stride=pitch), :] = ybuf[c, tau * SUBLANES:(tau + 1) * SUBLANES, :]
            for s in range(SUBLANES):
                write_out(c, s, ubuf[c, s * pitch:s * pitch + seg, :])
    return finals


def _lru_scratch(t, ntile):
    pitch_rows = SUBLANES * (t // SUBLANES + 4) + SUBLANES
    return [
        pltpu.VMEM((ntile, t + 2 * SUBLANES, LANES), F32),
        pltpu.VMEM((ntile, pitch_rows, LANES), F32),
        pltpu.VMEM((ntile, t, LANES), F32),
        pltpu.VMEM((2 * ntile, t, LANES), F32),
        pltpu.VMEM((2 * ntile, t, LANES), F32),
        pltpu.VMEM((ntile, t, LANES), F32),
    ]


def _lru_ctx_kernel(*args, t_ctx, need_ctx):
    if need_ctx:
        (uc_ref, *refs, hf_ref, yc_ref, upad, ubuf, uint, hbuf, pbuf, ybuf) = args
    else:
        (uc_ref, *refs, hf_ref, upad, ubuf, uint, hbuf, pbuf, ybuf) = args
    seg = t_ctx // SUBLANES

    def write_out(c, s, rows):
        yc_ref[0, s * seg:(s + 1) * seg, c * LANES:(c + 1) * LANES] = rows

    zero = jnp.zeros((1, LANES), F32)
    finals = _lru_seq(t_ctx, uc_ref.at[0], write_out if need_ctx else None, (zero,) * (2 * LRU_TILES), refs,
                      (upad, ubuf, uint, hbuf, pbuf, ybuf), ntile=LRU_TILES)
    for q, st in enumerate(finals):
        c, d = divmod(q, 2)
        hf_ref[0, d:d + 1, c * LANES:(c + 1) * LANES] = st


def _lru_ctx(lx_c, conv_w, conv_b, wa_bd, wx_bd, ba, bx, lam, *, need_ctx):
    b, t_ctx, w = lx_c.shape
    cwid = LRU_TILES * LANES
    seq = lambda t: pl.BlockSpec((1, t, cwid), lambda i, j: (i, 0, j))
    par = lambda r: pl.BlockSpec((r, cwid), lambda i, j: (0, j))
    gate = pl.BlockSpec((2, LRU_TILES, LANES, LANES), lambda i, j: (0, j, 0, 0))
    kern = functools.partial(_lru_ctx_kernel, t_ctx=t_ctx, need_ctx=need_ctx)
    out_shape = [jax.ShapeDtypeStruct((b, 2, w), F32)]
    out_specs = [seq(2)]
    if need_ctx:
        out_shape.append(jax.ShapeDtypeStruct((b, t_ctx, w), F32))
        out_specs.append(seq(t_ctx))
    res = pl.pallas_call(
        kern,
        out_shape=tuple(out_shape),
        grid=(b, w // cwid),
        in_specs=[seq(t_ctx), par(CONV_K), par(1), gate, gate, par(2), par(2), par(2)],
        out_specs=tuple(out_specs),
        scratch_shapes=_lru_scratch(t_ctx, LRU_TILES),
        compiler_params=_cparams("arbitrary", "arbitrary"),
        name="rglru_ctx",
    )(lx_c, conv_w, conv_b, wa_bd, wx_bd, ba, bx, lam)
    return (res[0], res[1]) if need_ctx else (res[0], None)


def _ssd_kernel(*args, t_lat, t_ctx, need_ctx):
    if need_ctx:
        (xl_hbm, xc_hbm, cw_ref, cb_ref, dtb_ref, alog_ref, dsk_ref, yl_hbm, yc_ref,
         xg, ybuf, sem_in, sem_out, stg, xm, scc, sbt, cumb, srct, etb, sbuf, hst, yacc) = args
    else:
        (xl_hbm, xc_hbm, cw_ref, cb_ref, dtb_ref, alog_ref, dsk_ref, yl_hbm,
         xg, ybuf, sem_in, sem_out, stg, xm, scc, sbt, cumb, srct, etb, sbuf, hst, yacc) = args
        yc_ref = None
    ck = SSD_CHUNK
    rows = t_lat // GRID_W
    nc_ctx = t_ctx // ck
    nc_lat = t_lat // ck
    nx = SSD_HEADS * SSD_HEAD_DIM // LANES
    ng = SSD_GROUPS
    nslab = nx + 2 * ng
    halo = SUBLANES

    ri = lax.broadcasted_iota(jnp.int32, (ck, ck), 0)
    ci = lax.broadcasted_iota(jnp.int32, (ck, ck), 1)
    lane_lo = ci < SSD_HEAD_DIM

    b = pl.program_id(0)
    nb = pl.num_programs(0)
    slot = lax.rem(b, 2)
    ctx0 = halo
    lat0 = ctx0 + t_ctx + halo
    width = xg.shape[-1]

    def in_copies(bi, sl):
        cps = [pltpu.make_async_copy(xc_hbm.at[bi], xg.at[sl, pl.ds(ctx0, t_ctx), :], sem_in.at[sl])]
        for w in range(GRID_W):
            cps.append(pltpu.make_async_copy(
                xl_hbm.at[bi, :, w, :], xg.at[sl, pl.ds(lat0 + w * rows, rows), :], sem_in.at[sl]))
        return cps

    def out_copies(bi):
        return [pltpu.make_async_copy(ybuf.at[pl.ds(w * rows, rows), :], yl_hbm.at[bi, :, w, :], sem_out.at[0])
                for w in range(GRID_W)]

    @pl.when(b == 0)
    def _():
        zpad = jnp.zeros((halo, width), F32)
        for sl in range(2):
            for r in (0, ctx0 + t_ctx, lat0 + t_lat):
                xg[sl, r:r + halo, :] = zpad
        for cp in in_copies(0, 0):
            cp.start()

    @pl.when(b + 1 < nb)
    def _():
        for cp in in_copies(b + 1, 1 - slot):
            cp.start()

    for cp in in_copies(b, slot):
        cp.wait()

    xs = xg.at[slot]

    def src_row(k):
        return pl.multiple_of(jnp.where(k < nc_ctx, ctx0 + k * ck, lat0 + (k - nc_ctx) * ck), SUBLANES)

    def conv_silu(src0, j):
        stg[j] = xs[pl.ds(src0 - halo, ck + 2 * halo), j * LANES:(j + 1) * LANES]
        cw = cw_ref[j]
        half = cb_ref[j]
        for tap in range(CONV_K):
            half = half + cw[tap:tap + 1, :] * stg[j, halo + tap - 1:halo + tap - 1 + ck, :]
        return half * jnp.tanh(half) + half

    dtb = dtb_ref[...]
    lane_ok = lax.broadcasted_iota(jnp.int32, (1, LANES), 1) < 2 * SSD_HEADS
    nega2 = jnp.where(lane_ok, -jnp.exp(alog_ref[...]) * math.log2(math.e), 0.0)

    nck = nc_ctx + nc_lat
    nh2 = 2 * SSD_HEADS
    hpp = SSD_HEADS // ng // 2
    dir1_col = lax.broadcasted_iota(jnp.int32, (1, LANES), 1) >= SSD_HEADS
    tris = (ri >= ci, ri <= ci)

    def xcat_of(j, r0):
        return jnp.concatenate([xm[j, pl.ds(r0, ck), :], xm[nx + j, pl.ds(r0, ck), :]], axis=0)

    def pass1(kc, carry):
        r0 = pl.multiple_of(kc * ck, ck)
        src0 = src_row(kc)
        dt = _softplus(xs[pl.ds(src0, ck), nslab * LANES:(nslab + 1) * LANES] + dtb)
        la = dt * nega2
        cumf = la
        sh = 1
        while sh < ck:
            if sh < SUBLANES:
                cumf = cumf + jnp.where(ri >= sh, pltpu.roll(cumf, sh, 0), 0.0)
            else:
                cumf = cumf + jnp.concatenate([jnp.zeros((sh, LANES), F32), cumf[0:ck - sh, :]], axis=0)
            sh *= 2
        tot = cumf[ck - 1:ck, :]
        cum = jnp.where(dir1_col, tot - cumf + la, cumf)
        ldt = jnp.log2(dt)
        cumb[pl.ds(r0, ck), :] = cum
        srct[pl.ds(pl.multiple_of(kc * nh2, nh2), nh2), :] = (cum - ldt).T[0:nh2, :]
        wgt_t = jnp.exp2((tot - cum + ldt).T[0:nh2, :])
        etb[pl.ds(pl.multiple_of(kc * SUBLANES, SUBLANES), SUBLANES), :] = jnp.broadcast_to(
            jnp.exp2(tot), (SUBLANES, LANES))
        for g in range(ng):
            scc[g, pl.ds(r0, ck), :] = conv_silu(src0, nx + ng + g).astype(BF16)
            btf = conv_silu(src0, nx + g).T
            sbt[g, pl.ds(r0, ck), :] = btf.astype(BF16)
            for pr in range(hpp):
                j = g * hpp + pr
                x = conv_silu(src0, j)
                yacc[j, pl.ds(r0, ck), :] = dsk_ref[j] * x
                xb = x.astype(BF16)
                zb = jnp.zeros_like(xb)
                xlo = jnp.where(lane_lo, xb, zb)
                xhi = jnp.where(lane_lo, zb, xb)
                xm[j, pl.ds(r0, ck), :] = xlo
                xm[nx + j, pl.ds(r0, ck), :] = xhi
                btw = []
                for d in range(2):
                    hc = (d * SSD_HEADS + 2 * j, d * SSD_HEADS + 2 * j + 1)
                    btw.append(jnp.concatenate([(btf * wgt_t[h:h + 1, :]).astype(BF16) for h in hc], axis=1))
                s = _dot(jnp.concatenate(btw, axis=0), jnp.concatenate([xlo, xhi], axis=0)).astype(BF16)
                sbuf[j, pl.ds(r0, ck), :] = s[0:SSD_STATE, :]
                sbuf[nx + j, pl.ds(r0, ck), :] = s[SSD_STATE:2 * SSD_STATE, :]
        return carry

    lax.fori_loop(0, nck, pass1, 0, unroll=2)

    hst[...] = jnp.zeros(hst.shape, F32)

    def pass2(i, carry):
        for d in range(2):
            if d == 0:
                kc = i
            else:
                kc = jnp.where(i < nc_ctx, nc_ctx - 1 - i, nck - 1 - (i - nc_ctx))
            r0 = pl.multiple_of(kc * ck, ck)
            e = etb[pl.ds(pl.multiple_of(kc * SUBLANES, SUBLANES), 1), :]
            for j in range(nx):
                hc = (d * SSD_HEADS + 2 * j, d * SSD_HEADS + 2 * j + 1)
                et = jnp.where(lane_lo[0:1, :], jnp.broadcast_to(e[:, hc[0]:hc[0] + 1], (1, LANES)),
                               jnp.broadcast_to(e[:, hc[1]:hc[1] + 1], (1, LANES)))
                h = hst[d * nx + j]
                s = sbuf[d * nx + j, pl.ds(r0, ck), :].astype(F32)
                sbuf[d * nx + j, pl.ds(r0, ck), :] = h.astype(BF16)
                hst[d * nx + j] = h * et + s
        return carry

    lax.fori_loop(0, nck, pass2, 0)

    def pass3(kc, is_ctx):
        r0 = kc * ck if isinstance(kc, int) else pl.multiple_of(kc * ck, ck)
        s0 = kc * nh2 if isinstance(kc, int) else pl.multiple_of(kc * nh2, nh2)
        cum = cumb[pl.ds(r0, ck), :]
        src_t = srct[pl.ds(s0, nh2), :]
        for g in range(ng):
            cm = scc[g, pl.ds(r0, ck), :]
            cb = _dot(cm, sbt[g, pl.ds(r0, ck), :])
            for pr in range(hpp):
                j = g * hpp + pr
                dec = []
                ecs = []
                for d in range(2):
                    hc = (d * SSD_HEADS + 2 * j, d * SSD_HEADS + 2 * j + 1)
                    col = [jnp.broadcast_to(cum[:, h:h + 1], (ck, LANES)) for h in hc]
                    dec.append([jnp.exp2(jnp.where(tris[d], col[q] - src_t[hc[q]:hc[q] + 1, :], NEG_BIG))
                                for q in range(2)])
                    ecs.append(jnp.exp2(jnp.where(lane_lo, col[0], col[1])))
                ms = [(cb * (dec[0][q] + dec[1][q])).astype(BF16) for q in range(2)]
                y = yacc[j, pl.ds(r0, ck), :] + _dot(jnp.concatenate(ms, axis=1), xcat_of(j, r0))
                hin = jnp.concatenate([sbuf[j, pl.ds(r0, ck), :], sbuf[nx + j, pl.ds(r0, ck), :]], axis=1)
                yoff = _dot(cm, hin)
                y = y + yoff[:, 0:LANES] * ecs[0] + yoff[:, LANES:2 * LANES] * ecs[1]
                if is_ctx:
                    yc_ref[0, pl.ds(r0, ck), j * LANES:(j + 1) * LANES] = y
                else:
                    y0 = pl.multiple_of((kc - nc_ctx) * ck, ck)
                    ybuf[pl.ds(y0, ck), j * LANES:(j + 1) * LANES] = y

    if need_ctx:
        for kc in range(nc_ctx):
            pass3(kc, True)

    @pl.when(b > 0)
    def _():
        for cp in out_copies(b - 1):
            cp.wait()

    def lat_body(c, carry):
        pass3(nc_ctx + c, False)
        return carry

    lax.fori_loop(0, nc_lat, lat_body, 0, unroll=2)

    for cp in out_copies(b):
        cp.start()

    @pl.when(b == nb - 1)
    def _():
        for cp in out_copies(b):
            cp.wait()


def _ssd(xd_l, xd_c, cw, cb, dtb, alog, dsk, *, need_ctx):
    b, t_lat, width = xd_l.shape
    t_ctx = xd_c.shape[1]
    nx = SSD_HEADS * SSD_HEAD_DIM // LANES
    ng = SSD_GROUPS
    inner = nx * LANES
    tot = t_lat + t_ctx
    rows = t_lat // GRID_W
    kern = functools.partial(_ssd_kernel, t_lat=t_lat, t_ctx=t_ctx, need_ctx=need_ctx)
    full = lambda a: pl.BlockSpec(a.shape, lambda i: (0,) * a.ndim)
    hbm = pl.BlockSpec(memory_space=pl.ANY)
    out_shape = [jax.ShapeDtypeStruct((b, rows, GRID_W, inner), F32)]
    out_specs = [hbm]
    if need_ctx:
        out_shape.append(jax.ShapeDtypeStruct((b, t_ctx, inner), F32))
        out_specs.append(pl.BlockSpec((1, t_ctx, inner), lambda i: (i, 0, 0)))
    res = pl.pallas_call(
        kern,
        out_shape=tuple(out_shape),
        grid=(b,),
        in_specs=[hbm, hbm, full(cw), full(cb), full(dtb), full(alog), full(dsk)],
        out_specs=tuple(out_specs),
        scratch_shapes=[
            pltpu.VMEM((2, tot + 3 * SUBLANES, width), F32),
            pltpu.VMEM((t_lat, inner), F32),
            pltpu.SemaphoreType.DMA((2,)),
            pltpu.SemaphoreType.DMA((1,)),
            pltpu.VMEM((nx + 2 * ng, SSD_CHUNK + 2 * SUBLANES, LANES), F32),
            pltpu.VMEM((2 * nx, tot, LANES), BF16),
            pltpu.VMEM((ng, tot, LANES), BF16),
            pltpu.VMEM((ng, tot, LANES), BF16),
            pltpu.VMEM((tot, LANES), F32),
            pltpu.VMEM((tot // SSD_CHUNK * 2 * SSD_HEADS, LANES), F32),
            pltpu.VMEM((tot // SSD_CHUNK * SUBLANES, LANES), F32),
            pltpu.VMEM((2 * nx, tot, LANES), BF16),
            pltpu.VMEM((2 * nx, SSD_STATE, LANES), F32),
            pltpu.VMEM((nx, tot, LANES), F32),
        ],
        compiler_params=_cparams("arbitrary"),
        name="ssd",
    )(xd_l.reshape(b, rows, GRID_W, width), xd_c, cw, cb, dtb, alog, dsk)
    y_l = res[0].reshape(b, t_lat, inner)
    return (y_l, res[1]) if need_ctx else (y_l, None)


MLP_FF_CHUNK = 1024


def _outmlp_body(lru_raw, refs, fg_ref, o_ref):
    x_ref, lg_ref, ssd_ref, z_ref, mod_ref, sg_ref, n2_ref, wo_ref, w1_ref, w2_ref = refs
    lw = z_ref.shape[-1]
    gw = lw // SSD_GROUPS
    dff = w1_ref.shape[1]
    parts = []
    for g in range(SSD_GROUPS):
        y = ssd_ref[0, :, g * gw:(g + 1) * gw] * _silu(z_ref[0, :, g * gw:(g + 1) * gw])
        y = y * lax.rsqrt(jnp.mean(y * y, axis=-1, keepdims=True) + EPS)
        parts.append(y * sg_ref[:, g * gw:(g + 1) * gw])
    ssd_n = jnp.concatenate(parts, axis=1).astype(BF16)
    lru = (lru_raw * _gelu_tanh(lg_ref[0])).astype(BF16)
    y = _dot(lru, wo_ref[0:lw, :]) + _dot(ssd_n, wo_ref[lw:2 * lw, :])
    x1 = x_ref[0] + mod_ref[0, 2:3, :] * y
    h = (_rmsnorm(x1, n2_ref[...]) * (1.0 + mod_ref[0, 4:5, :]) + mod_ref[0, 3:4, :]).astype(BF16)
    acc = jnp.zeros_like(x1)
    for j in range(dff // MLP_FF_CHUNK):
        a = _dot(h, w1_ref[:, j * MLP_FF_CHUNK:(j + 1) * MLP_FF_CHUNK])
        a = jnp.square(jnp.maximum(a, 0.0)).astype(BF16)
        acc = acc + _dot(a, w2_ref[j * MLP_FF_CHUNK:(j + 1) * MLP_FF_CHUNK, :])
    x2 = x1 + mod_ref[0, 5:6, :] * acc
    if fg_ref is not None:
        x2 = _rmsnorm(x2, fg_ref[...])
    o_ref[0] = x2


def _outmlp_kernel(*args, final_norm):
    x_ref, lru_ref, *rest = args
    if final_norm:
        *refs, fg_ref, o_ref = rest
    else:
        *refs, o_ref = rest
        fg_ref = None
    _outmlp_body(lru_ref[0], (x_ref, *refs), fg_ref, o_ref)


def _outmlp(x, lru, lg, ssd, z, mod, mod_row, sg, n2, wo, w1, w2, layer, fg, *, tm):
    b, t, d = x.shape
    lw = lru.shape[-1]
    final_norm = fg is not None
    kern = functools.partial(_outmlp_kernel, final_norm=final_norm)
    tok = lambda n: pl.BlockSpec((1, tm, n), lambda i, j: (i, j, 0))
    res = lambda a: pl.BlockSpec((None,) + a.shape[1:], lambda i, j: (layer, 0, 0), pipeline_mode=pl.Buffered(1))
    row = lambda n: pl.BlockSpec((1, n), lambda i, j: (0, 0))
    ins = [x, lru, lg, ssd, z, mod, sg, n2, wo, w1, w2]
    in_specs = [tok(d), tok(lw), tok(lw), tok(lw), tok(lw),
                pl.BlockSpec((1, N_MOD, d), lambda i, j: (mod_row(i), 0, 0)),
                row(lw), row(d), res(wo), res(w1), res(w2)]
    if final_norm:
        ins.append(fg)
        in_specs.append(row(d))
    return pl.pallas_call(
        kern,
        out_shape=jax.ShapeDtypeStruct((b, t, d), F32),
        grid=(b, t // tm),
        in_specs=in_specs,
        out_specs=tok(d),
        compiler_params=_cparams("arbitrary", "arbitrary"),
        name="outmlp",
    )(*ins)


def _lru_lat_kernel(lx_ref, h0_ref, *rest, t_lat):
    *refs, y_ref, upad, ubuf, uint, hbuf, pbuf, ybuf = rest
    seg = t_lat // SUBLANES

    def write_out(c, s, rows):
        y_ref[0, s * seg:(s + 1) * seg, c * LANES:(c + 1) * LANES] = rows

    h0 = [h0_ref[0, d:d + 1, c * LANES:(c + 1) * LANES] for c in range(LRU_TILES) for d in range(2)]
    _lru_seq(t_lat, lx_ref.at[0], write_out, h0, refs,
             (upad, ubuf, uint, hbuf, pbuf, ybuf), ntile=LRU_TILES, straight_line=True)


def _lru_lat(lx, h0, conv_w, conv_b, wa_bd, wx_bd, ba, bx, lam):
    b, t, w = lx.shape
    cwid = LRU_TILES * LANES
    seq = lambda r: pl.BlockSpec((1, r, cwid), lambda i, j: (i, 0, j))
    par = lambda r: pl.BlockSpec((r, cwid), lambda i, j: (0, j))
    gate = pl.BlockSpec((2, LRU_TILES, LANES, LANES), lambda i, j: (0, j, 0, 0))
    return pl.pallas_call(
        functools.partial(_lru_lat_kernel, t_lat=t),
        out_shape=jax.ShapeDtypeStruct((b, t, w), F32),
        grid=(b, w // cwid),
        in_specs=[seq(t), seq(2), par(CONV_K), par(1), gate, gate, par(2), par(2), par(2)],
        out_specs=seq(t),
        scratch_shapes=_lru_scratch(t, LRU_TILES),
        compiler_params=_cparams("arbitrary", "arbitrary"),
        name="rglru_lat",
    )(lx, h0, conv_w, conv_b, wa_bd, wx_bd, ba, bx, lam)


def _pair_block_diag(w):
    two, h, k, _ = w.shape
    w = w.reshape(two, h // 2, 2, k, k)
    z = jnp.zeros_like(w[:, :, 0])
    top = jnp.concatenate([w[:, :, 0], z], axis=-1)
    bot = jnp.concatenate([z, w[:, :, 1]], axis=-1)
    return jnp.concatenate([top, bot], axis=-2)


def _pad_lanes(a):
    return jnp.pad(a, [(0, 0)] * (a.ndim - 1) + [(0, LANES - a.shape[-1])])


def kernel(x, c, ctx, c_ctx, ada_w, ada_b, norm1_g, norm2_g, w_in, lru_conv_w, lru_conv_b, lru_wa, lru_ba,
           lru_wx, lru_bx, lru_lambda, ssd_conv_w, ssd_conv_b, ssd_dt_bias, ssd_a_log, ssd_d, ssd_norm_g,
           w_out, mlp_w1, mlp_w2, final_g):
    bsz, t_lat, d = x.shape
    t_ctx = ctx.shape[1]
    depth = ada_w.shape[0]
    lw = lru_conv_w.shape[-1]
    nxbc = ssd_conv_w.shape[-1]
    ndt = 2 * SSD_HEADS
    nscan = lw + nxbc + ndt
    assert bsz + 1 <= MOD_ROWS and t_lat % (GRID_W * SUBLANES) == 0 and t_ctx % SSD_CHUNK == 0

    cvec = jnp.zeros((MOD_ROWS, d), F32).at[:bsz].set(c).at[bsz].set(c_ctx)
    mod = _modulation(cvec, ada_w, ada_b).reshape(depth, MOD_ROWS, N_MOD, d)

    lat_row = lambda i: i
    ctx_row = lambda i: bsz
    tm_ctx = t_ctx
    nslab = nxbc // LANES
    nxd = nxbc + LANES

    w_re = jnp.concatenate(
        [w_in[:, :, :lw + nxbc], _pad_lanes(w_in[:, :, lw + nxbc:nscan]), w_in[:, :, nscan:]], axis=2).astype(BF16)
    wo = w_out.astype(BF16)
    w1 = mlp_w1.astype(BF16)
    w2 = mlp_w2.astype(BF16)

    for l in range(depth):
        need_ctx = l < depth - 1
        g1 = norm1_g[l].reshape(1, d)
        lx_l, xd_l, lg_l, z_l = _inproj(x, mod[l], lat_row, g1, w_re, l, lw=lw, nxd=nxd, tm=INPROJ_ROWS)
        lx_c, xd_c, lg_c, z_c = _inproj(ctx, mod[l], ctx_row, g1, w_re, l, lw=lw, nxd=nxd, tm=tm_ctx)

        lru_par = (0.5 * lru_conv_w[l], 0.5 * lru_conv_b[l].reshape(1, lw),
                   _pair_block_diag(lru_wa[l]).astype(BF16), _pair_block_diag(lru_wx[l]).astype(BF16),
                   0.5 * lru_ba[l], 0.5 * lru_bx[l], lru_lambda[l])
        h0_c, lru_c = _lru_ctx(lx_c, *lru_par, need_ctx=need_ctx)

        cw = 0.5 * ssd_conv_w[l].reshape(CONV_K, nslab, LANES).transpose(1, 0, 2)
        cb = 0.5 * ssd_conv_b[l].reshape(nslab, 1, LANES)
        dtb = _pad_lanes(ssd_dt_bias[l].reshape(1, ndt))
        alog = _pad_lanes(ssd_a_log[l].reshape(1, ndt))
        dsk = jnp.repeat(ssd_d[l], SSD_HEAD_DIM).reshape(lw // LANES, 1, LANES)
        ssd_l, ssd_c = _ssd(xd_l, xd_c, cw, cb, dtb, alog, dsk, need_ctx=need_ctx)

        sg = ssd_norm_g[l].reshape(1, lw)
        n2 = norm2_g[l].reshape(1, d)
        fg = final_g.reshape(1, d) if l == depth - 1 else None
        lru_l = _lru_lat(lx_l, h0_c, *lru_par)
        x = _outmlp(x, lru_l, lg_l, ssd_l, z_l, mod[l], lat_row, sg, n2, wo, w1, w2, l, fg, tm=MLP_ROWS)
        if need_ctx:
            ctx = _outmlp(ctx, lru_c, lg_c, ssd_c, z_c, mod[l], ctx_row, sg, n2, wo, w1, w2, l, None,
                          tm=tm_ctx)
    return x
```

```python
import functools
import math

import jax
import jax.numpy as jnp
from jax import lax
from jax.experimental import pallas as pl
from jax.experimental.pallas import tpu as pltpu

F32 = jnp.float32
BF16 = jnp.bfloat16

EPS = 1e-6
GRID_W = 64
LRU_C = 8.0
SSD_HEAD_DIM = 64
SSD_HEADS = 8
SSD_GROUPS = 2
SSD_STATE = 128
SSD_CHUNK = 128
CONV_K = 4
N_MOD = 6
LANES = 128
SUBLANES = 8
MOD_ROWS = 16
VMEM_LIMIT = 56 * 1024 * 1024
NEG_BIG = -1e30
MOD_COLS = 1536
INPROJ_ROWS = 1024
MLP_ROWS = 512


def _sigmoid(x):
    return 0.5 * (jnp.tanh(0.5 * x) + 1.0)


def _silu(x):
    return x * _sigmoid(x)


def _softplus(x):
    return jnp.maximum(x, 0.0) + jnp.log1p(jnp.exp(-jnp.abs(x)))


def _gelu_tanh(x):
    c = math.sqrt(2.0 / math.pi)
    return (0.5 * x) * (1.0 + jnp.tanh(x * (c + (c * 0.044715) * (x * x))))


def _rmsnorm(x, g):
    return x * lax.rsqrt(jnp.mean(x * x, axis=-1, keepdims=True) + EPS) * g


def _dot(a, b):
    return jnp.dot(a, b, preferred_element_type=F32)


def _cparams(*sem):
    return pltpu.CompilerParams(dimension_semantics=sem, vmem_limit_bytes=VMEM_LIMIT)


def _mod_kernel(c_ref, w_ref, b_ref, o_ref):
    s = _silu(c_ref[...]).astype(BF16)
    o_ref[0] = _dot(s, w_ref[0].astype(BF16)) + b_ref[0]


def _modulation(cvec, ada_w, ada_b):
    depth, d, n = ada_w.shape
    tn = MOD_COLS
    return pl.pallas_call(
        _mod_kernel,
        out_shape=jax.ShapeDtypeStruct((depth, MOD_ROWS, n), F32),
        grid=(depth, n // tn),
        in_specs=[
            pl.BlockSpec((MOD_ROWS, d), lambda l, j: (0, 0)),
            pl.BlockSpec((1, d, tn), lambda l, j: (l, 0, j)),
            pl.BlockSpec((1, 1, tn), lambda l, j: (l, 0, j)),
        ],
        out_specs=pl.BlockSpec((1, MOD_ROWS, tn), lambda l, j: (l, 0, j)),
        compiler_params=_cparams("arbitrary", "arbitrary"),
        name="modulation",
    )(cvec, ada_w, ada_b.reshape(depth, 1, n))


def _inproj_kernel(x_ref, mod_ref, g_ref, w_ref, lx_ref, xd_ref, lg_ref, z_ref, *, lw, nxd):
    x = x_ref[0]
    h = _rmsnorm(x, g_ref[...]) * (1.0 + mod_ref[0, 1:2, :]) + mod_ref[0, 0:1, :]
    h = h.astype(BF16)
    lx_ref[0] = _dot(h, w_ref[:, 0:lw])
    xd_ref[0] = _dot(h, w_ref[:, lw:lw + nxd])
    o = lw + nxd
    lg_ref[0] = _dot(h, w_ref[:, o:o + lw])
    z_ref[0] = _dot(h, w_ref[:, o + lw:o + 2 * lw])


def _inproj(x, mod, mod_row, g, w, layer, *, lw, nxd, tm):
    b, t, d = x.shape
    kern = functools.partial(_inproj_kernel, lw=lw, nxd=nxd)
    tok = lambda n: pl.BlockSpec((1, tm, n), lambda i, j: (i, j, 0))
    return pl.pallas_call(
        kern,
        out_shape=(
            jax.ShapeDtypeStruct((b, t, lw), F32),
            jax.ShapeDtypeStruct((b, t, nxd), F32),
            jax.ShapeDtypeStruct((b, t, lw), F32),
            jax.ShapeDtypeStruct((b, t, lw), F32),
        ),
        grid=(b, t // tm),
        in_specs=[
            tok(d),
            pl.BlockSpec((1, N_MOD, d), lambda i, j: (mod_row(i), 0, 0)),
            pl.BlockSpec((1, d), lambda i, j: (0, 0)),
            pl.BlockSpec((None,) + w.shape[1:], lambda i, j: (layer, 0, 0), pipeline_mode=pl.Buffered(1)),
        ],
        out_specs=(tok(lw), tok(nxd), tok(lw), tok(lw)),
        compiler_params=_cparams("arbitrary", "arbitrary"),
        name="inproj",
    )(x, mod, g, w)


LRU_ROWS = 256
LRU_TILES = 2


def _lru_seq(t, u_ref, write_out, h0, refs, scr, *, ntile, straight_line=False):
    cw_ref, cb_ref, wa_ref, wx_ref, ba_ref, bx_ref, lam_ref = refs
    upad, ubuf, uint, hbuf, pbuf, ybuf = scr
    seg = t // SUBLANES
    pitch = seg + 4
    nblk = max(t // LRU_ROWS, 1)
    rb = t // nblk
    nv = rb // SUBLANES
    tiles = [slice(c * LANES, (c + 1) * LANES) for c in range(ntile)]
    need_out = write_out is not None

    zeros8 = jnp.zeros((SUBLANES, LANES), F32)
    ones8 = jnp.ones((SUBLANES, LANES), F32)
    for c, ln in enumerate(tiles):
        upad[c, 0:SUBLANES, :] = zeros8
        upad[c, SUBLANES:SUBLANES + t, :] = u_ref[:, ln]
        upad[c, SUBLANES + t:2 * SUBLANES + t, :] = zeros8
        cw = cw_ref[:, ln]
        for s in range(SUBLANES):
            acc = cb_ref[:, ln]
            for tap in range(CONV_K):
                r = SUBLANES + s * seg + tap - 1
                acc = acc + cw[tap:tap + 1, :] * upad[c, r:r + seg, :]
            ubuf[c, s * pitch:s * pitch + seg, :] = acc
        for tau in range(seg):
            uint[c, tau * SUBLANES:(tau + 1) * SUBLANES, :] = ubuf[c, pl.ds(tau, SUBLANES, stride=pitch), :]

    row = lax.broadcasted_iota(jnp.int32, (SUBLANES, LANES), 0)
    kdec = [[(0.5 * LRU_C) * _softplus(-lam_ref[d:d + 1, ln]) for d in range(2)] for ln in tiles]
    chains = [(c, d) for c in range(ntile) for d in range(2)]

    def row0(blk):
        return blk * rb if isinstance(blk, int) else pl.multiple_of(blk * rb, rb)

    def blk_body(i, carry):
        carry = list(carry)
        for q, (c, d) in enumerate(chains):
            ln = tiles[c]
            h, p = carry[2 * q], carry[2 * q + 1]
            r0 = row0(i if d == 0 else nblk - 1 - i)
            u = uint[c, pl.ds(r0, rb), :]
            ub = u.astype(BF16)
            nla = kdec[c][d] * jnp.tanh(_dot(ub, wa_ref[d, c]) + ba_ref[d:d + 1, ln]) + kdec[c][d]
            gate = jnp.tanh(_dot(ub, wx_ref[d, c]) + bx_ref[d:d + 1, ln]) + 1.0
            a = jnp.exp2(nla * (-math.log2(math.e)))
            v = jnp.tanh(nla) * (1.0 + a * a)
            root = jnp.where(v > 0.0, v * lax.rsqrt(v), 0.0)
            bb = root * (gate * u)
            hs = [None] * nv
            ps = [None] * nv
            order = range(nv) if d == 0 else range(nv - 1, -1, -1)
            for k in order:
                ak = a[k * SUBLANES:(k + 1) * SUBLANES, :]
                h = ak * h + bb[k * SUBLANES:(k + 1) * SUBLANES, :]
                p = ak * p
                hs[k] = h
                ps[k] = p
            hbuf[q, pl.ds(r0, rb), :] = jnp.concatenate(hs, axis=0)
            pbuf[q, pl.ds(r0, rb), :] = jnp.concatenate(ps, axis=0)
            carry[2 * q], carry[2 * q + 1] = h, p
        return tuple(carry)

    def loop(body, init):
        if straight_line:
            for i in range(nblk):
                init = body(i, init)
            return init
        return lax.fori_loop(0, nblk, body, init)

    hp = loop(blk_body, (zeros8, ones8) * len(chains))
    finals = []
    cmats = []
    for q, (c, d) in enumerate(chains):
        h, p = hp[2 * q], hp[2 * q + 1]
        st = h0[q]
        cmat = zeros8
        order = range(SUBLANES) if d == 0 else range(SUBLANES - 1, -1, -1)
        for s in order:
            cmat = jnp.where(row == s, jnp.broadcast_to(st, (SUBLANES, LANES)), cmat)
            st = h[s:s + 1, :] + p[s:s + 1, :] * st
        finals.append(st)
        cmats.append(cmat)

    if need_out:
        def out_body(i, carry):
            r0 = row0(i)
            for c in range(ntile):
                y = None
                for q in (2 * c, 2 * c + 1):
                    t1 = hbuf[q, pl.ds(r0, rb), :].reshape(nv, SUBLANES, LANES)
                    t2 = pbuf[q, pl.ds(r0, rb), :].reshape(nv, SUBLANES, LANES) * cmats[q][None]
                    y = t1 + t2 if y is None else y + t1 + t2
                ybuf[c, pl.ds(r0, rb), :] = y.reshape(rb, LANES)
            return carry

        loop(out_body, 0)
        for c in range(ntile):
            for tau in range(seg):
                ubuf[c, pl.ds(tau, SUBLANES, stride=pitch), :] = ybuf[c, tau * SUBLANES:(tau + 1) * SUBLANES, :]
            for s in range(SUBLANES):
                write_out(c, s, ubuf[c, s * pitch:s * pitch + seg, :])
    return finals


def _lru_scratch(t, ntile):
    pitch_rows = SUBLANES * (t // SUBLANES + 4) + SUBLANES
    return [
        pltpu.VMEM((ntile, t + 2 * SUBLANES, LANES), F32),
        pltpu.VMEM((ntile, pitch_rows, LANES), F32),
        pltpu.VMEM((ntile, t, LANES), F32),
        pltpu.VMEM((2 * ntile, t, LANES), F32),
        pltpu.VMEM((2 * ntile, t, LANES), F32),
        pltpu.VMEM((ntile, t, LANES), F32),
    ]


def _lru_ctx_kernel(*args, t_ctx, need_ctx):
    if need_ctx:
        (uc_ref, *refs, hf_ref, yc_ref, upad, ubuf, uint, hbuf, pbuf, ybuf) = args
    else:
        (uc_ref, *refs, hf_ref, upad, ubuf, uint, hbuf, pbuf, ybuf) = args
    seg = t_ctx // SUBLANES

    def write_out(c, s, rows):
        yc_ref[0, s * seg:(s + 1) * seg, c * LANES:(c + 1) * LANES] = rows

    zero = jnp.zeros((1, LANES), F32)
    finals = _lru_seq(t_ctx, uc_ref.at[0], write_out if need_ctx else None, (zero,) * (2 * LRU_TILES), refs,
                      (upad, ubuf, uint, hbuf, pbuf, ybuf), ntile=LRU_TILES)
    for q, st in enumerate(finals):
        c, d = divmod(q, 2)
        hf_ref[0, d:d + 1, c * LANES:(c + 1) * LANES] = st


def _lru_ctx(lx_c, conv_w, conv_b, wa_bd, wx_bd, ba, bx, lam, *, need_ctx):
    b, t_ctx, w = lx_c.shape
    cwid = LRU_TILES * LANES
    seq = lambda t: pl.BlockSpec((1, t, cwid), lambda i, j: (i, 0, j))
    par = lambda r: pl.BlockSpec((r, cwid), lambda i, j: (0, j))
    gate = pl.BlockSpec((2, LRU_TILES, LANES, LANES), lambda i, j: (0, j, 0, 0))
    kern = functools.partial(_lru_ctx_kernel, t_ctx=t_ctx, need_ctx=need_ctx)
    out_shape = [jax.ShapeDtypeStruct((b, 2, w), F32)]
    out_specs = [seq(2)]
    if need_ctx:
        out_shape.append(jax.ShapeDtypeStruct((b, t_ctx, w), F32))
        out_specs.append(seq(t_ctx))
    res = pl.pallas_call(
        kern,
        out_shape=tuple(out_shape),
        grid=(b, w // cwid),
        in_specs=[seq(t_ctx), par(CONV_K), par(1), gate, gate, par(2), par(2), par(2)],
        out_specs=tuple(out_specs),
        scratch_shapes=_lru_scratch(t_ctx, LRU_TILES),
        compiler_params=_cparams("arbitrary", "arbitrary"),
        name="rglru_ctx",
    )(lx_c, conv_w, conv_b, wa_bd, wx_bd, ba, bx, lam)
    return (res[0], res[1]) if need_ctx else (res[0], None)


def _ssd_kernel(*args, t_lat, t_ctx, need_ctx):
    if need_ctx:
        (xl_hbm, xc_hbm, cw_ref, cb_ref, dtb_ref, alog_ref, dsk_ref, yl_hbm, yc_ref,
         xg, ybuf, sem_in, sem_out, stg, xm, scc, sbt, cumb, srct, etb, sbuf, hst, yacc) = args
    else:
        (xl_hbm, xc_hbm, cw_ref, cb_ref, dtb_ref, alog_ref, dsk_ref, yl_hbm,
         xg, ybuf, sem_in, sem_out, stg, xm, scc, sbt, cumb, srct, etb, sbuf, hst, yacc) = args
        yc_ref = None
    ck = SSD_CHUNK
    rows = t_lat // GRID_W
    nc_ctx = t_ctx // ck
    nc_lat = t_lat // ck
    nx = SSD_HEADS * SSD_HEAD_DIM // LANES
    ng = SSD_GROUPS
    nslab = nx + 2 * ng
    halo = SUBLANES

    ri = lax.broadcasted_iota(jnp.int32, (ck, ck), 0)
    ci = lax.broadcasted_iota(jnp.int32, (ck, ck), 1)
    lane_lo = ci < SSD_HEAD_DIM

    b = pl.program_id(0)
    nb = pl.num_programs(0)
    slot = lax.rem(b, 2)
    ctx0 = halo
    lat0 = ctx0 + t_ctx + halo
    width = xg.shape[-1]

    def in_copies(bi, sl):
        cps = [pltpu.make_async_copy(xc_hbm.at[bi], xg.at[sl, pl.ds(ctx0, t_ctx), :], sem_in.at[sl])]
        for w in range(GRID_W):
            cps.append(pltpu.make_async_copy(
                xl_hbm.at[bi, :, w, :], xg.at[sl, pl.ds(lat0 + w * rows, rows), :], sem_in.at[sl]))
        return cps

    def out_copies(bi):
        return [pltpu.make_async_copy(ybuf.at[pl.ds(w * rows, rows), :], yl_hbm.at[bi, :, w, :], sem_out.at[0])
                for w in range(GRID_W)]

    @pl.when(b == 0)
    def _():
        zpad = jnp.zeros((halo, width), F32)
        for sl in range(2):
            for r in (0, ctx0 + t_ctx, lat0 + t_lat):
                xg[sl, r:r + halo, :] = zpad
        for cp in in_copies(0, 0):
            cp.start()

    @pl.when(b + 1 < nb)
    def _():
        for cp in in_copies(b + 1, 1 - slot):
            cp.start()

    for cp in in_copies(b, slot):
        cp.wait()

    xs = xg.at[slot]

    def src_row(k):
        return pl.multiple_of(jnp.where(k < nc_ctx, ctx0 + k * ck, lat0 + (k - nc_ctx) * ck), SUBLANES)

    def conv_silu(src0, j):
        stg[j] = xs[pl.ds(src0 - halo, ck + 2 * halo), j * LANES:(j + 1) * LANES]
        cw = cw_ref[j]
        half = cb_ref[j]
        for tap in range(CONV_K):
            half = half + cw[tap:tap + 1, :] * stg[j, halo + tap - 1:halo + tap - 1 + ck, :]
        return half * jnp.tanh(half) + half

    dtb = dtb_ref[...]
    lane_ok = lax.broadcasted_iota(jnp.int32, (1, LANES), 1) < 2 * SSD_HEADS
    nega2 = jnp.where(lane_ok, -jnp.exp(alog_ref[...]) * math.log2(math.e), 0.0)

    nck = nc_ctx + nc_lat
    nh2 = 2 * SSD_HEADS
    hpp = SSD_HEADS // ng // 2
    dir1_col = lax.broadcasted_iota(jnp.int32, (1, LANES), 1) >= SSD_HEADS
    tris = (ri >= ci, ri <= ci)

    def xcat_of(j, r0):
        return jnp.concatenate([xm[j, pl.ds(r0, ck), :], xm[nx + j, pl.ds(r0, ck), :]], axis=0)

    def pass1(kc, carry):
        r0 = pl.multiple_of(kc * ck, ck)
        src0 = src_row(kc)
        dt = _softplus(xs[pl.ds(src0, ck), nslab * LANES:(nslab + 1) * LANES] + dtb)
        la = dt * nega2
        cumf = la
        sh = 1
        while sh < ck:
            if sh < SUBLANES:
                cumf = cumf + jnp.where(ri >= sh, pltpu.roll(cumf, sh, 0), 0.0)
            else:
                cumf = cumf + jnp.concatenate([jnp.zeros((sh, LANES), F32), cumf[0:ck - sh, :]], axis=0)
            sh *= 2
        tot = cumf[ck - 1:ck, :]
        cum = jnp.where(dir1_col, tot - cumf + la, cumf)
        ldt = jnp.log2(dt)
        cumb[pl.ds(r0, ck), :] = cum
        srct[pl.ds(pl.multiple_of(kc * nh2, nh2), nh2), :] = (cum - ldt).T[0:nh2, :]
        wgt_t = jnp.exp2((tot - cum + ldt).T[0:nh2, :])
        etb[pl.ds(pl.multiple_of(kc * SUBLANES, SUBLANES), SUBLANES), :] = jnp.broadcast_to(
            jnp.exp2(tot), (SUBLANES, LANES))
        for g in range(ng):
            scc[g, pl.ds(r0, ck), :] = conv_silu(src0, nx + ng + g).astype(BF16)
            btf = conv_silu(src0, nx + g).T
            sbt[g, pl.ds(r0, ck), :] = btf.astype(BF16)
            for pr in range(hpp):
                j = g * hpp + pr
                x = conv_silu(src0, j)
                yacc[j, pl.ds(r0, ck), :] = dsk_ref[j] * x
                xb = x.astype(BF16)
                zb = jnp.zeros_like(xb)
                xlo = jnp.where(lane_lo, xb, zb)
                xhi = jnp.where(lane_lo, zb, xb)
                xm[j, pl.ds(r0, ck), :] = xlo
                xm[nx + j, pl.ds(r0, ck), :] = xhi
                btw = []
                for d in range(2):
                    hc = (d * SSD_HEADS + 2 * j, d * SSD_HEADS + 2 * j + 1)
                    btw.append(jnp.concatenate([(btf * wgt_t[h:h + 1, :]).astype(BF16) for h in hc], axis=1))
                s = _dot(jnp.concatenate(btw, axis=0), jnp.concatenate([xlo, xhi], axis=0)).astype(BF16)
                sbuf[j, pl.ds(r0, ck), :] = s[0:SSD_STATE, :]
                sbuf[nx + j, pl.ds(r0, ck), :] = s[SSD_STATE:2 * SSD_STATE, :]
        return carry

    lax.fori_loop(0, nck, pass1, 0, unroll=2)

    hst[...] = jnp.zeros(hst.shape, F32)

    def pass2(i, carry):
        for d in range(2):
            if d == 0:
                kc = i
            else:
                kc = jnp.where(i < nc_ctx, nc_ctx - 1 - i, nck - 1 - (i - nc_ctx))
            r0 = pl.multiple_of(kc * ck, ck)
            e = etb[pl.ds(pl.multiple_of(kc * SUBLANES, SUBLANES), 1), :]
            for j in range(nx):
                hc = (d * SSD_HEADS + 2 * j, d * SSD_HEADS + 2 * j + 1)
                et = jnp.where(lane_lo[0:1, :], jnp.broadcast_to(e[:, hc[0]:hc[0] + 1], (1, LANES)),
                               jnp.broadcast_to(e[:, hc[1]:hc[1] + 1], (1, LANES)))
                h = hst[d * nx + j]
                s = sbuf[d * nx + j, pl.ds(r0, ck), :].astype(F32)
                sbuf[d * nx + j, pl.ds(r0, ck), :] = h.astype(BF16)
                hst[d * nx + j] = h * et + s
        return carry

    lax.fori_loop(0, nck, pass2, 0)

    def pass3(kc, is_ctx):
        r0 = kc * ck if isinstance(kc, int) else pl.multiple_of(kc * ck, ck)
        s0 = kc * nh2 if isinstance(kc, int) else pl.multiple_of(kc * nh2, nh2)
        cum = cumb[pl.ds(r0, ck), :]
        src_t = srct[pl.ds(s0, nh2), :]
        for g in range(ng):
            cm = scc[g, pl.ds(r0, ck), :]
            cb = _dot(cm, sbt[g, pl.ds(r0, ck), :])
            for pr in range(hpp):
                j = g * hpp + pr
                dec = []
                ecs = []
                for d in range(2):
                    hc = (d * SSD_HEADS + 2 * j, d * SSD_HEADS + 2 * j + 1)
                    col = [jnp.broadcast_to(cum[:, h:h + 1], (ck, LANES)) for h in hc]
                    dec.append([jnp.exp2(jnp.where(tris[d], col[q] - src_t[hc[q]:hc[q] + 1, :], NEG_BIG))
                                for q in range(2)])
                    ecs.append(jnp.exp2(jnp.where(lane_lo, col[0], col[1])))
                ms = [(cb * (dec[0][q] + dec[1][q])).astype(BF16) for q in range(2)]
                y = yacc[j, pl.ds(r0, ck), :] + _dot(jnp.concatenate(ms, axis=1), xcat_of(j, r0))
                hin = jnp.concatenate([sbuf[j, pl.ds(r0, ck), :], sbuf[nx + j, pl.ds(r0, ck), :]], axis=1)
                yoff = _dot(cm, hin)
                y = y + yoff[:, 0:LANES] * ecs[0] + yoff[:, LANES:2 * LANES] * ecs[1]
                if is_ctx:
                    yc_ref[0, pl.ds(r0, ck), j * LANES:(j + 1) * LANES] = y
                else:
                    y0 = pl.multiple_of((kc - nc_ctx) * ck, ck)
                    ybuf[pl.ds(y0, ck), j * LANES:(j + 1) * LANES] = y

    if need_ctx:
        for kc in range(nc_ctx):
            pass3(kc, True)

    @pl.when(b > 0)
    def _():
        for cp in out_copies(b - 1):
            cp.wait()

    def lat_body(c, carry):
        pass3(nc_ctx + c, False)
        return carry

    lax.fori_loop(0, nc_lat, lat_body, 0, unroll=4)

    for cp in out_copies(b):
        cp.start()

    @pl.when(b == nb - 1)
    def _():
        for cp in out_copies(b):
            cp.wait()


def _ssd(xd_l, xd_c, cw, cb, dtb, alog, dsk, *, need_ctx):
    b, t_lat, width = xd_l.shape
    t_ctx = xd_c.shape[1]
    nx = SSD_HEADS * SSD_HEAD_DIM // LANES
    ng = SSD_GROUPS
    inner = nx * LANES
    tot = t_lat + t_ctx
    rows = t_lat // GRID_W
    kern = functools.partial(_ssd_kernel, t_lat=t_lat, t_ctx=t_ctx, need_ctx=need_ctx)
    full = lambda a: pl.BlockSpec(a.shape, lambda i: (0,) * a.ndim)
    hbm = pl.BlockSpec(memory_space=pl.ANY)
    out_shape = [jax.ShapeDtypeStruct((b, rows, GRID_W, inner), F32)]
    out_specs = [hbm]
    if need_ctx:
        out_shape.append(jax.ShapeDtypeStruct((b, t_ctx, inner), F32))
        out_specs.append(pl.BlockSpec((1, t_ctx, inner), lambda i: (i, 0, 0)))
    res = pl.pallas_call(
        kern,
        out_shape=tuple(out_shape),
        grid=(b,),
        in_specs=[hbm, hbm, full(cw), full(cb), full(dtb), full(alog), full(dsk)],
        out_specs=tuple(out_specs),
        scratch_shapes=[
            pltpu.VMEM((2, tot + 3 * SUBLANES, width), F32),
            pltpu.VMEM((t_lat, inner), F32),
            pltpu.SemaphoreType.DMA((2,)),
            pltpu.SemaphoreType.DMA((1,)),
            pltpu.VMEM((nx + 2 * ng, SSD_CHUNK + 2 * SUBLANES, LANES), F32),
            pltpu.VMEM((2 * nx, tot, LANES), BF16),
            pltpu.VMEM((ng, tot, LANES), BF16),
            pltpu.VMEM((ng, tot, LANES), BF16),
            pltpu.VMEM((tot, LANES), F32),
            pltpu.VMEM((tot // SSD_CHUNK * 2 * SSD_HEADS, LANES), F32),
            pltpu.VMEM((tot // SSD_CHUNK * SUBLANES, LANES), F32),
            pltpu.VMEM((2 * nx, tot, LANES), BF16),
            pltpu.VMEM((2 * nx, SSD_STATE, LANES), F32),
            pltpu.VMEM((nx, tot, LANES), F32),
        ],
        compiler_params=_cparams("arbitrary"),
        name="ssd",
    )(xd_l.reshape(b, rows, GRID_W, width), xd_c, cw, cb, dtb, alog, dsk)
    y_l = res[0].reshape(b, t_lat, inner)
    return (y_l, res[1]) if need_ctx else (y_l, None)


MLP_FF_CHUNK = 1024


def _outmlp_body(lru_raw, refs, fg_ref, o_ref):
    x_ref, lg_ref, ssd_ref, z_ref, mod_ref, sg_ref, n2_ref, wo_ref, w1_ref, w2_ref = refs
    lw = z_ref.shape[-1]
    gw = lw // SSD_GROUPS
    dff = w1_ref.shape[1]
    parts = []
    for g in range(SSD_GROUPS):
        y = ssd_ref[0, :, g * gw:(g + 1) * gw] * _silu(z_ref[0, :, g * gw:(g + 1) * gw])
        y = y * lax.rsqrt(jnp.mean(y * y, axis=-1, keepdims=True) + EPS)
        parts.append(y * sg_ref[:, g * gw:(g + 1) * gw])
    ssd_n = jnp.concatenate(parts, axis=1).astype(BF16)
    lru = (lru_raw * _gelu_tanh(lg_ref[0])).astype(BF16)
    y = _dot(lru, wo_ref[0:lw, :]) + _dot(ssd_n, wo_ref[lw:2 * lw, :])
    x1 = x_ref[0] + mod_ref[0, 2:3, :] * y
    h = (_rmsnorm(x1, n2_ref[...]) * (1.0 + mod_ref[0, 4:5, :]) + mod_ref[0, 3:4, :]).astype(BF16)
    acc = jnp.zeros_like(x1)
    for j in range(dff // MLP_FF_CHUNK):
        a = _dot(h, w1_ref[:, j * MLP_FF_CHUNK:(j + 1) * MLP_FF_CHUNK])
        a = jnp.square(jnp.maximum(a, 0.0)).astype(BF16)
        acc = acc + _dot(a, w2_ref[j * MLP_FF_CHUNK:(j + 1) * MLP_FF_CHUNK, :])
    x2 = x1 + mod_ref[0, 5:6, :] * acc
    if fg_ref is not None:
        x2 = _rmsnorm(x2, fg_ref[...])
    o_ref[0] = x2


def _outmlp_kernel(*args, final_norm):
    x_ref, lru_ref, *rest = args
    if final_norm:
        *refs, fg_ref, o_ref = rest
    else:
        *refs, o_ref = rest
        fg_ref = None
    _outmlp_body(lru_ref[0], (x_ref, *refs), fg_ref, o_ref)


def _outmlp(x, lru, lg, ssd, z, mod, mod_row, sg, n2, wo, w1, w2, layer, fg, *, tm):
    b, t, d = x.shape
    lw = lru.shape[-1]
    final_norm = fg is not None
    kern = functools.partial(_outmlp_kernel, final_norm=final_norm)
    tok = lambda n: pl.BlockSpec((1, tm, n), lambda i, j: (i, j, 0))
    res = lambda a: pl.BlockSpec((None,) + a.shape[1:], lambda i, j: (layer, 0, 0), pipeline_mode=pl.Buffered(1))
    row = lambda n: pl.BlockSpec((1, n), lambda i, j: (0, 0))
    ins = [x, lru, lg, ssd, z, mod, sg, n2, wo, w1, w2]
    in_specs = [tok(d), tok(lw), tok(lw), tok(lw), tok(lw),
                pl.BlockSpec((1, N_MOD, d), lambda i, j: (mod_row(i), 0, 0)),
                row(lw), row(d), res(wo), res(w1), res(w2)]
    if final_norm:
        ins.append(fg)
        in_specs.append(row(d))
    return pl.pallas_call(
        kern,
        out_shape=jax.ShapeDtypeStruct((b, t, d), F32),
        grid=(b, t // tm),
        in_specs=in_specs,
        out_specs=tok(d),
        compiler_params=_cparams("arbitrary", "arbitrary"),
        name="outmlp",
    )(*ins)


def _lru_lat_kernel(lx_ref, h0_ref, *rest, t_lat):
    *refs, y_ref, upad, ubuf, uint, hbuf, pbuf, ybuf = rest
    seg = t_lat // SUBLANES

    def write_out(c, s, rows):
        y_ref[0, s * seg:(s + 1) * seg, c * LANES:(c + 1) * LANES] = rows

    h0 = [h0_ref[0, d:d + 1, c * LANES:(c + 1) * LANES] for c in range(LRU_TILES) for d in range(2)]
    _lru_seq(t_lat, lx_ref.at[0], write_out, h0, refs,
             (upad, ubuf, uint, hbuf, pbuf, ybuf), ntile=LRU_TILES, straight_line=True)


def _lru_lat(lx, h0, conv_w, conv_b, wa_bd, wx_bd, ba, bx, lam):
    b, t, w = lx.shape
    cwid = LRU_TILES * LANES
    seq = lambda r: pl.BlockSpec((1, r, cwid), lambda i, j: (i, 0, j))
    par = lambda r: pl.BlockSpec((r, cwid), lambda i, j: (0, j))
    gate = pl.BlockSpec((2, LRU_TILES, LANES, LANES), lambda i, j: (0, j, 0, 0))
    return pl.pallas_call(
        functools.partial(_lru_lat_kernel, t_lat=t),
        out_shape=jax.ShapeDtypeStruct((b, t, w), F32),
        grid=(b, w // cwid),
        in_specs=[seq(t), seq(2), par(CONV_K), par(1), gate, gate, par(2), par(2), par(2)],
        out_specs=seq(t),
        scratch_shapes=_lru_scratch(t, LRU_TILES),
        compiler_params=_cparams("arbitrary", "arbitrary"),
        name="rglru_lat",
    )(lx, h0, conv_w, conv_b, wa_bd, wx_bd, ba, bx, lam)


def _pair_block_diag(w):
    two, h, k, _ = w.shape
    w = w.reshape(two, h // 2, 2, k, k)
    z = jnp.zeros_like(w[:, :, 0])
    top = jnp.concatenate([w[:, :, 0], z], axis=-1)
    bot = jnp.concatenate([z, w[:, :, 1]], axis=-1)
    return jnp.concatenate([top, bot], axis=-2)


def _pad_lanes(a):
    return jnp.pad(a, [(0, 0)] * (a.ndim - 1) + [(0, LANES - a.shape[-1])])


def kernel(x, c, ctx, c_ctx, ada_w, ada_b, norm1_g, norm2_g, w_in, lru_conv_w, lru_conv_b, lru_wa, lru_ba,
           lru_wx, lru_bx, lru_lambda, ssd_conv_w, ssd_conv_b, ssd_dt_bias, ssd_a_log, ssd_d, ssd_norm_g,
           w_out, mlp_w1, mlp_w2, final_g):
    bsz, t_lat, d = x.shape
    t_ctx = ctx.shape[1]
    depth = ada_w.shape[0]
    lw = lru_conv_w.shape[-1]
    nxbc = ssd_conv_w.shape[-1]
    ndt = 2 * SSD_HEADS
    nscan = lw + nxbc + ndt
    assert bsz + 1 <= MOD_ROWS and t_lat % (GRID_W * SUBLANES) == 0 and t_ctx % SSD_CHUNK == 0

    cvec = jnp.zeros((MOD_ROWS, d), F32).at[:bsz].set(c).at[bsz].set(c_ctx)
    mod = _modulation(cvec, ada_w, ada_b).reshape(depth, MOD_ROWS, N_MOD, d)

    lat_row = lambda i: i
    ctx_row = lambda i: bsz
    tm_ctx = t_ctx
    nslab = nxbc // LANES
    nxd = nxbc + LANES

    w_re = jnp.concatenate(
        [w_in[:, :, :lw + nxbc], _pad_lanes(w_in[:, :, lw + nxbc:nscan]), w_in[:, :, nscan:]], axis=2).astype(BF16)
    wo = w_out.astype(BF16)
    w1 = mlp_w1.astype(BF16)
    w2 = mlp_w2.astype(BF16)

    for l in range(depth):
        need_ctx = l < depth - 1
        g1 = norm1_g[l].reshape(1, d)
        lx_l, xd_l, lg_l, z_l = _inproj(x, mod[l], lat_row, g1, w_re, l, lw=lw, nxd=nxd, tm=INPROJ_ROWS)
        lx_c, xd_c, lg_c, z_c = _inproj(ctx, mod[l], ctx_row, g1, w_re, l, lw=lw, nxd=nxd, tm=tm_ctx)

        lru_par = (0.5 * lru_conv_w[l], 0.5 * lru_conv_b[l].reshape(1, lw),
                   _pair_block_diag(lru_wa[l]).astype(BF16), _pair_block_diag(lru_wx[l]).astype(BF16),
                   0.5 * lru_ba[l], 0.5 * lru_bx[l], lru_lambda[l])
        h0_c, lru_c = _lru_ctx(lx_c, *lru_par, need_ctx=need_ctx)

        cw = 0.5 * ssd_conv_w[l].reshape(CONV_K, nslab, LANES).transpose(1, 0, 2)
        cb = 0.5 * ssd_conv_b[l].reshape(nslab, 1, LANES)
        dtb = _pad_lanes(ssd_dt_bias[l].reshape(1, ndt))
        alog = _pad_lanes(ssd_a_log[l].reshape(1, ndt))
        dsk = jnp.repeat(ssd_d[l], SSD_HEAD_DIM).reshape(lw // LANES, 1, LANES)
        ssd_l, ssd_c = _ssd(xd_l, xd_c, cw, cb, dtb, alog, dsk, need_ctx=need_ctx)

        sg = ssd_norm_g[l].reshape(1, lw)
        n2 = norm2_g[l].reshape(1, d)
        fg = final_g.reshape(1, d) if l == depth - 1 else None
        lru_l = _lru_lat(lx_l, h0_c, *lru_par)
        x = _outmlp(x, lru_l, lg_l, ssd_l, z_l, mod[l], lat_row, sg, n2, wo, w1, w2, l, fg, tm=MLP_ROWS)
        if need_ctx:
            ctx = _outmlp(ctx, lru_c, lg_c, ssd_c, z_c, mod[l], ctx_row, sg, n2, wo, w1, w2, l, None,
                          tm=tm_ctx)
    return x
```

```python
import functools
import math

import jax
import jax.numpy as jnp
from jax import lax
from jax.experimental import pallas as pl
from jax.experimental.pallas import tpu as pltpu

F32 = jnp.float32
BF16 = jnp.bfloat16

EPS = 1e-6
GRID_W = 64
LRU_C = 8.0
SSD_HEAD_DIM = 64
SSD_HEADS = 8
SSD_GROUPS = 2
SSD_STATE = 128
SSD_CHUNK = 128
CONV_K = 4
N_MOD = 6
LANES = 128
SUBLANES = 8
MOD_ROWS = 16
VMEM_LIMIT = 56 * 1024 * 1024
NEG_BIG = -1e30
MOD_COLS = 1536
INPROJ_ROWS = 1024
MLP_ROWS = 512


def _sigmoid(x):
    return 0.5 * (jnp.tanh(0.5 * x) + 1.0)


def _silu(x):
    return x * _sigmoid(x)


def _softplus(x):
    return jnp.maximum(x, 0.0) + jnp.log1p(jnp.exp(-jnp.abs(x)))


def _gelu_tanh(x):
    c = math.sqrt(2.0 / math.pi)
    return (0.5 * x) * (1.0 + jnp.tanh(x * (c + (c * 0.044715) * (x * x))))


def _rmsnorm(x, g):
    return x * lax.rsqrt(jnp.mean(x * x, axis=-1, keepdims=True) + EPS) * g


def _dot(a, b):
    return jnp.dot(a, b, preferred_element_type=F32)


def _cparams(*sem):
    return pltpu.CompilerParams(dimension_semantics=sem, vmem_limit_bytes=VMEM_LIMIT)


def _mod_kernel(c_ref, w_ref, b_ref, o_ref):
    s = _silu(c_ref[...]).astype(BF16)
    o_ref[0] = _dot(s, w_ref[0].astype(BF16)) + b_ref[0]


def _modulation(cvec, ada_w, ada_b):
    depth, d, n = ada_w.shape
    tn = MOD_COLS
    return pl.pallas_call(
        _mod_kernel,
        out_shape=jax.ShapeDtypeStruct((depth, MOD_ROWS, n), F32),
        grid=(depth, n // tn),
        in_specs=[
            pl.BlockSpec((MOD_ROWS, d), lambda l, j: (0, 0)),
            pl.BlockSpec((1, d, tn), lambda l, j: (l, 0, j)),
            pl.BlockSpec((1, 1, tn), lambda l, j: (l, 0, j)),
        ],
        out_specs=pl.BlockSpec((1, MOD_ROWS, tn), lambda l, j: (l, 0, j)),
        compiler_params=_cparams("arbitrary", "arbitrary"),
        name="modulation",
    )(cvec, ada_w, ada_b.reshape(depth, 1, n))


def _inproj_kernel(x_ref, mod_ref, g_ref, w_ref, lx_ref, xd_ref, lg_ref, z_ref, *, lw, nxd):
    x = x_ref[0]
    h = _rmsnorm(x, g_ref[...]) * (1.0 + mod_ref[0, 1:2, :]) + mod_ref[0, 0:1, :]
    h = h.astype(BF16)
    lx_ref[0] = _dot(h, w_ref[:, 0:lw])
    xd_ref[0] = _dot(h, w_ref[:, lw:lw + nxd])
    o = lw + nxd
    lg_ref[0] = _dot(h, w_ref[:, o:o + lw])
    z_ref[0] = _dot(h, w_ref[:, o + lw:o + 2 * lw])


def _inproj(x, mod, mod_row, g, w, layer, *, lw, nxd, tm):
    b, t, d = x.shape
    kern = functools.partial(_inproj_kernel, lw=lw, nxd=nxd)
    tok = lambda n: pl.BlockSpec((1, tm, n), lambda i, j: (i, j, 0))
    return pl.pallas_call(
        kern,
        out_shape=(
            jax.ShapeDtypeStruct((b, t, lw), F32),
            jax.ShapeDtypeStruct((b, t, nxd), F32),
            jax.ShapeDtypeStruct((b, t, lw), F32),
            jax.ShapeDtypeStruct((b, t, lw), F32),
        ),
        grid=(b, t // tm),
        in_specs=[
            tok(d),
            pl.BlockSpec((1, N_MOD, d), lambda i, j: (mod_row(i), 0, 0)),
            pl.BlockSpec((1, d), lambda i, j: (0, 0)),
            pl.BlockSpec((None,) + w.shape[1:], lambda i, j: (layer, 0, 0), pipeline_mode=pl.Buffered(1)),
        ],
        out_specs=(tok(lw), tok(nxd), tok(lw), tok(lw)),
        compiler_params=_cparams("arbitrary", "arbitrary"),
        name="inproj",
    )(x, mod, g, w)


LRU_ROWS = 256
LRU_TILES = 2


def _lru_seq(t, u_ref, write_out, h0, refs, scr, *, ntile, straight_line=False):
    cw_ref, cb_ref, wa_ref, wx_ref, ba_ref, bx_ref, lam_ref = refs
    upad, ubuf, uint, hbuf, pbuf, ybuf = scr
    seg = t // SUBLANES
    pitch = seg + 4
    nblk = max(t // LRU_ROWS, 1)
    rb = t // nblk
    nv = rb // SUBLANES
    tiles = [slice(c * LANES, (c + 1) * LANES) for c in range(ntile)]
    need_out = write_out is not None

    zeros8 = jnp.zeros((SUBLANES, LANES), F32)
    ones8 = jnp.ones((SUBLANES, LANES), F32)
    for c, ln in enumerate(tiles):
        upad[c, 0:SUBLANES, :] = zeros8
        upad[c, SUBLANES:SUBLANES + t, :] = u_ref[:, ln]
        upad[c, SUBLANES + t:2 * SUBLANES + t, :] = zeros8
        cw = cw_ref[:, ln]
        for s in range(SUBLANES):
            acc = cb_ref[:, ln]
            for tap in range(CONV_K):
                r = SUBLANES + s * seg + tap - 1
                acc = acc + cw[tap:tap + 1, :] * upad[c, r:r + seg, :]
            ubuf[c, s * pitch:s * pitch + seg, :] = acc
        for tau in range(seg):
            uint[c, tau * SUBLANES:(tau + 1) * SUBLANES, :] = ubuf[c, pl.ds(tau, SUBLANES, stride=pitch), :]

    row = lax.broadcasted_iota(jnp.int32, (SUBLANES, LANES), 0)
    kdec = [[(0.5 * LRU_C) * _softplus(-lam_ref[d:d + 1, ln]) for d in range(2)] for ln in tiles]
    chains = [(c, d) for c in range(ntile) for d in range(2)]

    def row0(blk):
        return blk * rb if isinstance(blk, int) else pl.multiple_of(blk * rb, rb)

    def blk_body(i, carry):
        carry = list(carry)
        for q, (c, d) in enumerate(chains):
            ln = tiles[c]
            h, p = carry[2 * q], carry[2 * q + 1]
            r0 = row0(i if d == 0 else nblk - 1 - i)
            u = uint[c, pl.ds(r0, rb), :]
            ub = u.astype(BF16)
            nla = kdec[c][d] * jnp.tanh(_dot(ub, wa_ref[d, c]) + ba_ref[d:d + 1, ln]) + kdec[c][d]
            gate = jnp.tanh(_dot(ub, wx_ref[d, c]) + bx_ref[d:d + 1, ln]) + 1.0
            a = jnp.exp2(nla * (-math.log2(math.e)))
            v = jnp.tanh(nla) * (1.0 + a * a)
            root = jnp.where(v > 0.0, v * lax.rsqrt(v), 0.0)
            bb = root * (gate * u)
            hs = [None] * nv
            ps = [None] * nv
            order = range(nv) if d == 0 else range(nv - 1, -1, -1)
            for k in order:
                ak = a[k * SUBLANES:(k + 1) * SUBLANES, :]
                h = ak * h + bb[k * SUBLANES:(k + 1) * SUBLANES, :]
                p = ak * p
                hs[k] = h
                ps[k] = p
            hbuf[q, pl.ds(r0, rb), :] = jnp.concatenate(hs, axis=0)
            pbuf[q, pl.ds(r0, rb), :] = jnp.concatenate(ps, axis=0)
            carry[2 * q], carry[2 * q + 1] = h, p
        return tuple(carry)

    def loop(body, init):
        if straight_line:
            for i in range(nblk):
                init = body(i, init)
            return init
        return lax.fori_loop(0, nblk, body, init)

    hp = loop(blk_body, (zeros8, ones8) * len(chains))
    finals = []
    cmats = []
    for q, (c, d) in enumerate(chains):
        h, p = hp[2 * q], hp[2 * q + 1]
        st = h0[q]
        cmat = zeros8
        order = range(SUBLANES) if d == 0 else range(SUBLANES - 1, -1, -1)
        for s in order:
            cmat = jnp.where(row == s, jnp.broadcast_to(st, (SUBLANES, LANES)), cmat)
            st = h[s:s + 1, :] + p[s:s + 1, :] * st
        finals.append(st)
        cmats.append(cmat)

    if need_out:
        def out_body(i, carry):
            r0 = row0(i)
            for c in range(ntile):
                y = None
                for q in (2 * c, 2 * c + 1):
                    t1 = hbuf[q, pl.ds(r0, rb), :].reshape(nv, SUBLANES, LANES)
                    t2 = pbuf[q, pl.ds(r0, rb), :].reshape(nv, SUBLANES, LANES) * cmats[q][None]
                    y = t1 + t2 if y is None else y + t1 + t2
                ybuf[c, pl.ds(r0, rb), :] = y.reshape(rb, LANES)
            return carry

        loop(out_body, 0)
        for c in range(ntile):
            for tau in range(seg):
                ubuf[c, pl.ds(tau, SUBLANES, stride=pitch), :] = ybuf[c, tau * SUBLANES:(tau + 1) * SUBLANES, :]
            for s in range(SUBLANES):
                write_out(c, s, ubuf[c, s * pitch:s * pitch + seg, :])
    return finals


def _lru_scratch(t, ntile):
    pitch_rows = SUBLANES * (t // SUBLANES + 4) + SUBLANES
    return [
        pltpu.VMEM((ntile, t + 2 * SUBLANES, LANES), F32),
        pltpu.VMEM((ntile, pitch_rows, LANES), F32),
        pltpu.VMEM((ntile, t, LANES), F32),
        pltpu.VMEM((2 * ntile, t, LANES), F32),
        pltpu.VMEM((2 * ntile, t, LANES), F32),
        pltpu.VMEM((ntile, t, LANES), F32),
    ]


def _lru_ctx_kernel(*args, t_ctx, need_ctx):
    if need_ctx:
        (uc_ref, *refs, hf_ref, yc_ref, upad, ubuf, uint, hbuf, pbuf, ybuf) = args
    else:
        (uc_ref, *refs, hf_ref, upad, ubuf, uint, hbuf, pbuf, ybuf) = args
    seg = t_ctx // SUBLANES

    def write_out(c, s, rows):
        yc_ref[0, s * seg:(s + 1) * seg, c * LANES:(c + 1) * LANES] = rows

    zero = jnp.zeros((1, LANES), F32)
    finals = _lru_seq(t_ctx, uc_ref.at[0], write_out if need_ctx else None, (zero,) * (2 * LRU_TILES), refs,
                      (upad, ubuf, uint, hbuf, pbuf, ybuf), ntile=LRU_TILES)
    for q, st in enumerate(finals):
        c, d = divmod(q, 2)
        hf_ref[0, d:d + 1, c * LANES:(c + 1) * LANES] = st


def _lru_ctx(lx_c, conv_w, conv_b, wa_bd, wx_bd, ba, bx, lam, *, need_ctx):
    b, t_ctx, w = lx_c.shape
    cwid = LRU_TILES * LANES
    seq = lambda t: pl.BlockSpec((1, t, cwid), lambda i, j: (i, 0, j))
    par = lambda r: pl.BlockSpec((r, cwid), lambda i, j: (0, j))
    gate = pl.BlockSpec((2, LRU_TILES, LANES, LANES), lambda i, j: (0, j, 0, 0))
    kern = functools.partial(_lru_ctx_kernel, t_ctx=t_ctx, need_ctx=need_ctx)
    out_shape = [jax.ShapeDtypeStruct((b, 2, w), F32)]
    out_specs = [seq(2)]
    if need_ctx:
        out_shape.append(jax.ShapeDtypeStruct((b, t_ctx, w), F32))
        out_specs.append(seq(t_ctx))
    res = pl.pallas_call(
        kern,
        out_shape=tuple(out_shape),
        grid=(b, w // cwid),
        in_specs=[seq(t_ctx), par(CONV_K), par(1), gate, gate, par(2), par(2), par(2)],
        out_specs=tuple(out_specs),
        scratch_shapes=_lru_scratch(t_ctx, LRU_TILES),
        compiler_params=_cparams("arbitrary", "arbitrary"),
        name="rglru_ctx",
    )(lx_c, conv_w, conv_b, wa_bd, wx_bd, ba, bx, lam)
    return (res[0], res[1]) if need_ctx else (res[0], None)


def _ssd_kernel(*args, t_lat, t_ctx, need_ctx):
    if need_ctx:
        (xl_hbm, xc_hbm, cw_ref, cb_ref, dtb_ref, alog_ref, dsk_ref, yl_hbm, yc_ref,
         xg, ybuf, sem_in, sem_out, stg, xm, scc, sbt, cumb, srct, etb, sbuf, hst, yacc) = args
    else:
        (xl_hbm, xc_hbm, cw_ref, cb_ref, dtb_ref, alog_ref, dsk_ref, yl_hbm,
         xg, ybuf, sem_in, sem_out, stg, xm, scc, sbt, cumb, srct, etb, sbuf, hst, yacc) = args
        yc_ref = None
    ck = SSD_CHUNK
    rows = t_lat // GRID_W
    nc_ctx = t_ctx // ck
    nc_lat = t_lat // ck
    nx = SSD_HEADS * SSD_HEAD_DIM // LANES
    ng = SSD_GROUPS
    nslab = nx + 2 * ng
    halo = SUBLANES

    ri = lax.broadcasted_iota(jnp.int32, (ck, ck), 0)
    ci = lax.broadcasted_iota(jnp.int32, (ck, ck), 1)
    lane_lo = ci < SSD_HEAD_DIM

    b = pl.program_id(0)
    nb = pl.num_programs(0)
    slot = lax.rem(b, 2)
    ctx0 = halo
    lat0 = ctx0 + t_ctx + halo
    width = xg.shape[-1]

    def in_copies(bi, sl):
        cps = [pltpu.make_async_copy(xc_hbm.at[bi], xg.at[sl, pl.ds(ctx0, t_ctx), :], sem_in.at[sl])]
        for w in range(GRID_W):
            cps.append(pltpu.make_async_copy(
                xl_hbm.at[bi, :, w, :], xg.at[sl, pl.ds(lat0 + w * rows, rows), :], sem_in.at[sl]))
        return cps

    def out_copies(bi):
        return [pltpu.make_async_copy(ybuf.at[pl.ds(w * rows, rows), :], yl_hbm.at[bi, :, w, :], sem_out.at[0])
                for w in range(GRID_W)]

    @pl.when(b == 0)
    def _():
        zpad = jnp.zeros((halo, width), F32)
        for sl in range(2):
            for r in (0, ctx0 + t_ctx, lat0 + t_lat):
                xg[sl, r:r + halo, :] = zpad
        for cp in in_copies(0, 0):
            cp.start()

    @pl.when(b + 1 < nb)
    def _():
        for cp in in_copies(b + 1, 1 - slot):
            cp.start()

    for cp in in_copies(b, slot):
        cp.wait()

    xs = xg.at[slot]

    def src_row(k):
        return pl.multiple_of(jnp.where(k < nc_ctx, ctx0 + k * ck, lat0 + (k - nc_ctx) * ck), SUBLANES)

    def conv_silu(src0, j):
        stg[j] = xs[pl.ds(src0 - halo, ck + 2 * halo), j * LANES:(j + 1) * LANES]
        cw = cw_ref[j]
        half = cb_ref[j]
        for tap in range(CONV_K):
            half = half + cw[tap:tap + 1, :] * stg[j, halo + tap - 1:halo + tap - 1 + ck, :]
        return half * jnp.tanh(half) + half

    dtb = dtb_ref[...]
    lane_ok = lax.broadcasted_iota(jnp.int32, (1, LANES), 1) < 2 * SSD_HEADS
    nega2 = jnp.where(lane_ok, -jnp.exp(alog_ref[...]) * math.log2(math.e), 0.0)

    nck = nc_ctx + nc_lat
    nh2 = 2 * SSD_HEADS
    hpp = SSD_HEADS // ng // 2
    dir1_col = lax.broadcasted_iota(jnp.int32, (1, LANES), 1) >= SSD_HEADS
    tris = (ri >= ci, ri <= ci)

    def xcat_of(j, r0):
        return jnp.concatenate([xm[j, pl.ds(r0, ck), :], xm[nx + j, pl.ds(r0, ck), :]], axis=0)

    def pass1(kc, carry):
        r0 = pl.multiple_of(kc * ck, ck)
        src0 = src_row(kc)
        dt = _softplus(xs[pl.ds(src0, ck), nslab * LANES:(nslab + 1) * LANES] + dtb)
        la = dt * nega2
        cumf = la
        sh = 1
        while sh < ck:
            if sh < SUBLANES:
                cumf = cumf + jnp.where(ri >= sh, pltpu.roll(cumf, sh, 0), 0.0)
            else:
                cumf = cumf + jnp.concatenate([jnp.zeros((sh, LANES), F32), cumf[0:ck - sh, :]], axis=0)
            sh *= 2
        tot = cumf[ck - 1:ck, :]
        cum = jnp.where(dir1_col, tot - cumf + la, cumf)
        ldt = jnp.log2(dt)
        cumb[pl.ds(r0, ck), :] = cum
        srct[pl.ds(pl.multiple_of(kc * nh2, nh2), nh2), :] = (cum - ldt).T[0:nh2, :]
        wgt_t = jnp.exp2((tot - cum + ldt).T[0:nh2, :])
        etb[pl.ds(pl.multiple_of(kc * SUBLANES, SUBLANES), SUBLANES), :] = jnp.broadcast_to(
            jnp.exp2(tot), (SUBLANES, LANES))
        for g in range(ng):
            scc[g, pl.ds(r0, ck), :] = conv_silu(src0, nx + ng + g).astype(BF16)
            btf = conv_silu(src0, nx + g).T
            sbt[g, pl.ds(r0, ck), :] = btf.astype(BF16)
            for pr in range(hpp):
                j = g * hpp + pr
                x = conv_silu(src0, j)
                yacc[j, pl.ds(r0, ck), :] = dsk_ref[j] * x
                xb = x.astype(BF16)
                zb = jnp.zeros_like(xb)
                xlo = jnp.where(lane_lo, xb, zb)
                xhi = jnp.where(lane_lo, zb, xb)
                xm[j, pl.ds(r0, ck), :] = xlo
                xm[nx + j, pl.ds(r0, ck), :] = xhi
                btw = []
                for d in range(2):
                    hc = (d * SSD_HEADS + 2 * j, d * SSD_HEADS + 2 * j + 1)
                    btw.append(jnp.concatenate([(btf * wgt_t[h:h + 1, :]).astype(BF16) for h in hc], axis=1))
                s = _dot(jnp.concatenate(btw, axis=0), jnp.concatenate([xlo, xhi], axis=0)).astype(BF16)
                sbuf[j, pl.ds(r0, ck), :] = s[0:SSD_STATE, :]
                sbuf[nx + j, pl.ds(r0, ck), :] = s[SSD_STATE:2 * SSD_STATE, :]
        return carry

    lax.fori_loop(0, nck, pass1, 0, unroll=6)

    hst[...] = jnp.zeros(hst.shape, F32)

    def pass2(i, carry):
        for d in range(2):
            if d == 0:
                kc = i
            else:
                kc = jnp.where(i < nc_ctx, nc_ctx - 1 - i, nck - 1 - (i - nc_ctx))
            r0 = pl.multiple_of(kc * ck, ck)
            e = etb[pl.ds(pl.multiple_of(kc * SUBLANES, SUBLANES), 1), :]
            for j in range(nx):
                hc = (d * SSD_HEADS + 2 * j, d * SSD_HEADS + 2 * j + 1)
                et = jnp.where(lane_lo[0:1, :], jnp.broadcast_to(e[:, hc[0]:hc[0] + 1], (1, LANES)),
                               jnp.broadcast_to(e[:, hc[1]:hc[1] + 1], (1, LANES)))
                h = hst[d * nx + j]
                s = sbuf[d * nx + j, pl.ds(r0, ck), :].astype(F32)
                sbuf[d * nx + j, pl.ds(r0, ck), :] = h.astype(BF16)
                hst[d * nx + j] = h * et + s
        return carry

    lax.fori_loop(0, nck, pass2, 0)

    def pass3(kc, is_ctx):
        r0 = kc * ck if isinstance(kc, int) else pl.multiple_of(kc * ck, ck)
        s0 = kc * nh2 if isinstance(kc, int) else pl.multiple_of(kc * nh2, nh2)
        cum = cumb[pl.ds(r0, ck), :]
        src_t = srct[pl.ds(s0, nh2), :]
        for g in range(ng):
            cm = scc[g, pl.ds(r0, ck), :]
            cb = _dot(cm, sbt[g, pl.ds(r0, ck), :])
            for pr in range(hpp):
                j = g * hpp + pr
                dec = []
                ecs = []
                for d in range(2):
                    hc = (d * SSD_HEADS + 2 * j, d * SSD_HEADS + 2 * j + 1)
                    col = [jnp.broadcast_to(cum[:, h:h + 1], (ck, LANES)) for h in hc]
                    dec.append([jnp.exp2(jnp.where(tris[d], col[q] - src_t[hc[q]:hc[q] + 1, :], NEG_BIG))
                                for q in range(2)])
                    ecs.append(jnp.exp2(jnp.where(lane_lo, col[0], col[1])))
                ms = [(cb * (dec[0][q] + dec[1][q])).astype(BF16) for q in range(2)]
                y = yacc[j, pl.ds(r0, ck), :] + _dot(jnp.concatenate(ms, axis=1), xcat_of(j, r0))
                hin = jnp.concatenate([sbuf[j, pl.ds(r0, ck), :], sbuf[nx + j, pl.ds(r0, ck), :]], axis=1)
                yoff = _dot(cm, hin)
                y = y + yoff[:, 0:LANES] * ecs[0] + yoff[:, LANES:2 * LANES] * ecs[1]
                if is_ctx:
                    yc_ref[0, pl.ds(r0, ck), j * LANES:(j + 1) * LANES] = y
                else:
                    y0 = pl.multiple_of((kc - nc_ctx) * ck, ck)
                    ybuf[pl.ds(y0, ck), j * LANES:(j + 1) * LANES] = y

    if need_ctx:
        for kc in range(nc_ctx):
            pass3(kc, True)

    @pl.when(b > 0)
    def _():
        for cp in out_copies(b - 1):
            cp.wait()

    def lat_body(c, carry):
        pass3(nc_ctx + c, False)
        return carry

    lax.fori_loop(0, nc_lat, lat_body, 0, unroll=8)

    for cp in out_copies(b):
        cp.start()

    @pl.when(b == nb - 1)
    def _():
        for cp in out_copies(b):
            cp.wait()


def _ssd(xd_l, xd_c, cw, cb, dtb, alog, dsk, *, need_ctx):
    b, t_lat, width = xd_l.shape
    t_ctx = xd_c.shape[1]
    nx = SSD_HEADS * SSD_HEAD_DIM // LANES
    ng = SSD_GROUPS
    inner = nx * LANES
    tot = t_lat + t_ctx
    rows = t_lat // GRID_W
    kern = functools.partial(_ssd_kernel, t_lat=t_lat, t_ctx=t_ctx, need_ctx=need_ctx)
    full = lambda a: pl.BlockSpec(a.shape, lambda i: (0,) * a.ndim)
    hbm = pl.BlockSpec(memory_space=pl.ANY)
    out_shape = [jax.ShapeDtypeStruct((b, rows, GRID_W, inner), F32)]
    out_specs = [hbm]
    if need_ctx:
        out_shape.append(jax.ShapeDtypeStruct((b, t_ctx, inner), F32))
        out_specs.append(pl.BlockSpec((1, t_ctx, inner), lambda i: (i, 0, 0)))
    res = pl.pallas_call(
        kern,
        out_shape=tuple(out_shape),
        grid=(b,),
        in_specs=[hbm, hbm, full(cw), full(cb), full(dtb), full(alog), full(dsk)],
        out_specs=tuple(out_specs),
        scratch_shapes=[
            pltpu.VMEM((2, tot + 3 * SUBLANES, width), F32),
            pltpu.VMEM((t_lat, inner), F32),
            pltpu.SemaphoreType.DMA((2,)),
            pltpu.SemaphoreType.DMA((1,)),
            pltpu.VMEM((nx + 2 * ng, SSD_CHUNK + 2 * SUBLANES, LANES), F32),
            pltpu.VMEM((2 * nx, tot, LANES), BF16),
            pltpu.VMEM((ng, tot, LANES), BF16),
            pltpu.VMEM((ng, tot, LANES), BF16),
            pltpu.VMEM((tot, LANES), F32),
            pltpu.VMEM((tot // SSD_CHUNK * 2 * SSD_HEADS, LANES), F32),
            pltpu.VMEM((tot // SSD_CHUNK * SUBLANES, LANES), F32),
            pltpu.VMEM((2 * nx, tot, LANES), BF16),
            pltpu.VMEM((2 * nx, SSD_STATE, LANES), F32),
            pltpu.VMEM((nx, tot, LANES), F32),
        ],
        compiler_params=_cparams("arbitrary"),
        name="ssd",
    )(xd_l.reshape(b, rows, GRID_W, width), xd_c, cw, cb, dtb, alog, dsk)
    y_l = res[0].reshape(b, t_lat, inner)
    return (y_l, res[1]) if need_ctx else (y_l, None)


MLP_FF_CHUNK = 1024


def _outmlp_body(lru_raw, refs, fg_ref, o_ref):
    x_ref, lg_ref, ssd_ref, z_ref, mod_ref, sg_ref, n2_ref, wo_ref, w1_ref, w2_ref = refs
    lw = z_ref.shape[-1]
    gw = lw // SSD_GROUPS
    dff = w1_ref.shape[1]
    parts = []
    for g in range(SSD_GROUPS):
        y = ssd_ref[0, :, g * gw:(g + 1) * gw] * _silu(z_ref[0, :, g * gw:(g + 1) * gw])
        y = y * lax.rsqrt(jnp.mean(y * y, axis=-1, keepdims=True) + EPS)
        parts.append(y * sg_ref[:, g * gw:(g + 1) * gw])
    ssd_n = jnp.concatenate(parts, axis=1).astype(BF16)
    lru = (lru_raw * _gelu_tanh(lg_ref[0])).astype(BF16)
    y = _dot(lru, wo_ref[0:lw, :]) + _dot(ssd_n, wo_ref[lw:2 * lw, :])
    x1 = x_ref[0] + mod_ref[0, 2:3, :] * y
    h = (_rmsnorm(x1, n2_ref[...]) * (1.0 + mod_ref[0, 4:5, :]) + mod_ref[0, 3:4, :]).astype(BF16)
    acc = jnp.zeros_like(x1)
    for j in range(dff // MLP_FF_CHUNK):
        a = _dot(h, w1_ref[:, j * MLP_FF_CHUNK:(j + 1) * MLP_FF_CHUNK])
        a = jnp.square(jnp.maximum(a, 0.0)).astype(BF16)
        acc = acc + _dot(a, w2_ref[j * MLP_FF_CHUNK:(j + 1) * MLP_FF_CHUNK, :])
    x2 = x1 + mod_ref[0, 5:6, :] * acc
    if fg_ref is not None:
        x2 = _rmsnorm(x2, fg_ref[...])
    o_ref[0] = x2


def _outmlp_kernel(*args, final_norm):
    x_ref, lru_ref, *rest = args
    if final_norm:
        *refs, fg_ref, o_ref = rest
    else:
        *refs, o_ref = rest
        fg_ref = None
    _outmlp_body(lru_ref[0], (x_ref, *refs), fg_ref, o_ref)


def _outmlp(x, lru, lg, ssd, z, mod, mod_row, sg, n2, wo, w1, w2, layer, fg, *, tm):
    b, t, d = x.shape
    lw = lru.shape[-1]
    final_norm = fg is not None
    kern = functools.partial(_outmlp_kernel, final_norm=final_norm)
    tok = lambda n: pl.BlockSpec((1, tm, n), lambda i, j: (i, j, 0))
    res = lambda a: pl.BlockSpec((None,) + a.shape[1:], lambda i, j: (layer, 0, 0), pipeline_mode=pl.Buffered(1))
    row = lambda n: pl.BlockSpec((1, n), lambda i, j: (0, 0))
    ins = [x, lru, lg, ssd, z, mod, sg, n2, wo, w1, w2]
    in_specs = [tok(d), tok(lw), tok(lw), tok(lw), tok(lw),
                pl.BlockSpec((1, N_MOD, d), lambda i, j: (mod_row(i), 0, 0)),
                row(lw), row(d), res(wo), res(w1), res(w2)]
    if final_norm:
        ins.append(fg)
        in_specs.append(row(d))
    return pl.pallas_call(
        kern,
        out_shape=jax.ShapeDtypeStruct((b, t, d), F32),
        grid=(b, t // tm),
        in_specs=in_specs,
        out_specs=tok(d),
        compiler_params=_cparams("arbitrary", "arbitrary"),
        name="outmlp",
    )(*ins)


def _lru_lat_kernel(lx_ref, h0_ref, *rest, t_lat):
    *refs, y_ref, upad, ubuf, uint, hbuf, pbuf, ybuf = rest
    seg = t_lat // SUBLANES

    def write_out(c, s, rows):
        y_ref[0, s * seg:(s + 1) * seg, c * LANES:(c + 1) * LANES] = rows

    h0 = [h0_ref[0, d:d + 1, c * LANES:(c + 1) * LANES] for c in range(LRU_TILES) for d in range(2)]
    _lru_seq(t_lat, lx_ref.at[0], write_out, h0, refs,
             (upad, ubuf, uint, hbuf, pbuf, ybuf), ntile=LRU_TILES, straight_line=True)


def _lru_lat(lx, h0, conv_w, conv_b, wa_bd, wx_bd, ba, bx, lam):
    b, t, w = lx.shape
    cwid = LRU_TILES * LANES
    seq = lambda r: pl.BlockSpec((1, r, cwid), lambda i, j: (i, 0, j))
    par = lambda r: pl.BlockSpec((r, cwid), lambda i, j: (0, j))
    gate = pl.BlockSpec((2, LRU_TILES, LANES, LANES), lambda i, j: (0, j, 0, 0))
    return pl.pallas_call(
        functools.partial(_lru_lat_kernel, t_lat=t),
        out_shape=jax.ShapeDtypeStruct((b, t, w), F32),
        grid=(b, w // cwid),
        in_specs=[seq(t), seq(2), par(CONV_K), par(1), gate, gate, par(2), par(2), par(2)],
        out_specs=seq(t),
        scratch_shapes=_lru_scratch(t, LRU_TILES),
        compiler_params=_cparams("arbitrary", "arbitrary"),
        name="rglru_lat",
    )(lx, h0, conv_w, conv_b, wa_bd, wx_bd, ba, bx, lam)


def _pair_block_diag(w):
    two, h, k, _ = w.shape
    w = w.reshape(two, h // 2, 2, k, k)
    z = jnp.zeros_like(w[:, :, 0])
    top = jnp.concatenate([w[:, :, 0], z], axis=-1)
    bot = jnp.concatenate([z, w[:, :, 1]], axis=-1)
    return jnp.concatenate([top, bot], axis=-2)


def _pad_lanes(a):
    return jnp.pad(a, [(0, 0)] * (a.ndim - 1) + [(0, LANES - a.shape[-1])])


def kernel(x, c, ctx, c_ctx, ada_w, ada_b, norm1_g, norm2_g, w_in, lru_conv_w, lru_conv_b, lru_wa, lru_ba,
           lru_wx, lru_bx, lru_lambda, ssd_conv_w, ssd_conv_b, ssd_dt_bias, ssd_a_log, ssd_d, ssd_norm_g,
           w_out, mlp_w1, mlp_w2, final_g):
    bsz, t_lat, d = x.shape
    t_ctx = ctx.shape[1]
    depth = ada_w.shape[0]
    lw = lru_conv_w.shape[-1]
    nxbc = ssd_conv_w.shape[-1]
    ndt = 2 * SSD_HEADS
    nscan = lw + nxbc + ndt
    assert bsz + 1 <= MOD_ROWS and t_lat % (GRID_W * SUBLANES) == 0 and t_ctx % SSD_CHUNK == 0

    cvec = jnp.zeros((MOD_ROWS, d), F32).at[:bsz].set(c).at[bsz].set(c_ctx)
    mod = _modulation(cvec, ada_w, ada_b).reshape(depth, MOD_ROWS, N_MOD, d)

    lat_row = lambda i: i
    ctx_row = lambda i: bsz
    tm_ctx = t_ctx
    nslab = nxbc // LANES
    nxd = nxbc + LANES

    w_re = jnp.concatenate(
        [w_in[:, :, :lw + nxbc], _pad_lanes(w_in[:, :, lw + nxbc:nscan]), w_in[:, :, nscan:]], axis=2).astype(BF16)
    wo = w_out.astype(BF16)
    w1 = mlp_w1.astype(BF16)
    w2 = mlp_w2.astype(BF16)

    for l in range(depth):
        need_ctx = l < depth - 1
        g1 = norm1_g[l].reshape(1, d)
        lx_l, xd_l, lg_l, z_l = _inproj(x, mod[l], lat_row, g1, w_re, l, lw=lw, nxd=nxd, tm=INPROJ_ROWS)
        lx_c, xd_c, lg_c, z_c = _inproj(ctx, mod[l], ctx_row, g1, w_re, l, lw=lw, nxd=nxd, tm=tm_ctx)

        lru_par = (0.5 * lru_conv_w[l], 0.5 * lru_conv_b[l].reshape(1, lw),
                   _pair_block_diag(lru_wa[l]).astype(BF16), _pair_block_diag(lru_wx[l]).astype(BF16),
                   0.5 * lru_ba[l], 0.5 * lru_bx[l], lru_lambda[l])
        h0_c, lru_c = _lru_ctx(lx_c, *lru_par, need_ctx=need_ctx)

        cw = 0.5 * ssd_conv_w[l].reshape(CONV_K, nslab, LANES).transpose(1, 0, 2)
        cb = 0.5 * ssd_conv_b[l].reshape(nslab, 1, LANES)
        dtb = _pad_lanes(ssd_dt_bias[l].reshape(1, ndt))
        alog = _pad_lanes(ssd_a_log[l].reshape(1, ndt))
        dsk = jnp.repeat(ssd_d[l], SSD_HEAD_DIM).reshape(lw // LANES, 1, LANES)
        ssd_l, ssd_c = _ssd(xd_l, xd_c, cw, cb, dtb, alog, dsk, need_ctx=need_ctx)

        sg = ssd_norm_g[l].reshape(1, lw)
        n2 = norm2_g[l].reshape(1, d)
        fg = final_g.reshape(1, d) if l == depth - 1 else None
        lru_l = _lru_lat(lx_l, h0_c, *lru_par)
        x = _outmlp(x, lru_l, lg_l, ssd_l, z_l, mod[l], lat_row, sg, n2, wo, w1, w2, l, fg, tm=MLP_ROWS)
        if need_ctx:
            ctx = _outmlp(ctx, lru_c, lg_c, ssd_c, z_c, mod[l], ctx_row, sg, n2, wo, w1, w2, l, None,
                          tm=tm_ctx)
    return x
```

```python
import functools
import math

import jax
import jax.numpy as jnp
from jax import lax
from jax.experimental import pallas as pl
from jax.experimental.pallas import tpu as pltpu

F32 = jnp.float32
BF16 = jnp.bfloat16

EPS = 1e-6
GRID_W = 64
LRU_C = 8.0
SSD_HEAD_DIM = 64
SSD_HEADS = 8
SSD_GROUPS = 2
SSD_STATE = 128
SSD_CHUNK = 128
CONV_K = 4
N_MOD = 6
LANES = 128
SUBLANES = 8
MOD_ROWS = 16
VMEM_LIMIT = 56 * 1024 * 1024
NEG_BIG = -1e30
MOD_COLS = 1536
INPROJ_ROWS = 1024
MLP_ROWS = 512


def _sigmoid(x):
    return 0.5 * (jnp.tanh(0.5 * x) + 1.0)


def _silu(x):
    return x * _sigmoid(x)


def _softplus(x):
    return jnp.maximum(x, 0.0) + jnp.log1p(jnp.exp(-jnp.abs(x)))


def _gelu_tanh(x):
    c = math.sqrt(2.0 / math.pi)
    return (0.5 * x) * (1.0 + jnp.tanh(x * (c + (c * 0.044715) * (x * x))))


def _rmsnorm(x, g):
    return x * lax.rsqrt(jnp.mean(x * x, axis=-1, keepdims=True) + EPS) * g


def _dot(a, b):
    return jnp.dot(a, b, preferred_element_type=F32)


def _cparams(*sem):
    return pltpu.CompilerParams(dimension_semantics=sem, vmem_limit_bytes=VMEM_LIMIT)


def _mod_kernel(c_ref, w_ref, b_ref, o_ref):
    s = _silu(c_ref[...]).astype(BF16)
    o_ref[0] = _dot(s, w_ref[0].astype(BF16)) + b_ref[0]


def _modulation(cvec, ada_w, ada_b):
    depth, d, n = ada_w.shape
    tn = MOD_COLS
    return pl.pallas_call(
        _mod_kernel,
        out_shape=jax.ShapeDtypeStruct((depth, MOD_ROWS, n), F32),
        grid=(depth, n // tn),
        in_specs=[
            pl.BlockSpec((MOD_ROWS, d), lambda l, j: (0, 0)),
            pl.BlockSpec((1, d, tn), lambda l, j: (l, 0, j)),
            pl.BlockSpec((1, 1, tn), lambda l, j: (l, 0, j)),
        ],
        out_specs=pl.BlockSpec((1, MOD_ROWS, tn), lambda l, j: (l, 0, j)),
        compiler_params=_cparams("arbitrary", "arbitrary"),
        name="modulation",
    )(cvec, ada_w, ada_b.reshape(depth, 1, n))


def _inproj_kernel(x_ref, mod_ref, g_ref, w_ref, lx_ref, xd_ref, lg_ref, z_ref, *, lw, nxd):
    x = x_ref[0]
    h = _rmsnorm(x, g_ref[...]) * (1.0 + mod_ref[0, 1:2, :]) + mod_ref[0, 0:1, :]
    h = h.astype(BF16)
    lx_ref[0] = _dot(h, w_ref[:, 0:lw])
    xd_ref[0] = _dot(h, w_ref[:, lw:lw + nxd])
    o = lw + nxd
    lg_ref[0] = _dot(h, w_ref[:, o:o + lw])
    z_ref[0] = _dot(h, w_ref[:, o + lw:o + 2 * lw])


def _inproj(x, mod, mod_row, g, w, layer, *, lw, nxd, tm):
    b, t, d = x.shape
    kern = functools.partial(_inproj_kernel, lw=lw, nxd=nxd)
    tok = lambda n: pl.BlockSpec((1, tm, n), lambda i, j: (i, j, 0))
    return pl.pallas_call(
        kern,
        out_shape=(
            jax.ShapeDtypeStruct((b, t, lw), F32),
            jax.ShapeDtypeStruct((b, t, nxd), F32),
            jax.ShapeDtypeStruct((b, t, lw), F32),
            jax.ShapeDtypeStruct((b, t, lw), F32),
        ),
        grid=(b, t // tm),
        in_specs=[
            tok(d),
            pl.BlockSpec((1, N_MOD, d), lambda i, j: (mod_row(i), 0, 0)),
            pl.BlockSpec((1, d), lambda i, j: (0, 0)),
            pl.BlockSpec((None,) + w.shape[1:], lambda i, j: (layer, 0, 0), pipeline_mode=pl.Buffered(1)),
        ],
        out_specs=(tok(lw), tok(nxd), tok(lw), tok(lw)),
        compiler_params=_cparams("arbitrary", "arbitrary"),
        name="inproj",
    )(x, mod, g, w)


LRU_ROWS = 256
LRU_TILES = 2


def _lru_seq(t, u_ref, write_out, h0, refs, scr, *, ntile, straight_line=False):
    cw_ref, cb_ref, wa_ref, wx_ref, ba_ref, bx_ref, lam_ref = refs
    upad, ubuf, uint, hbuf, pbuf, ybuf = scr
    seg = t // SUBLANES
    pitch = seg + 4
    nblk = max(t // LRU_ROWS, 1)
    rb = t // nblk
    nv = rb // SUBLANES
    tiles = [slice(c * LANES, (c + 1) * LANES) for c in range(ntile)]
    need_out = write_out is not None

    zeros8 = jnp.zeros((SUBLANES, LANES), F32)
    ones8 = jnp.ones((SUBLANES, LANES), F32)
    for c, ln in enumerate(tiles):
        upad[c, 0:SUBLANES, :] = zeros8
        upad[c, SUBLANES:SUBLANES + t, :] = u_ref[:, ln]
        upad[c, SUBLANES + t:2 * SUBLANES + t, :] = zeros8
        cw = cw_ref[:, ln]
        for s in range(SUBLANES):
            acc = cb_ref[:, ln]
            for tap in range(CONV_K):
                r = SUBLANES + s * seg + tap - 1
                acc = acc + cw[tap:tap + 1, :] * upad[c, r:r + seg, :]
            ubuf[c, s * pitch:s * pitch + seg, :] = acc
        for tau in range(seg):
            uint[c, tau * SUBLANES:(tau + 1) * SUBLANES, :] = ubuf[c, pl.ds(tau, SUBLANES, stride=pitch), :]

    row = lax.broadcasted_iota(jnp.int32, (SUBLANES, LANES), 0)
    kdec = [[(0.5 * LRU_C) * _softplus(-lam_ref[d:d + 1, ln]) for d in range(2)] for ln in tiles]
    chains = [(c, d) for c in range(ntile) for d in range(2)]

    def row0(blk):
        return blk * rb if isinstance(blk, int) else pl.multiple_of(blk * rb, rb)

    def blk_body(i, carry):
        carry = list(carry)
        for q, (c, d) in enumerate(chains):
            ln = tiles[c]
            h, p = carry[2 * q], carry[2 * q + 1]
            r0 = row0(i if d == 0 else nblk - 1 - i)
            u = uint[c, pl.ds(r0, rb), :]
            ub = u.astype(BF16)
            nla = kdec[c][d] * jnp.tanh(_dot(ub, wa_ref[d, c]) + ba_ref[d:d + 1, ln]) + kdec[c][d]
            gate = jnp.tanh(_dot(ub, wx_ref[d, c]) + bx_ref[d:d + 1, ln]) + 1.0
            a = jnp.exp2(nla * (-math.log2(math.e)))
            v = jnp.tanh(nla) * (1.0 + a * a)
            root = jnp.where(v > 0.0, v * lax.rsqrt(v), 0.0)
            bb = root * (gate * u)
            hs = [None] * nv
            ps = [None] * nv
            order = range(nv) if d == 0 else range(nv - 1, -1, -1)
            for k in order:
                ak = a[k * SUBLANES:(k + 1) * SUBLANES, :]
                h = ak * h + bb[k * SUBLANES:(k + 1) * SUBLANES, :]
                p = ak * p
                hs[k] = h
                ps[k] = p
            hbuf[q, pl.ds(r0, rb), :] = jnp.concatenate(hs, axis=0)
            pbuf[q, pl.ds(r0, rb), :] = jnp.concatenate(ps, axis=0)
            carry[2 * q], carry[2 * q + 1] = h, p
        return tuple(carry)

    def loop(body, init):
        if straight_line:
            for i in range(nblk):
                init = body(i, init)
            return init
        return lax.fori_loop(0, nblk, body, init)

    hp = loop(blk_body, (zeros8, ones8) * len(chains))
    finals = []
    cmats = []
    for q, (c, d) in enumerate(chains):
        h, p = hp[2 * q], hp[2 * q + 1]
        st = h0[q]
        cmat = zeros8
        order = range(SUBLANES) if d == 0 else range(SUBLANES - 1, -1, -1)
        for s in order:
            cmat = jnp.where(row == s, jnp.broadcast_to(st, (SUBLANES, LANES)), cmat)
            st = h[s:s + 1, :] + p[s:s + 1, :] * st
        finals.append(st)
        cmats.append(cmat)

    if need_out:
        def out_body(i, carry):
            r0 = row0(i)
            for c in range(ntile):
                y = None
                for q in (2 * c, 2 * c + 1):
                    t1 = hbuf[q, pl.ds(r0, rb), :].reshape(nv, SUBLANES, LANES)
                    t2 = pbuf[q, pl.ds(r0, rb), :].reshape(nv, SUBLANES, LANES) * cmats[q][None]
                    y = t1 + t2 if y is None else y + t1 + t2
                ybuf[c, pl.ds(r0, rb), :] = y.reshape(rb, LANES)
            return carry

        loop(out_body, 0)
        for c in range(ntile):
            for tau in range(seg):
                ubuf[c, pl.ds(tau, SUBLANES, stride=pitch), :] = ybuf[c, tau * SUBLANES:(tau + 1) * SUBLANES, :]
            for s in range(SUBLANES):
                write_out(c, s, ubuf[c, s * pitch:s * pitch + seg, :])
    return finals


def _lru_scratch(t, ntile):
    pitch_rows = SUBLANES * (t // SUBLANES + 4) + SUBLANES
    return [
        pltpu.VMEM((ntile, t + 2 * SUBLANES, LANES), F32),
        pltpu.VMEM((ntile, pitch_rows, LANES), F32),
        pltpu.VMEM((ntile, t, LANES), F32),
        pltpu.VMEM((2 * ntile, t, LANES), F32),
        pltpu.VMEM((2 * ntile, t, LANES), F32),
        pltpu.VMEM((ntile, t, LANES), F32),
    ]


def _lru_ctx_kernel(*args, t_ctx, need_ctx):
    if need_ctx:
        (uc_ref, *refs, hf_ref, yc_ref, upad, ubuf, uint, hbuf, pbuf, ybuf) = args
    else:
        (uc_ref, *refs, hf_ref, upad, ubuf, uint, hbuf, pbuf, ybuf) = args
    seg = t_ctx // SUBLANES

    def write_out(c, s, rows):
        yc_ref[0, s * seg:(s + 1) * seg, c * LANES:(c + 1) * LANES] = rows

    zero = jnp.zeros((1, LANES), F32)
    finals = _lru_seq(t_ctx, uc_ref.at[0], write_out if need_ctx else None, (zero,) * (2 * LRU_TILES), refs,
                      (upad, ubuf, uint, hbuf, pbuf, ybuf), ntile=LRU_TILES)
    for q, st in enumerate(finals):
        c, d = divmod(q, 2)
        hf_ref[0, d:d + 1, c * LANES:(c + 1) * LANES] = st


def _lru_ctx(lx_c, conv_w, conv_b, wa_bd, wx_bd, ba, bx, lam, *, need_ctx):
    b, t_ctx, w = lx_c.shape
    cwid = LRU_TILES * LANES
    seq = lambda t: pl.BlockSpec((1, t, cwid), lambda i, j: (i, 0, j))
    par = lambda r: pl.BlockSpec((r, cwid), lambda i, j: (0, j))
    gate = pl.BlockSpec((2, LRU_TILES, LANES, LANES), lambda i, j: (0, j, 0, 0))
    kern = functools.partial(_lru_ctx_kernel, t_ctx=t_ctx, need_ctx=need_ctx)
    out_shape = [jax.ShapeDtypeStruct((b, 2, w), F32)]
    out_specs = [seq(2)]
    if need_ctx:
        out_shape.append(jax.ShapeDtypeStruct((b, t_ctx, w), F32))
        out_specs.append(seq(t_ctx))
    res = pl.pallas_call(
        kern,
        out_shape=tuple(out_shape),
        grid=(b, w // cwid),
        in_specs=[seq(t_ctx), par(CONV_K), par(1), gate, gate, par(2), par(2), par(2)],
        out_specs=tuple(out_specs),
        scratch_shapes=_lru_scratch(t_ctx, LRU_TILES),
        compiler_params=_cparams("arbitrary", "arbitrary"),
        name="rglru_ctx",
    )(lx_c, conv_w, conv_b, wa_bd, wx_bd, ba, bx, lam)
    return (res[0], res[1]) if need_ctx else (res[0], None)


def _ssd_kernel(*args, t_lat, t_ctx, need_ctx):
    if need_ctx:
        (xl_hbm, xc_hbm, cw_ref, cb_ref, dtb_ref, alog_ref, dsk_ref, yl_hbm, yc_ref,
         xg, ybuf, sem_in, sem_out, stg, xm, scc, sbt, cumb, srct, etb, sbuf, hst, yacc) = args
    else:
        (xl_hbm, xc_hbm, cw_ref, cb_ref, dtb_ref, alog_ref, dsk_ref, yl_hbm,
         xg, ybuf, sem_in, sem_out, stg, xm, scc, sbt, cumb, srct, etb, sbuf, hst, yacc) = args
        yc_ref = None
    ck = SSD_CHUNK
    rows = t_lat // GRID_W
    nc_ctx = t_ctx // ck
    nc_lat = t_lat // ck
    nx = SSD_HEADS * SSD_HEAD_DIM // LANES
    ng = SSD_GROUPS
    nslab = nx + 2 * ng
    halo = SUBLANES

    ri = lax.broadcasted_iota(jnp.int32, (ck, ck), 0)
    ci = lax.broadcasted_iota(jnp.int32, (ck, ck), 1)
    lane_lo = ci < SSD_HEAD_DIM

    b = pl.program_id(0)
    nb = pl.num_programs(0)
    slot = lax.rem(b, 2)
    ctx0 = halo
    lat0 = ctx0 + t_ctx + halo
    width = xg.shape[-1]

    def in_copies(bi, sl):
        cps = [pltpu.make_async_copy(xc_hbm.at[bi], xg.at[sl, pl.ds(ctx0, t_ctx), :], sem_in.at[sl])]
        for w in range(GRID_W):
            cps.append(pltpu.make_async_copy(
                xl_hbm.at[bi, :, w, :], xg.at[sl, pl.ds(lat0 + w * rows, rows), :], sem_in.at[sl]))
        return cps

    def out_copies(bi):
        return [pltpu.make_async_copy(ybuf.at[pl.ds(w * rows, rows), :], yl_hbm.at[bi, :, w, :], sem_out.at[0])
                for w in range(GRID_W)]

    @pl.when(b == 0)
    def _():
        zpad = jnp.zeros((halo, width), F32)
        for sl in range(2):
            for r in (0, ctx0 + t_ctx, lat0 + t_lat):
                xg[sl, r:r + halo, :] = zpad
        for cp in in_copies(0, 0):
            cp.start()

    @pl.when(b + 1 < nb)
    def _():
        for cp in in_copies(b + 1, 1 - slot):
            cp.start()

    for cp in in_copies(b, slot):
        cp.wait()

    xs = xg.at[slot]

    def src_row(k):
        return pl.multiple_of(jnp.where(k < nc_ctx, ctx0 + k * ck, lat0 + (k - nc_ctx) * ck), SUBLANES)

    def conv_silu(src0, j):
        stg[j] = xs[pl.ds(src0 - halo, ck + 2 * halo), j * LANES:(j + 1) * LANES]
        cw = cw_ref[j]
        half = cb_ref[j]
        for tap in range(CONV_K):
            half = half + cw[tap:tap + 1, :] * stg[j, halo + tap - 1:halo + tap - 1 + ck, :]
        return half * jnp.tanh(half) + half

    dtb = dtb_ref[...]
    lane_ok = lax.broadcasted_iota(jnp.int32, (1, LANES), 1) < 2 * SSD_HEADS
    nega2 = jnp.where(lane_ok, -jnp.exp(alog_ref[...]) * math.log2(math.e), 0.0)

    nck = nc_ctx + nc_lat
    nh2 = 2 * SSD_HEADS
    hpp = SSD_HEADS // ng // 2
    dir1_col = lax.broadcasted_iota(jnp.int32, (1, LANES), 1) >= SSD_HEADS
    tris = (ri >= ci, ri <= ci)

    def xcat_of(j, r0):
        return jnp.concatenate([xm[j, pl.ds(r0, ck), :], xm[nx + j, pl.ds(r0, ck), :]], axis=0)

    def pass1(kc, carry):
        r0 = pl.multiple_of(kc * ck, ck)
        src0 = src_row(kc)
        dt = _softplus(xs[pl.ds(src0, ck), nslab * LANES:(nslab + 1) * LANES] + dtb)
        la = dt * nega2
        cumf = la
        sh = 1
        while sh < ck:
            if sh < SUBLANES:
                cumf = cumf + jnp.where(ri >= sh, pltpu.roll(cumf, sh, 0), 0.0)
            else:
                cumf = cumf + jnp.concatenate([jnp.zeros((sh, LANES), F32), cumf[0:ck - sh, :]], axis=0)
            sh *= 2
        tot = cumf[ck - 1:ck, :]
        cum = jnp.where(dir1_col, tot - cumf + la, cumf)
        ldt = jnp.log2(dt)
        cumb[pl.ds(r0, ck), :] = cum
        srct[pl.ds(pl.multiple_of(kc * nh2, nh2), nh2), :] = (cum - ldt).T[0:nh2, :]
        wgt_t = jnp.exp2((tot - cum + ldt).T[0:nh2, :])
        etb[pl.ds(pl.multiple_of(kc * SUBLANES, SUBLANES), SUBLANES), :] = jnp.broadcast_to(
            jnp.exp2(tot), (SUBLANES, LANES))
        for g in range(ng):
            scc[g, pl.ds(r0, ck), :] = conv_silu(src0, nx + ng + g).astype(BF16)
            btf = conv_silu(src0, nx + g).T
            sbt[g, pl.ds(r0, ck), :] = btf.astype(BF16)
            for pr in range(hpp):
                j = g * hpp + pr
                x = conv_silu(src0, j)
                yacc[j, pl.ds(r0, ck), :] = dsk_ref[j] * x
                xb = x.astype(BF16)
                zb = jnp.zeros_like(xb)
                xlo = jnp.where(lane_lo, xb, zb)
                xhi = jnp.where(lane_lo, zb, xb)
                xm[j, pl.ds(r0, ck), :] = xlo
                xm[nx + j, pl.ds(r0, ck), :] = xhi
                btw = []
                for d in range(2):
                    hc = (d * SSD_HEADS + 2 * j, d * SSD_HEADS + 2 * j + 1)
                    btw.append(jnp.concatenate([(btf * wgt_t[h:h + 1, :]).astype(BF16) for h in hc], axis=1))
                s = _dot(jnp.concatenate(btw, axis=0), jnp.concatenate([xlo, xhi], axis=0)).astype(BF16)
                sbuf[j, pl.ds(r0, ck), :] = s[0:SSD_STATE, :]
                sbuf[nx + j, pl.ds(r0, ck), :] = s[SSD_STATE:2 * SSD_STATE, :]
        return carry

    lax.fori_loop(0, nck, pass1, 0, unroll=6)

    hst[...] = jnp.zeros(hst.shape, F32)

    def pass2(i, carry):
        for d in range(2):
            if d == 0:
                kc = i
            else:
                kc = jnp.where(i < nc_ctx, nc_ctx - 1 - i, nck - 1 - (i - nc_ctx))
            r0 = pl.multiple_of(kc * ck, ck)
            e = etb[pl.ds(pl.multiple_of(kc * SUBLANES, SUBLANES), 1), :]
            for j in range(nx):
                hc = (d * SSD_HEADS + 2 * j, d * SSD_HEADS + 2 * j + 1)
                et = jnp.where(lane_lo[0:1, :], jnp.broadcast_to(e[:, hc[0]:hc[0] + 1], (1, LANES)),
                               jnp.broadcast_to(e[:, hc[1]:hc[1] + 1], (1, LANES)))
                h = hst[d * nx + j]
                s = sbuf[d * nx + j, pl.ds(r0, ck), :].astype(F32)
                sbuf[d * nx + j, pl.ds(r0, ck), :] = h.astype(BF16)
                hst[d * nx + j] = h * et + s
        return carry

    lax.fori_loop(0, nck, pass2, 0, unroll=3)

    def pass3(kc, is_ctx):
        r0 = kc * ck if isinstance(kc, int) else pl.multiple_of(kc * ck, ck)
        s0 = kc * nh2 if isinstance(kc, int) else pl.multiple_of(kc * nh2, nh2)
        cum = cumb[pl.ds(r0, ck), :]
        src_t = srct[pl.ds(s0, nh2), :]
        for g in range(ng):
            cm = scc[g, pl.ds(r0, ck), :]
            cb = _dot(cm, sbt[g, pl.ds(r0, ck), :])
            for pr in range(hpp):
                j = g * hpp + pr
                dec = []
                ecs = []
                for d in range(2):
                    hc = (d * SSD_HEADS + 2 * j, d * SSD_HEADS + 2 * j + 1)
                    col = [jnp.broadcast_to(cum[:, h:h + 1], (ck, LANES)) for h in hc]
                    dec.append([jnp.exp2(jnp.where(tris[d], col[q] - src_t[hc[q]:hc[q] + 1, :], NEG_BIG))
                                for q in range(2)])
                    ecs.append(jnp.exp2(jnp.where(lane_lo, col[0], col[1])))
                ms = [(cb * (dec[0][q] + dec[1][q])).astype(BF16) for q in range(2)]
                y = yacc[j, pl.ds(r0, ck), :] + _dot(jnp.concatenate(ms, axis=1), xcat_of(j, r0))
                hin = jnp.concatenate([sbuf[j, pl.ds(r0, ck), :], sbuf[nx + j, pl.ds(r0, ck), :]], axis=1)
                yoff = _dot(cm, hin)
                y = y + yoff[:, 0:LANES] * ecs[0] + yoff[:, LANES:2 * LANES] * ecs[1]
                if is_ctx:
                    yc_ref[0, pl.ds(r0, ck), j * LANES:(j + 1) * LANES] = y
                else:
                    y0 = pl.multiple_of((kc - nc_ctx) * ck, ck)
                    ybuf[pl.ds(y0, ck), j * LANES:(j + 1) * LANES] = y

    if need_ctx:
        for kc in range(nc_ctx):
            pass3(kc, True)

    @pl.when(b > 0)
    def _():
        for cp in out_copies(b - 1):
            cp.wait()

    def lat_body(c, carry):
        pass3(nc_ctx + c, False)
        return carry

    lax.fori_loop(0, nc_lat, lat_body, 0, unroll=8)

    for cp in out_copies(b):
        cp.start()

    @pl.when(b == nb - 1)
    def _():
        for cp in out_copies(b):
            cp.wait()


def _ssd(xd_l, xd_c, cw, cb, dtb, alog, dsk, *, need_ctx):
    b, t_lat, width = xd_l.shape
    t_ctx = xd_c.shape[1]
    nx = SSD_HEADS * SSD_HEAD_DIM // LANES
    ng = SSD_GROUPS
    inner = nx * LANES
    tot = t_lat + t_ctx
    rows = t_lat // GRID_W
    kern = functools.partial(_ssd_kernel, t_lat=t_lat, t_ctx=t_ctx, need_ctx=need_ctx)
    full = lambda a: pl.BlockSpec(a.shape, lambda i: (0,) * a.ndim)
    hbm = pl.BlockSpec(memory_space=pl.ANY)
    out_shape = [jax.ShapeDtypeStruct((b, rows, GRID_W, inner), F32)]
    out_specs = [hbm]
    if need_ctx:
        out_shape.append(jax.ShapeDtypeStruct((b, t_ctx, inner), F32))
        out_specs.append(pl.BlockSpec((1, t_ctx, inner), lambda i: (i, 0, 0)))
    res = pl.pallas_call(
        kern,
        out_shape=tuple(out_shape),
        grid=(b,),
        in_specs=[hbm, hbm, full(cw), full(cb), full(dtb), full(alog), full(dsk)],
        out_specs=tuple(out_specs),
        scratch_shapes=[
            pltpu.VMEM((2, tot + 3 * SUBLANES, width), F32),
            pltpu.VMEM((t_lat, inner), F32),
            pltpu.SemaphoreType.DMA((2,)),
            pltpu.SemaphoreType.DMA((1,)),
            pltpu.VMEM((nx + 2 * ng, SSD_CHUNK + 2 * SUBLANES, LANES), F32),
            pltpu.VMEM((2 * nx, tot, LANES), BF16),
            pltpu.VMEM((ng, tot, LANES), BF16),
            pltpu.VMEM((ng, tot, LANES), BF16),
            pltpu.VMEM((tot, LANES), F32),
            pltpu.VMEM((tot // SSD_CHUNK * 2 * SSD_HEADS, LANES), F32),
            pltpu.VMEM((tot // SSD_CHUNK * SUBLANES, LANES), F32),
            pltpu.VMEM((2 * nx, tot, LANES), BF16),
            pltpu.VMEM((2 * nx, SSD_STATE, LANES), F32),
            pltpu.VMEM((nx, tot, LANES), F32),
        ],
        compiler_params=_cparams("arbitrary"),
        name="ssd",
    )(xd_l.reshape(b, rows, GRID_W, width), xd_c, cw, cb, dtb, alog, dsk)
    y_l = res[0].reshape(b, t_lat, inner)
    return (y_l, res[1]) if need_ctx else (y_l, None)


MLP_FF_CHUNK = 1024


def _outmlp_body(lru_raw, refs, fg_ref, o_ref):
    x_ref, lg_ref, ssd_ref, z_ref, mod_ref, sg_ref, n2_ref, wo_ref, w1_ref, w2_ref = refs
    lw = z_ref.shape[-1]
    gw = lw // SSD_GROUPS
    dff = w1_ref.shape[1]
    parts = []
    for g in range(SSD_GROUPS):
        y = ssd_ref[0, :, g * gw:(g + 1) * gw] * _silu(z_ref[0, :, g * gw:(g + 1) * gw])
        y = y * lax.rsqrt(jnp.mean(y * y, axis=-1, keepdims=True) + EPS)
        parts.append(y * sg_ref[:, g * gw:(g + 1) * gw])
    ssd_n = jnp.concatenate(parts, axis=1).astype(BF16)
    lru = (lru_raw * _gelu_tanh(lg_ref[0])).astype(BF16)
    y = _dot(lru, wo_ref[0:lw, :]) + _dot(ssd_n, wo_ref[lw:2 * lw, :])
    x1 = x_ref[0] + mod_ref[0, 2:3, :] * y
    h = (_rmsnorm(x1, n2_ref[...]) * (1.0 + mod_ref[0, 4:5, :]) + mod_ref[0, 3:4, :]).astype(BF16)
    acc = jnp.zeros_like(x1)
    for j in range(dff // MLP_FF_CHUNK):
        a = _dot(h, w1_ref[:, j * MLP_FF_CHUNK:(j + 1) * MLP_FF_CHUNK])
        a = jnp.square(jnp.maximum(a, 0.0)).astype(BF16)
        acc = acc + _dot(a, w2_ref[j * MLP_FF_CHUNK:(j + 1) * MLP_FF_CHUNK, :])
    x2 = x1 + mod_ref[0, 5:6, :] * acc
    if fg_ref is not None:
        x2 = _rmsnorm(x2, fg_ref[...])
    o_ref[0] = x2


def _outmlp_kernel(*args, final_norm):
    x_ref, lru_ref, *rest = args
    if final_norm:
        *refs, fg_ref, o_ref = rest
    else:
        *refs, o_ref = rest
        fg_ref = None
    _outmlp_body(lru_ref[0], (x_ref, *refs), fg_ref, o_ref)


def _outmlp(x, lru, lg, ssd, z, mod, mod_row, sg, n2, wo, w1, w2, layer, fg, *, tm):
    b, t, d = x.shape
    lw = lru.shape[-1]
    final_norm = fg is not None
    kern = functools.partial(_outmlp_kernel, final_norm=final_norm)
    tok = lambda n: pl.BlockSpec((1, tm, n), lambda i, j: (i, j, 0))
    res = lambda a: pl.BlockSpec((None,) + a.shape[1:], lambda i, j: (layer, 0, 0), pipeline_mode=pl.Buffered(1))
    row = lambda n: pl.BlockSpec((1, n), lambda i, j: (0, 0))
    ins = [x, lru, lg, ssd, z, mod, sg, n2, wo, w1, w2]
    in_specs = [tok(d), tok(lw), tok(lw), tok(lw), tok(lw),
                pl.BlockSpec((1, N_MOD, d), lambda i, j: (mod_row(i), 0, 0)),
                row(lw), row(d), res(wo), res(w1), res(w2)]
    if final_norm:
        ins.append(fg)
        in_specs.append(row(d))
    return pl.pallas_call(
        kern,
        out_shape=jax.ShapeDtypeStruct((b, t, d), F32),
        grid=(b, t // tm),
        in_specs=in_specs,
        out_specs=tok(d),
        compiler_params=_cparams("arbitrary", "arbitrary"),
        name="outmlp",
    )(*ins)


def _lru_lat_kernel(lx_ref, h0_ref, *rest, t_lat):
    *refs, y_ref, upad, ubuf, uint, hbuf, pbuf, ybuf = rest
    seg = t_lat // SUBLANES

    def write_out(c, s, rows):
        y_ref[0, s * seg:(s + 1) * seg, c * LANES:(c + 1) * LANES] = rows

    h0 = [h0_ref[0, d:d + 1, c * LANES:(c + 1) * LANES] for c in range(LRU_TILES) for d in range(2)]
    _lru_seq(t_lat, lx_ref.at[0], write_out, h0, refs,
             (upad, ubuf, uint, hbuf, pbuf, ybuf), ntile=LRU_TILES, straight_line=True)


def _lru_lat(lx, h0, conv_w, conv_b, wa_bd, wx_bd, ba, bx, lam):
    b, t, w = lx.shape
    cwid = LRU_TILES * LANES
    seq = lambda r: pl.BlockSpec((1, r, cwid), lambda i, j: (i, 0, j))
    par = lambda r: pl.BlockSpec((r, cwid), lambda i, j: (0, j))
    gate = pl.BlockSpec((2, LRU_TILES, LANES, LANES), lambda i, j: (0, j, 0, 0))
    return pl.pallas_call(
        functools.partial(_lru_lat_kernel, t_lat=t),
        out_shape=jax.ShapeDtypeStruct((b, t, w), F32),
        grid=(b, w // cwid),
        in_specs=[seq(t), seq(2), par(CONV_K), par(1), gate, gate, par(2), par(2), par(2)],
        out_specs=seq(t),
        scratch_shapes=_lru_scratch(t, LRU_TILES),
        compiler_params=_cparams("arbitrary", "arbitrary"),
        name="rglru_lat",
    )(lx, h0, conv_w, conv_b, wa_bd, wx_bd, ba, bx, lam)


def _pair_block_diag(w):
    two, h, k, _ = w.shape
    w = w.reshape(two, h // 2, 2, k, k)
    z = jnp.zeros_like(w[:, :, 0])
    top = jnp.concatenate([w[:, :, 0], z], axis=-1)
    bot = jnp.concatenate([z, w[:, :, 1]], axis=-1)
    return jnp.concatenate([top, bot], axis=-2)


def _pad_lanes(a):
    return jnp.pad(a, [(0, 0)] * (a.ndim - 1) + [(0, LANES - a.shape[-1])])


def kernel(x, c, ctx, c_ctx, ada_w, ada_b, norm1_g, norm2_g, w_in, lru_conv_w, lru_conv_b, lru_wa, lru_ba,
           lru_wx, lru_bx, lru_lambda, ssd_conv_w, ssd_conv_b, ssd_dt_bias, ssd_a_log, ssd_d, ssd_norm_g,
           w_out, mlp_w1, mlp_w2, final_g):
    bsz, t_lat, d = x.shape
    t_ctx = ctx.shape[1]
    depth = ada_w.shape[0]
    lw = lru_conv_w.shape[-1]
    nxbc = ssd_conv_w.shape[-1]
    ndt = 2 * SSD_HEADS
    nscan = lw + nxbc + ndt
    assert bsz + 1 <= MOD_ROWS and t_lat % (GRID_W * SUBLANES) == 0 and t_ctx % SSD_CHUNK == 0

    cvec = jnp.zeros((MOD_ROWS, d), F32).at[:bsz].set(c).at[bsz].set(c_ctx)
    mod = _modulation(cvec, ada_w, ada_b).reshape(depth, MOD_ROWS, N_MOD, d)

    lat_row = lambda i: i
    ctx_row = lambda i: bsz
    n_ctx = bsz * t_ctx
    in_rows = math.gcd(INPROJ_ROWS, n_ctx)
    mlp_rows = math.gcd(MLP_ROWS, n_ctx)

    def ctx_tiles(a, rows):
        return a.reshape(n_ctx // rows, rows, a.shape[-1])

    nslab = nxbc // LANES
    nxd = nxbc + LANES

    w_re = jnp.concatenate(
        [w_in[:, :, :lw + nxbc], _pad_lanes(w_in[:, :, lw + nxbc:nscan]), w_in[:, :, nscan:]], axis=2).astype(BF16)
    wo = w_out.astype(BF16)
    w1 = mlp_w1.astype(BF16)
    w2 = mlp_w2.astype(BF16)

    for l in range(depth):
        need_ctx = l < depth - 1
        g1 = norm1_g[l].reshape(1, d)
        lx_l, xd_l, lg_l, z_l = _inproj(x, mod[l], lat_row, g1, w_re, l, lw=lw, nxd=nxd, tm=INPROJ_ROWS)
        lx_c, xd_c, lg_c, z_c = (a.reshape(bsz, t_ctx, -1) for a in _inproj(
            ctx_tiles(ctx, in_rows), mod[l], ctx_row, g1, w_re, l, lw=lw, nxd=nxd, tm=in_rows))

        lru_par = (0.5 * lru_conv_w[l], 0.5 * lru_conv_b[l].reshape(1, lw),
                   _pair_block_diag(lru_wa[l]).astype(BF16), _pair_block_diag(lru_wx[l]).astype(BF16),
                   0.5 * lru_ba[l], 0.5 * lru_bx[l], lru_lambda[l])
        h0_c, lru_c = _lru_ctx(lx_c, *lru_par, need_ctx=need_ctx)

        cw = 0.5 * ssd_conv_w[l].reshape(CONV_K, nslab, LANES).transpose(1, 0, 2)
        cb = 0.5 * ssd_conv_b[l].reshape(nslab, 1, LANES)
        dtb = _pad_lanes(ssd_dt_bias[l].reshape(1, ndt))
        alog = _pad_lanes(ssd_a_log[l].reshape(1, ndt))
        dsk = jnp.repeat(ssd_d[l], SSD_HEAD_DIM).reshape(lw // LANES, 1, LANES)
        ssd_l, ssd_c = _ssd(xd_l, xd_c, cw, cb, dtb, alog, dsk, need_ctx=need_ctx)

        sg = ssd_norm_g[l].reshape(1, lw)
        n2 = norm2_g[l].reshape(1, d)
        fg = final_g.reshape(1, d) if l == depth - 1 else None
        lru_l = _lru_lat(lx_l, h0_c, *lru_par)
        x = _outmlp(x, lru_l, lg_l, ssd_l, z_l, mod[l], lat_row, sg, n2, wo, w1, w2, l, fg, tm=MLP_ROWS)
        if need_ctx:
            ctx = _outmlp(*(ctx_tiles(a, mlp_rows) for a in (ctx, lru_c, lg_c, ssd_c, z_c)), mod[l], ctx_row,
                          sg, n2, wo, w1, w2, l, None, tm=mlp_rows).reshape(bsz, t_ctx, d)
    return x
```

```python
import functools
import math

import jax
import jax.numpy as jnp
from jax import lax
from jax.experimental import pallas as pl
from jax.experimental.pallas import tpu as pltpu

F32 = jnp.float32
BF16 = jnp.bfloat16

EPS = 1e-6
GRID_W = 64
LRU_C = 8.0
SSD_HEAD_DIM = 64
SSD_HEADS = 8
SSD_GROUPS = 2
SSD_STATE = 128
SSD_CHUNK = 128
CONV_K = 4
N_MOD = 6
LANES = 128
SUBLANES = 8
MOD_ROWS = 16
VMEM_LIMIT = 56 * 1024 * 1024
NEG_BIG = -1e30
MOD_COLS = 1536
INPROJ_ROWS = 1024
MLP_ROWS = 512


def _sigmoid(x):
    return 0.5 * (jnp.tanh(0.5 * x) + 1.0)


def _silu(x):
    return x * _sigmoid(x)


def _softplus(x):
    return jnp.maximum(x, 0.0) + jnp.log1p(jnp.exp(-jnp.abs(x)))


def _gelu_tanh(x):
    c = math.sqrt(2.0 / math.pi)
    return (0.5 * x) * (1.0 + jnp.tanh(x * (c + (c * 0.044715) * (x * x))))


def _rmsnorm(x, g):
    return x * lax.rsqrt(jnp.mean(x * x, axis=-1, keepdims=True) + EPS) * g


def _dot(a, b):
    return jnp.dot(a, b, preferred_element_type=F32)


def _cparams(*sem):
    return pltpu.CompilerParams(dimension_semantics=sem, vmem_limit_bytes=VMEM_LIMIT)


def _mod_kernel(c_ref, w_ref, b_ref, o_ref):
    s = _silu(c_ref[...]).astype(BF16)
    o_ref[0] = _dot(s, w_ref[0].astype(BF16)) + b_ref[0]


def _modulation(cvec, ada_w, ada_b):
    depth, d, n = ada_w.shape
    tn = MOD_COLS
    return pl.pallas_call(
        _mod_kernel,
        out_shape=jax.ShapeDtypeStruct((depth, MOD_ROWS, n), F32),
        grid=(depth, n // tn),
        in_specs=[
            pl.BlockSpec((MOD_ROWS, d), lambda l, j: (0, 0)),
            pl.BlockSpec((1, d, tn), lambda l, j: (l, 0, j)),
            pl.BlockSpec((1, 1, tn), lambda l, j: (l, 0, j)),
        ],
        out_specs=pl.BlockSpec((1, MOD_ROWS, tn), lambda l, j: (l, 0, j)),
        compiler_params=_cparams("arbitrary", "arbitrary"),
        name="modulation",
    )(cvec, ada_w, ada_b.reshape(depth, 1, n))


def _inproj_kernel(x_ref, mod_ref, g_ref, w_ref, lx_ref, xd_ref, lg_ref, z_ref, *, lw, nxd):
    x = x_ref[0]
    h = _rmsnorm(x, g_ref[...]) * (1.0 + mod_ref[0, 1:2, :]) + mod_ref[0, 0:1, :]
    h = h.astype(BF16)
    lx_ref[0] = _dot(h, w_ref[:, 0:lw]).astype(lx_ref.dtype)
    xd_ref[0] = _dot(h, w_ref[:, lw:lw + nxd])
    o = lw + nxd
    lg_ref[0] = _dot(h, w_ref[:, o:o + lw]).astype(lg_ref.dtype)
    z_ref[0] = _dot(h, w_ref[:, o + lw:o + 2 * lw]).astype(z_ref.dtype)


def _inproj(x, mod, mod_row, g, w, layer, *, lw, nxd, tm):
    b, t, d = x.shape
    kern = functools.partial(_inproj_kernel, lw=lw, nxd=nxd)
    tok = lambda n: pl.BlockSpec((1, tm, n), lambda i, j: (i, j, 0))
    return pl.pallas_call(
        kern,
        out_shape=(
            jax.ShapeDtypeStruct((b, t, lw), BF16),
            jax.ShapeDtypeStruct((b, t, nxd), F32),
            jax.ShapeDtypeStruct((b, t, lw), BF16),
            jax.ShapeDtypeStruct((b, t, lw), BF16),
        ),
        grid=(b, t // tm),
        in_specs=[
            tok(d),
            pl.BlockSpec((1, N_MOD, d), lambda i, j: (mod_row(i), 0, 0)),
            pl.BlockSpec((1, d), lambda i, j: (0, 0)),
            pl.BlockSpec((None,) + w.shape[1:], lambda i, j: (layer, 0, 0), pipeline_mode=pl.Buffered(1)),
        ],
        out_specs=(tok(lw), tok(nxd), tok(lw), tok(lw)),
        compiler_params=_cparams("arbitrary", "arbitrary"),
        name="inproj",
    )(x, mod, g, w)


LRU_ROWS = 256
LRU_TILES = 2


def _lru_seq(t, u_ref, write_out, h0, refs, scr, *, ntile, straight_line=False):
    cw_ref, cb_ref, wa_ref, wx_ref, ba_ref, bx_ref, lam_ref = refs
    upad, ubuf, uint, hbuf, pbuf, ybuf = scr
    seg = t // SUBLANES
    pitch = seg + 4
    nblk = max(t // LRU_ROWS, 1)
    rb = t // nblk
    nv = rb // SUBLANES
    tiles = [slice(c * LANES, (c + 1) * LANES) for c in range(ntile)]
    need_out = write_out is not None

    zeros8 = jnp.zeros((SUBLANES, LANES), F32)
    ones8 = jnp.ones((SUBLANES, LANES), F32)
    for c, ln in enumerate(tiles):
        upad[c, 0:SUBLANES, :] = zeros8
        upad[c, SUBLANES:SUBLANES + t, :] = u_ref[:, ln].astype(F32)
        upad[c, SUBLANES + t:2 * SUBLANES + t, :] = zeros8
        cw = cw_ref[:, ln]
        for s in range(SUBLANES):
            acc = cb_ref[:, ln]
            for tap in range(CONV_K):
                r = SUBLANES + s * seg + tap - 1
                acc = acc + cw[tap:tap + 1, :] * upad[c, r:r + seg, :]
            ubuf[c, s * pitch:s * pitch + seg, :] = acc
        for tau in range(seg):
            uint[c, tau * SUBLANES:(tau + 1) * SUBLANES, :] = ubuf[c, pl.ds(tau, SUBLANES, stride=pitch), :]

    row = lax.broadcasted_iota(jnp.int32, (SUBLANES, LANES), 0)
    kdec = [[(0.5 * LRU_C) * _softplus(-lam_ref[d:d + 1, ln]) for d in range(2)] for ln in tiles]
    chains = [(c, d) for c in range(ntile) for d in range(2)]

    def row0(blk):
        return blk * rb if isinstance(blk, int) else pl.multiple_of(blk * rb, rb)

    def blk_body(i, carry):
        carry = list(carry)
        for q, (c, d) in enumerate(chains):
            ln = tiles[c]
            h, p = carry[2 * q], carry[2 * q + 1]
            r0 = row0(i if d == 0 else nblk - 1 - i)
            u = uint[c, pl.ds(r0, rb), :]
            ub = u.astype(BF16)
            nla = kdec[c][d] * jnp.tanh(_dot(ub, wa_ref[d, c]) + ba_ref[d:d + 1, ln]) + kdec[c][d]
            gate = jnp.tanh(_dot(ub, wx_ref[d, c]) + bx_ref[d:d + 1, ln]) + 1.0
            a = jnp.exp2(nla * (-math.log2(math.e)))
            v = jnp.tanh(nla) * (1.0 + a * a)
            root = jnp.where(v > 0.0, v * lax.rsqrt(v), 0.0)
            bb = root * (gate * u)
            hs = [None] * nv
            ps = [None] * nv
            order = range(nv) if d == 0 else range(nv - 1, -1, -1)
            for k in order:
                ak = a[k * SUBLANES:(k + 1) * SUBLANES, :]
                h = ak * h + bb[k * SUBLANES:(k + 1) * SUBLANES, :]
                p = ak * p
                hs[k] = h
                ps[k] = p
            hbuf[q, pl.ds(r0, rb), :] = jnp.concatenate(hs, axis=0)
            pbuf[q, pl.ds(r0, rb), :] = jnp.concatenate(ps, axis=0)
            carry[2 * q], carry[2 * q + 1] = h, p
        return tuple(carry)

    def loop(body, init):
        if straight_line:
            for i in range(nblk):
                init = body(i, init)
            return init
        return lax.fori_loop(0, nblk, body, init)

    hp = loop(blk_body, (zeros8, ones8) * len(chains))
    finals = []
    cmats = []
    for q, (c, d) in enumerate(chains):
        h, p = hp[2 * q], hp[2 * q + 1]
        st = h0[q]
        cmat = zeros8
        order = range(SUBLANES) if d == 0 else range(SUBLANES - 1, -1, -1)
        for s in order:
            cmat = jnp.where(row == s, jnp.broadcast_to(st, (SUBLANES, LANES)), cmat)
            st = h[s:s + 1, :] + p[s:s + 1, :] * st
        finals.append(st)
        cmats.append(cmat)

    if need_out:
        def out_body(i, carry):
            r0 = row0(i)
            for c in range(ntile):
                y = None
                for q in (2 * c, 2 * c + 1):
                    t1 = hbuf[q, pl.ds(r0, rb), :].reshape(nv, SUBLANES, LANES)
                    t2 = pbuf[q, pl.ds(r0, rb), :].reshape(nv, SUBLANES, LANES) * cmats[q][None]
                    y = t1 + t2 if y is None else y + t1 + t2
                ybuf[c, pl.ds(r0, rb), :] = y.reshape(rb, LANES)
            return carry

        loop(out_body, 0)
        for c in range(ntile):
            for tau in range(seg):
                ubuf[c, pl.ds(tau, SUBLANES, stride=pitch), :] = ybuf[c, tau * SUBLANES:(tau + 1) * SUBLANES, :]
            for s in range(SUBLANES):
                write_out(c, s, ubuf[c, s * pitch:s * pitch + seg, :])
    return finals


def _lru_scratch(t, ntile):
    pitch_rows = SUBLANES * (t // SUBLANES + 4) + SUBLANES
    return [
        pltpu.VMEM((ntile, t + 2 * SUBLANES, LANES), F32),
        pltpu.VMEM((ntile, pitch_rows, LANES), F32),
        pltpu.VMEM((ntile, t, LANES), F32),
        pltpu.VMEM((2 * ntile, t, LANES), F32),
        pltpu.VMEM((2 * ntile, t, LANES), F32),
        pltpu.VMEM((ntile, t, LANES), F32),
    ]


def _lru_ctx_kernel(*args, t_ctx, need_ctx):
    if need_ctx:
        (uc_ref, *refs, hf_ref, yc_ref, upad, ubuf, uint, hbuf, pbuf, ybuf) = args
    else:
        (uc_ref, *refs, hf_ref, upad, ubuf, uint, hbuf, pbuf, ybuf) = args
    seg = t_ctx // SUBLANES

    def write_out(c, s, rows):
        yc_ref[0, s * seg:(s + 1) * seg, c * LANES:(c + 1) * LANES] = rows

    zero = jnp.zeros((1, LANES), F32)
    finals = _lru_seq(t_ctx, uc_ref.at[0], write_out if need_ctx else None, (zero,) * (2 * LRU_TILES), refs,
                      (upad, ubuf, uint, hbuf, pbuf, ybuf), ntile=LRU_TILES)
    for q, st in enumerate(finals):
        c, d = divmod(q, 2)
        hf_ref[0, d:d + 1, c * LANES:(c + 1) * LANES] = st


def _lru_ctx(lx_c, conv_w, conv_b, wa_bd, wx_bd, ba, bx, lam, *, need_ctx):
    b, t_ctx, w = lx_c.shape
    cwid = LRU_TILES * LANES
    seq = lambda t: pl.BlockSpec((1, t, cwid), lambda i, j: (i, 0, j))
    par = lambda r: pl.BlockSpec((r, cwid), lambda i, j: (0, j))
    gate = pl.BlockSpec((2, LRU_TILES, LANES, LANES), lambda i, j: (0, j, 0, 0))
    kern = functools.partial(_lru_ctx_kernel, t_ctx=t_ctx, need_ctx=need_ctx)
    out_shape = [jax.ShapeDtypeStruct((b, 2, w), F32)]
    out_specs = [seq(2)]
    if need_ctx:
        out_shape.append(jax.ShapeDtypeStruct((b, t_ctx, w), F32))
        out_specs.append(seq(t_ctx))
    res = pl.pallas_call(
        kern,
        out_shape=tuple(out_shape),
        grid=(b, w // cwid),
        in_specs=[seq(t_ctx), par(CONV_K), par(1), gate, gate, par(2), par(2), par(2)],
        out_specs=tuple(out_specs),
        scratch_shapes=_lru_scratch(t_ctx, LRU_TILES),
        compiler_params=_cparams("arbitrary", "arbitrary"),
        name="rglru_ctx",
    )(lx_c, conv_w, conv_b, wa_bd, wx_bd, ba, bx, lam)
    return (res[0], res[1]) if need_ctx else (res[0], None)


def _ssd_kernel(*args, t_lat, t_ctx, need_ctx):
    if need_ctx:
        (xl_hbm, xc_hbm, cw_ref, cb_ref, dtb_ref, alog_ref, dsk_ref, yl_hbm, yc_ref,
         xg, ybuf, sem_in, sem_out, stg, xm, scc, sbt, cumb, srct, etb, sbuf, hst, yacc) = args
    else:
        (xl_hbm, xc_hbm, cw_ref, cb_ref, dtb_ref, alog_ref, dsk_ref, yl_hbm,
         xg, ybuf, sem_in, sem_out, stg, xm, scc, sbt, cumb, srct, etb, sbuf, hst, yacc) = args
        yc_ref = None
    ck = SSD_CHUNK
    rows = t_lat // GRID_W
    nc_ctx = t_ctx // ck
    nc_lat = t_lat // ck
    nx = SSD_HEADS * SSD_HEAD_DIM // LANES
    ng = SSD_GROUPS
    nslab = nx + 2 * ng
    halo = SUBLANES

    ri = lax.broadcasted_iota(jnp.int32, (ck, ck), 0)
    ci = lax.broadcasted_iota(jnp.int32, (ck, ck), 1)
    lane_lo = ci < SSD_HEAD_DIM

    b = pl.program_id(0)
    nb = pl.num_programs(0)
    slot = lax.rem(b, 2)
    ctx0 = halo
    lat0 = ctx0 + t_ctx + halo
    width = xg.shape[-1]

    def in_copies(bi, sl):
        cps = [pltpu.make_async_copy(xc_hbm.at[bi], xg.at[sl, pl.ds(ctx0, t_ctx), :], sem_in.at[sl])]
        for w in range(GRID_W):
            cps.append(pltpu.make_async_copy(
                xl_hbm.at[bi, :, w, :], xg.at[sl, pl.ds(lat0 + w * rows, rows), :], sem_in.at[sl]))
        return cps

    def out_copies(bi):
        return [pltpu.make_async_copy(ybuf.at[pl.ds(w * rows, rows), :], yl_hbm.at[bi, :, w, :], sem_out.at[0])
                for w in range(GRID_W)]

    @pl.when(b == 0)
    def _():
        zpad = jnp.zeros((halo, width), F32)
        for sl in range(2):
            for r in (0, ctx0 + t_ctx, lat0 + t_lat):
                xg[sl, r:r + halo, :] = zpad
        for cp in in_copies(0, 0):
            cp.start()

    @pl.when(b + 1 < nb)
    def _():
        for cp in in_copies(b + 1, 1 - slot):
            cp.start()

    for cp in in_copies(b, slot):
        cp.wait()

    xs = xg.at[slot]

    def src_row(k):
        return pl.multiple_of(jnp.where(k < nc_ctx, ctx0 + k * ck, lat0 + (k - nc_ctx) * ck), SUBLANES)

    def conv_silu(src0, j):
        stg[j] = xs[pl.ds(src0 - halo, ck + 2 * halo), j * LANES:(j + 1) * LANES]
        cw = cw_ref[j]
        half = cb_ref[j]
        for tap in range(CONV_K):
            half = half + cw[tap:tap + 1, :] * stg[j, halo + tap - 1:halo + tap - 1 + ck, :]
        return half * jnp.tanh(half) + half

    dtb = dtb_ref[...]
    lane_ok = lax.broadcasted_iota(jnp.int32, (1, LANES), 1) < 2 * SSD_HEADS
    nega2 = jnp.where(lane_ok, -jnp.exp(alog_ref[...]) * math.log2(math.e), 0.0)

    nck = nc_ctx + nc_lat
    nh2 = 2 * SSD_HEADS
    hpp = SSD_HEADS // ng // 2
    dir1_col = lax.broadcasted_iota(jnp.int32, (1, LANES), 1) >= SSD_HEADS
    tris = (ri >= ci, ri <= ci)

    def xcat_of(j, r0):
        return jnp.concatenate([xm[j, pl.ds(r0, ck), :], xm[nx + j, pl.ds(r0, ck), :]], axis=0)

    def pass1(kc, carry):
        r0 = pl.multiple_of(kc * ck, ck)
        src0 = src_row(kc)
        dt = _softplus(xs[pl.ds(src0, ck), nslab * LANES:(nslab + 1) * LANES] + dtb)
        la = dt * nega2
        cumf = la
        sh = 1
        while sh < ck:
            if sh < SUBLANES:
                cumf = cumf + jnp.where(ri >= sh, pltpu.roll(cumf, sh, 0), 0.0)
            else:
                cumf = cumf + jnp.concatenate([jnp.zeros((sh, LANES), F32), cumf[0:ck - sh, :]], axis=0)
            sh *= 2
        tot = cumf[ck - 1:ck, :]
        cum = jnp.where(dir1_col, tot - cumf + la, cumf)
        ldt = jnp.log2(dt)
        cumb[pl.ds(r0, ck), :] = cum
        srct[pl.ds(pl.multiple_of(kc * nh2, nh2), nh2), :] = (cum - ldt).T[0:nh2, :]
        wgt_t = jnp.exp2((tot - cum + ldt).T[0:nh2, :])
        etb[pl.ds(pl.multiple_of(kc * SUBLANES, SUBLANES), SUBLANES), :] = jnp.broadcast_to(
            jnp.exp2(tot), (SUBLANES, LANES))
        for g in range(ng):
            scc[g, pl.ds(r0, ck), :] = conv_silu(src0, nx + ng + g).astype(BF16)
            btf = conv_silu(src0, nx + g).T
            sbt[g, pl.ds(r0, ck), :] = btf.astype(BF16)
            for pr in range(hpp):
                j = g * hpp + pr
                x = conv_silu(src0, j)
                yacc[j, pl.ds(r0, ck), :] = dsk_ref[j] * x
                xb = x.astype(BF16)
                zb = jnp.zeros_like(xb)
                xlo = jnp.where(lane_lo, xb, zb)
                xhi = jnp.where(lane_lo, zb, xb)
                xm[j, pl.ds(r0, ck), :] = xlo
                xm[nx + j, pl.ds(r0, ck), :] = xhi
                btw = []
                for d in range(2):
                    hc = (d * SSD_HEADS + 2 * j, d * SSD_HEADS + 2 * j + 1)
                    btw.append(jnp.concatenate([(btf * wgt_t[h:h + 1, :]).astype(BF16) for h in hc], axis=1))
                s = _dot(jnp.concatenate(btw, axis=0), jnp.concatenate([xlo, xhi], axis=0)).astype(BF16)
                sbuf[j, pl.ds(r0, ck), :] = s[0:SSD_STATE, :]
                sbuf[nx + j, pl.ds(r0, ck), :] = s[SSD_STATE:2 * SSD_STATE, :]
        return carry

    lax.fori_loop(0, nck, pass1, 0, unroll=6)

    hst[...] = jnp.zeros(hst.shape, F32)

    def pass2(i, carry):
        for d in range(2):
            if d == 0:
                kc = i
            else:
                kc = jnp.where(i < nc_ctx, nc_ctx - 1 - i, nck - 1 - (i - nc_ctx))
            r0 = pl.multiple_of(kc * ck, ck)
            e = etb[pl.ds(pl.multiple_of(kc * SUBLANES, SUBLANES), 1), :]
            for j in range(nx):
                hc = (d * SSD_HEADS + 2 * j, d * SSD_HEADS + 2 * j + 1)
                et = jnp.where(lane_lo[0:1, :], jnp.broadcast_to(e[:, hc[0]:hc[0] + 1], (1, LANES)),
                               jnp.broadcast_to(e[:, hc[1]:hc[1] + 1], (1, LANES)))
                h = hst[d * nx + j]
                s = sbuf[d * nx + j, pl.ds(r0, ck), :].astype(F32)
                sbuf[d * nx + j, pl.ds(r0, ck), :] = h.astype(BF16)
                hst[d * nx + j] = h * et + s
        return carry

    lax.fori_loop(0, nck, pass2, 0, unroll=3)

    def pass3(kc, is_ctx):
        r0 = kc * ck if isinstance(kc, int) else pl.multiple_of(kc * ck, ck)
        s0 = kc * nh2 if isinstance(kc, int) else pl.multiple_of(kc * nh2, nh2)
        cum = cumb[pl.ds(r0, ck), :]
        src_t = srct[pl.ds(s0, nh2), :]
        for g in range(ng):
            cm = scc[g, pl.ds(r0, ck), :]
            cb = _dot(cm, sbt[g, pl.ds(r0, ck), :])
            for pr in range(hpp):
                j = g * hpp + pr
                dec = []
                ecs = []
                for d in range(2):
                    hc = (d * SSD_HEADS + 2 * j, d * SSD_HEADS + 2 * j + 1)
                    col = [jnp.broadcast_to(cum[:, h:h + 1], (ck, LANES)) for h in hc]
                    dec.append([jnp.exp2(jnp.where(tris[d], col[q] - src_t[hc[q]:hc[q] + 1, :], NEG_BIG))
                                for q in range(2)])
                    ecs.append(jnp.exp2(jnp.where(lane_lo, col[0], col[1])))
                ms = [(cb * (dec[0][q] + dec[1][q])).astype(BF16) for q in range(2)]
                y = yacc[j, pl.ds(r0, ck), :] + _dot(jnp.concatenate(ms, axis=1), xcat_of(j, r0))
                hin = jnp.concatenate([sbuf[j, pl.ds(r0, ck), :], sbuf[nx + j, pl.ds(r0, ck), :]], axis=1)
                yoff = _dot(cm, hin)
                y = y + yoff[:, 0:LANES] * ecs[0] + yoff[:, LANES:2 * LANES] * ecs[1]
                if is_ctx:
                    yc_ref[0, pl.ds(r0, ck), j * LANES:(j + 1) * LANES] = y
                else:
                    y0 = pl.multiple_of((kc - nc_ctx) * ck, ck)
                    ybuf[pl.ds(y0, ck), j * LANES:(j + 1) * LANES] = y

    if need_ctx:
        for kc in range(nc_ctx):
            pass3(kc, True)

    @pl.when(b > 0)
    def _():
        for cp in out_copies(b - 1):
            cp.wait()

    def lat_body(c, carry):
        pass3(nc_ctx + c, False)
        return carry

    lax.fori_loop(0, nc_lat, lat_body, 0, unroll=8)

    for cp in out_copies(b):
        cp.start()

    @pl.when(b == nb - 1)
    def _():
        for cp in out_copies(b):
            cp.wait()


def _ssd(xd_l, xd_c, cw, cb, dtb, alog, dsk, *, need_ctx):
    b, t_lat, width = xd_l.shape
    t_ctx = xd_c.shape[1]
    nx = SSD_HEADS * SSD_HEAD_DIM // LANES
    ng = SSD_GROUPS
    inner = nx * LANES
    tot = t_lat + t_ctx
    rows = t_lat // GRID_W
    kern = functools.partial(_ssd_kernel, t_lat=t_lat, t_ctx=t_ctx, need_ctx=need_ctx)
    full = lambda a: pl.BlockSpec(a.shape, lambda i: (0,) * a.ndim)
    hbm = pl.BlockSpec(memory_space=pl.ANY)
    out_shape = [jax.ShapeDtypeStruct((b, rows, GRID_W, inner), F32)]
    out_specs = [hbm]
    if need_ctx:
        out_shape.append(jax.ShapeDtypeStruct((b, t_ctx, inner), F32))
        out_specs.append(pl.BlockSpec((1, t_ctx, inner), lambda i: (i, 0, 0)))
    res = pl.pallas_call(
        kern,
        out_shape=tuple(out_shape),
        grid=(b,),
        in_specs=[hbm, hbm, full(cw), full(cb), full(dtb), full(alog), full(dsk)],
        out_specs=tuple(out_specs),
        scratch_shapes=[
            pltpu.VMEM((2, tot + 3 * SUBLANES, width), F32),
            pltpu.VMEM((t_lat, inner), F32),
            pltpu.SemaphoreType.DMA((2,)),
            pltpu.SemaphoreType.DMA((1,)),
            pltpu.VMEM((nx + 2 * ng, SSD_CHUNK + 2 * SUBLANES, LANES), F32),
            pltpu.VMEM((2 * nx, tot, LANES), BF16),
            pltpu.VMEM((ng, tot, LANES), BF16),
            pltpu.VMEM((ng, tot, LANES), BF16),
            pltpu.VMEM((tot, LANES), F32),
            pltpu.VMEM((tot // SSD_CHUNK * 2 * SSD_HEADS, LANES), F32),
            pltpu.VMEM((tot // SSD_CHUNK * SUBLANES, LANES), F32),
            pltpu.VMEM((2 * nx, tot, LANES), BF16),
            pltpu.VMEM((2 * nx, SSD_STATE, LANES), F32),
            pltpu.VMEM((nx, tot, LANES), F32),
        ],
        compiler_params=_cparams("arbitrary"),
        name="ssd",
    )(xd_l.reshape(b, rows, GRID_W, width), xd_c, cw, cb, dtb, alog, dsk)
    y_l = res[0].reshape(b, t_lat, inner)
    return (y_l, res[1]) if need_ctx else (y_l, None)


MLP_FF_CHUNK = 1024


def _outmlp_body(lru_raw, refs, fg_ref, o_ref):
    x_ref, lg_ref, ssd_ref, z_ref, mod_ref, sg_ref, n2_ref, wo_ref, w1_ref, w2_ref = refs
    lw = z_ref.shape[-1]
    gw = lw // SSD_GROUPS
    dff = w1_ref.shape[1]
    parts = []
    for g in range(SSD_GROUPS):
        y = ssd_ref[0, :, g * gw:(g + 1) * gw] * _silu(z_ref[0, :, g * gw:(g + 1) * gw].astype(F32))
        y = y * lax.rsqrt(jnp.mean(y * y, axis=-1, keepdims=True) + EPS)
        parts.append(y * sg_ref[:, g * gw:(g + 1) * gw])
    ssd_n = jnp.concatenate(parts, axis=1).astype(BF16)
    lru = (lru_raw * _gelu_tanh(lg_ref[0].astype(F32))).astype(BF16)
    y = _dot(lru, wo_ref[0:lw, :]) + _dot(ssd_n, wo_ref[lw:2 * lw, :])
    x1 = x_ref[0] + mod_ref[0, 2:3, :] * y
    h = (_rmsnorm(x1, n2_ref[...]) * (1.0 + mod_ref[0, 4:5, :]) + mod_ref[0, 3:4, :]).astype(BF16)
    acc = jnp.zeros_like(x1)
    for j in range(dff // MLP_FF_CHUNK):
        a = _dot(h, w1_ref[:, j * MLP_FF_CHUNK:(j + 1) * MLP_FF_CHUNK])
        a = jnp.square(jnp.maximum(a, 0.0)).astype(BF16)
        acc = acc + _dot(a, w2_ref[j * MLP_FF_CHUNK:(j + 1) * MLP_FF_CHUNK, :])
    x2 = x1 + mod_ref[0, 5:6, :] * acc
    if fg_ref is not None:
        x2 = _rmsnorm(x2, fg_ref[...])
    o_ref[0] = x2


def _outmlp_kernel(*args, final_norm):
    x_ref, lru_ref, *rest = args
    if final_norm:
        *refs, fg_ref, o_ref = rest
    else:
        *refs, o_ref = rest
        fg_ref = None
    _outmlp_body(lru_ref[0], (x_ref, *refs), fg_ref, o_ref)


def _outmlp(x, lru, lg, ssd, z, mod, mod_row, sg, n2, wo, w1, w2, layer, fg, *, tm):
    b, t, d = x.shape
    lw = lru.shape[-1]
    final_norm = fg is not None
    kern = functools.partial(_outmlp_kernel, final_norm=final_norm)
    tok = lambda n: pl.BlockSpec((1, tm, n), lambda i, j: (i, j, 0))
    res = lambda a: pl.BlockSpec((None,) + a.shape[1:], lambda i, j: (layer, 0, 0), pipeline_mode=pl.Buffered(1))
    row = lambda n: pl.BlockSpec((1, n), lambda i, j: (0, 0))
    ins = [x, lru, lg, ssd, z, mod, sg, n2, wo, w1, w2]
    in_specs = [tok(d), tok(lw), tok(lw), tok(lw), tok(lw),
                pl.BlockSpec((1, N_MOD, d), lambda i, j: (mod_row(i), 0, 0)),
                row(lw), row(d), res(wo), res(w1), res(w2)]
    if final_norm:
        ins.append(fg)
        in_specs.append(row(d))
    return pl.pallas_call(
        kern,
        out_shape=jax.ShapeDtypeStruct((b, t, d), F32),
        grid=(b, t // tm),
        in_specs=in_specs,
        out_specs=tok(d),
        compiler_params=_cparams("arbitrary", "arbitrary"),
        name="outmlp",
    )(*ins)


def _lru_lat_kernel(lx_ref, h0_ref, *rest, t_lat):
    *refs, y_ref, upad, ubuf, uint, hbuf, pbuf, ybuf = rest
    seg = t_lat // SUBLANES

    def write_out(c, s, rows):
        y_ref[0, s * seg:(s + 1) * seg, c * LANES:(c + 1) * LANES] = rows

    h0 = [h0_ref[0, d:d + 1, c * LANES:(c + 1) * LANES] for c in range(LRU_TILES) for d in range(2)]
    _lru_seq(t_lat, lx_ref.at[0], write_out, h0, refs,
             (upad, ubuf, uint, hbuf, pbuf, ybuf), ntile=LRU_TILES, straight_line=True)


def _lru_lat(lx, h0, conv_w, conv_b, wa_bd, wx_bd, ba, bx, lam):
    b, t, w = lx.shape
    cwid = LRU_TILES * LANES
    seq = lambda r: pl.BlockSpec((1, r, cwid), lambda i, j: (i, 0, j))
    par = lambda r: pl.BlockSpec((r, cwid), lambda i, j: (0, j))
    gate = pl.BlockSpec((2, LRU_TILES, LANES, LANES), lambda i, j: (0, j, 0, 0))
    return pl.pallas_call(
        functools.partial(_lru_lat_kernel, t_lat=t),
        out_shape=jax.ShapeDtypeStruct((b, t, w), F32),
        grid=(b, w // cwid),
        in_specs=[seq(t), seq(2), par(CONV_K), par(1), gate, gate, par(2), par(2), par(2)],
        out_specs=seq(t),
        scratch_shapes=_lru_scratch(t, LRU_TILES),
        compiler_params=_cparams("arbitrary", "arbitrary"),
        name="rglru_lat",
    )(lx, h0, conv_w, conv_b, wa_bd, wx_bd, ba, bx, lam)


def _pair_block_diag(w):
    two, h, k, _ = w.shape
    w = w.reshape(two, h // 2, 2, k, k)
    z = jnp.zeros_like(w[:, :, 0])
    top = jnp.concatenate([w[:, :, 0], z], axis=-1)
    bot = jnp.concatenate([z, w[:, :, 1]], axis=-1)
    return jnp.concatenate([top, bot], axis=-2)


def _pad_lanes(a):
    return jnp.pad(a, [(0, 0)] * (a.ndim - 1) + [(0, LANES - a.shape[-1])])


def kernel(x, c, ctx, c_ctx, ada_w, ada_b, norm1_g, norm2_g, w_in, lru_conv_w, lru_conv_b, lru_wa, lru_ba,
           lru_wx, lru_bx, lru_lambda, ssd_conv_w, ssd_conv_b, ssd_dt_bias, ssd_a_log, ssd_d, ssd_norm_g,
           w_out, mlp_w1, mlp_w2, final_g):
    bsz, t_lat, d = x.shape
    t_ctx = ctx.shape[1]
    depth = ada_w.shape[0]
    lw = lru_conv_w.shape[-1]
    nxbc = ssd_conv_w.shape[-1]
    ndt = 2 * SSD_HEADS
    nscan = lw + nxbc + ndt
    assert bsz + 1 <= MOD_ROWS and t_lat % (GRID_W * SUBLANES) == 0 and t_ctx % SSD_CHUNK == 0

    cvec = jnp.zeros((MOD_ROWS, d), F32).at[:bsz].set(c).at[bsz].set(c_ctx)
    mod = _modulation(cvec, ada_w, ada_b).reshape(depth, MOD_ROWS, N_MOD, d)

    lat_row = lambda i: i
    ctx_row = lambda i: bsz
    n_ctx = bsz * t_ctx
    in_rows = math.gcd(INPROJ_ROWS, n_ctx)
    mlp_rows = math.gcd(MLP_ROWS, n_ctx)

    def ctx_tiles(a, rows):
        return a.reshape(n_ctx // rows, rows, a.shape[-1])

    nslab = nxbc // LANES
    nxd = nxbc + LANES

    w_re = jnp.concatenate(
        [w_in[:, :, :lw + nxbc], _pad_lanes(w_in[:, :, lw + nxbc:nscan]), w_in[:, :, nscan:]], axis=2).astype(BF16)
    wo = w_out.astype(BF16)
    w1 = mlp_w1.astype(BF16)
    w2 = mlp_w2.astype(BF16)

    for l in range(depth):
        need_ctx = l < depth - 1
        g1 = norm1_g[l].reshape(1, d)
        lx_l, xd_l, lg_l, z_l = _inproj(x, mod[l], lat_row, g1, w_re, l, lw=lw, nxd=nxd, tm=INPROJ_ROWS)
        lx_c, xd_c, lg_c, z_c = (a.reshape(bsz, t_ctx, -1) for a in _inproj(
            ctx_tiles(ctx, in_rows), mod[l], ctx_row, g1, w_re, l, lw=lw, nxd=nxd, tm=in_rows))

        lru_par = (0.5 * lru_conv_w[l], 0.5 * lru_conv_b[l].reshape(1, lw),
                   _pair_block_diag(lru_wa[l]).astype(BF16), _pair_block_diag(lru_wx[l]).astype(BF16),
                   0.5 * lru_ba[l], 0.5 * lru_bx[l], lru_lambda[l])
        h0_c, lru_c = _lru_ctx(lx_c, *lru_par, need_ctx=need_ctx)

        cw = 0.5 * ssd_conv_w[l].reshape(CONV_K, nslab, LANES).transpose(1, 0, 2)
        cb = 0.5 * ssd_conv_b[l].reshape(nslab, 1, LANES)
        dtb = _pad_lanes(ssd_dt_bias[l].reshape(1, ndt))
        alog = _pad_lanes(ssd_a_log[l].reshape(1, ndt))
        dsk = jnp.repeat(ssd_d[l], SSD_HEAD_DIM).reshape(lw // LANES, 1, LANES)
        ssd_l, ssd_c = _ssd(xd_l, xd_c, cw, cb, dtb, alog, dsk, need_ctx=need_ctx)

        sg = ssd_norm_g[l].reshape(1, lw)
        n2 = norm2_g[l].reshape(1, d)
        fg = final_g.reshape(1, d) if l == depth - 1 else None
        lru_l = _lru_lat(lx_l, h0_c, *lru_par)
        x = _outmlp(x, lru_l, lg_l, ssd_l, z_l, mod[l], lat_row, sg, n2, wo, w1, w2, l, fg, tm=MLP_ROWS)
        if need_ctx:
            ctx = _outmlp(*(ctx_tiles(a, mlp_rows) for a in (ctx, lru_c, lg_c, ssd_c, z_c)), mod[l], ctx_row,
                          sg, n2, wo, w1, w2, l, None, tm=mlp_rows).reshape(bsz, t_ctx, d)
    return x
```

```python
import functools
import math

import jax
import jax.numpy as jnp
from jax import lax
from jax.experimental import pallas as pl
from jax.experimental.pallas import tpu as pltpu

F32 = jnp.float32
BF16 = jnp.bfloat16

EPS = 1e-6
GRID_W = 64
LRU_C = 8.0
SSD_HEAD_DIM = 64
SSD_HEADS = 8
SSD_GROUPS = 2
SSD_STATE = 128
SSD_CHUNK = 128
CONV_K = 4
N_MOD = 6
LANES = 128
SUBLANES = 8
MOD_ROWS = 16
VMEM_LIMIT = 56 * 1024 * 1024
NEG_BIG = -1e30
MOD_COLS = 1536
INPROJ_ROWS = 1024
MLP_ROWS = 512


def _sigmoid(x):
    return 0.5 * (jnp.tanh(0.5 * x) + 1.0)


def _silu(x):
    return x * _sigmoid(x)


def _softplus(x):
    return jnp.maximum(x, 0.0) + jnp.log1p(jnp.exp(-jnp.abs(x)))


def _gelu_tanh(x):
    c = math.sqrt(2.0 / math.pi)
    return (0.5 * x) * (1.0 + jnp.tanh(x * (c + (c * 0.044715) * (x * x))))


def _rmsnorm(x, g):
    return x * lax.rsqrt(jnp.mean(x * x, axis=-1, keepdims=True) + EPS) * g


def _dot(a, b):
    return jnp.dot(a, b, preferred_element_type=F32)


def _cparams(*sem):
    return pltpu.CompilerParams(dimension_semantics=sem, vmem_limit_bytes=VMEM_LIMIT)


def _mod_kernel(c_ref, w_ref, b_ref, o_ref):
    s = _silu(c_ref[...]).astype(BF16)
    o_ref[0] = _dot(s, w_ref[0].astype(BF16)) + b_ref[0]


def _modulation(cvec, ada_w, ada_b):
    depth, d, n = ada_w.shape
    tn = MOD_COLS
    return pl.pallas_call(
        _mod_kernel,
        out_shape=jax.ShapeDtypeStruct((depth, MOD_ROWS, n), F32),
        grid=(depth, n // tn),
        in_specs=[
            pl.BlockSpec((MOD_ROWS, d), lambda l, j: (0, 0)),
            pl.BlockSpec((1, d, tn), lambda l, j: (l, 0, j)),
            pl.BlockSpec((1, 1, tn), lambda l, j: (l, 0, j)),
        ],
        out_specs=pl.BlockSpec((1, MOD_ROWS, tn), lambda l, j: (l, 0, j)),
        compiler_params=_cparams("arbitrary", "arbitrary"),
        name="modulation",
    )(cvec, ada_w, ada_b.reshape(depth, 1, n))


def _inproj_kernel(x_ref, mod_ref, g_ref, w_ref, lx_ref, xd_ref, lg_ref, z_ref, *, lw, nxd):
    x = x_ref[0]
    h = _rmsnorm(x, g_ref[...]) * (1.0 + mod_ref[0, 1:2, :]) + mod_ref[0, 0:1, :]
    h = h.astype(BF16)
    lx_ref[0] = _dot(h, w_ref[:, 0:lw])
    xd_ref[0] = _dot(h, w_ref[:, lw:lw + nxd])
    o = lw + nxd
    lg_ref[0] = _dot(h, w_ref[:, o:o + lw])
    z_ref[0] = _dot(h, w_ref[:, o + lw:o + 2 * lw])


def _inproj(x, mod, mod_row, g, w, layer, *, lw, nxd, tm):
    b, t, d = x.shape
    kern = functools.partial(_inproj_kernel, lw=lw, nxd=nxd)
    tok = lambda n: pl.BlockSpec((1, tm, n), lambda i, j: (i, j, 0))
    return pl.pallas_call(
        kern,
        out_shape=(
            jax.ShapeDtypeStruct((b, t, lw), F32),
            jax.ShapeDtypeStruct((b, t, nxd), F32),
            jax.ShapeDtypeStruct((b, t, lw), F32),
            jax.ShapeDtypeStruct((b, t, lw), F32),
        ),
        grid=(b, t // tm),
        in_specs=[
            tok(d),
            pl.BlockSpec((1, N_MOD, d), lambda i, j: (mod_row(i), 0, 0)),
            pl.BlockSpec((1, d), lambda i, j: (0, 0)),
            pl.BlockSpec((None,) + w.shape[1:], lambda i, j: (layer, 0, 0), pipeline_mode=pl.Buffered(1)),
        ],
        out_specs=(tok(lw), tok(nxd), tok(lw), tok(lw)),
        compiler_params=_cparams("arbitrary", "arbitrary"),
        name="inproj",
    )(x, mod, g, w)


LRU_ROWS = 256
LRU_TILES = 2


def _lru_seq(t, u_ref, write_out, h0, refs, scr, *, ntile, straight_line=False):
    cw_ref, cb_ref, wa_ref, wx_ref, ba_ref, bx_ref, lam_ref = refs
    upad, ubuf, uint, hbuf, pbuf, ybuf = scr
    seg = t // SUBLANES
    pitch = seg + 4
    nblk = max(t // LRU_ROWS, 1)
    rb = t // nblk
    nv = rb // SUBLANES
    tiles = [slice(c * LANES, (c + 1) * LANES) for c in range(ntile)]
    need_out = write_out is not None

    zeros8 = jnp.zeros((SUBLANES, LANES), F32)
    ones8 = jnp.ones((SUBLANES, LANES), F32)
    for c, ln in enumerate(tiles):
        upad[c, 0:SUBLANES, :] = zeros8
        upad[c, SUBLANES:SUBLANES + t, :] = u_ref[:, ln]
        upad[c, SUBLANES + t:2 * SUBLANES + t, :] = zeros8
        cw = cw_ref[:, ln]
        for s in range(SUBLANES):
            acc = cb_ref[:, ln]
            for tap in range(CONV_K):
                r = SUBLANES + s * seg + tap - 1
                acc = acc + cw[tap:tap + 1, :] * upad[c, r:r + seg, :]
            ubuf[c, s * pitch:s * pitch + seg, :] = acc
        for tau in range(seg):
            uint[c, tau * SUBLANES:(tau + 1) * SUBLANES, :] = ubuf[c, pl.ds(tau, SUBLANES, stride=pitch), :]

    row = lax.broadcasted_iota(jnp.int32, (SUBLANES, LANES), 0)
    kdec = [[(0.5 * LRU_C) * _softplus(-lam_ref[d:d + 1, ln]) for d in range(2)] for ln in tiles]
    chains = [(c, d) for c in range(ntile) for d in range(2)]

    def row0(blk):
        return blk * rb if isinstance(blk, int) else pl.multiple_of(blk * rb, rb)

    def blk_body(i, carry):
        carry = list(carry)
        for q, (c, d) in enumerate(chains):
            ln = tiles[c]
            h, p = carry[2 * q], carry[2 * q + 1]
            r0 = row0(i if d == 0 else nblk - 1 - i)
            u = uint[c, pl.ds(r0, rb), :]
            ub = u.astype(BF16)
            nla = kdec[c][d] * jnp.tanh(_dot(ub, wa_ref[d, c]) + ba_ref[d:d + 1, ln]) + kdec[c][d]
            gate = jnp.tanh(_dot(ub, wx_ref[d, c]) + bx_ref[d:d + 1, ln]) + 1.0
            a = jnp.exp2(nla * (-math.log2(math.e)))
            v = jnp.tanh(nla) * (1.0 + a * a)
            root = jnp.where(v > 0.0, v * lax.rsqrt(v), 0.0)
            bb = root * (gate * u)
            hs = [None] * nv
            ps = [None] * nv
            order = range(nv) if d == 0 else range(nv - 1, -1, -1)
            for k in order:
                ak = a[k * SUBLANES:(k + 1) * SUBLANES, :]
                h = ak * h + bb[k * SUBLANES:(k + 1) * SUBLANES, :]
                p = ak * p
                hs[k] = h
                ps[k] = p
            hbuf[q, pl.ds(r0, rb), :] = jnp.concatenate(hs, axis=0)
            pbuf[q, pl.ds(r0, rb), :] = jnp.concatenate(ps, axis=0)
            carry[2 * q], carry[2 * q + 1] = h, p
        return tuple(carry)

    def loop(body, init):
        if straight_line:
            for i in range(nblk):
                init = body(i, init)
            return init
        return lax.fori_loop(0, nblk, body, init)

    hp = loop(blk_body, (zeros8, ones8) * len(chains))
    finals = []
    cmats = []
    for q, (c, d) in enumerate(chains):
        h, p = hp[2 * q], hp[2 * q + 1]
        st = h0[q]
        cmat = zeros8
        order = range(SUBLANES) if d == 0 else range(SUBLANES - 1, -1, -1)
        for s in order:
            cmat = jnp.where(row == s, jnp.broadcast_to(st, (SUBLANES, LANES)), cmat)
            st = h[s:s + 1, :] + p[s:s + 1, :] * st
        finals.append(st)
        cmats.append(cmat)

    if need_out:
        def out_body(i, carry):
            r0 = row0(i)
            for c in range(ntile):
                y = None
                for q in (2 * c, 2 * c + 1):
                    t1 = hbuf[q, pl.ds(r0, rb), :].reshape(nv, SUBLANES, LANES)
                    t2 = pbuf[q, pl.ds(r0, rb), :].reshape(nv, SUBLANES, LANES) * cmats[q][None]
                    y = t1 + t2 if y is None else y + t1 + t2
                ybuf[c, pl.ds(r0, rb), :] = y.reshape(rb, LANES)
            return carry

        loop(out_body, 0)
        for c in range(ntile):
            for tau in range(seg):
                ubuf[c, pl.ds(tau, SUBLANES, stride=pitch), :] = ybuf[c, tau * SUBLANES:(tau + 1) * SUBLANES, :]
            for s in range(SUBLANES):
                write_out(c, s, ubuf[c, s * pitch:s * pitch + seg, :])
    return finals


def _lru_scratch(t, ntile):
    pitch_rows = SUBLANES * (t // SUBLANES + 4) + SUBLANES
    return [
        pltpu.VMEM((ntile, t + 2 * SUBLANES, LANES), F32),
        pltpu.VMEM((ntile, pitch_rows, LANES), F32),
        pltpu.VMEM((ntile, t, LANES), F32),
        pltpu.VMEM((2 * ntile, t, LANES), F32),
        pltpu.VMEM((2 * ntile, t, LANES), F32),
        pltpu.VMEM((ntile, t, LANES), F32),
    ]


def _lru_ctx_kernel(*args, t_ctx, need_ctx):
    if need_ctx:
        (uc_ref, *refs, hf_ref, yc_ref, upad, ubuf, uint, hbuf, pbuf, ybuf) = args
    else:
        (uc_ref, *refs, hf_ref, upad, ubuf, uint, hbuf, pbuf, ybuf) = args
    seg = t_ctx // SUBLANES

    def write_out(c, s, rows):
        yc_ref[0, s * seg:(s + 1) * seg, c * LANES:(c + 1) * LANES] = rows

    zero = jnp.zeros((1, LANES), F32)
    finals = _lru_seq(t_ctx, uc_ref.at[0], write_out if need_ctx else None, (zero,) * (2 * LRU_TILES), refs,
                      (upad, ubuf, uint, hbuf, pbuf, ybuf), ntile=LRU_TILES)
    for q, st in enumerate(finals):
        c, d = divmod(q, 2)
        hf_ref[0, d:d + 1, c * LANES:(c + 1) * LANES] = st


def _lru_ctx(lx_c, conv_w, conv_b, wa_bd, wx_bd, ba, bx, lam, *, need_ctx):
    b, t_ctx, w = lx_c.shape
    cwid = LRU_TILES * LANES
    seq = lambda t: pl.BlockSpec((1, t, cwid), lambda i, j: (i, 0, j))
    par = lambda r: pl.BlockSpec((r, cwid), lambda i, j: (0, j))
    gate = pl.BlockSpec((2, LRU_TILES, LANES, LANES), lambda i, j: (0, j, 0, 0))
    kern = functools.partial(_lru_ctx_kernel, t_ctx=t_ctx, need_ctx=need_ctx)
    out_shape = [jax.ShapeDtypeStruct((b, 2, w), F32)]
    out_specs = [seq(2)]
    if need_ctx:
        out_shape.append(jax.ShapeDtypeStruct((b, t_ctx, w), F32))
        out_specs.append(seq(t_ctx))
    res = pl.pallas_call(
        kern,
        out_shape=tuple(out_shape),
        grid=(b, w // cwid),
        in_specs=[seq(t_ctx), par(CONV_K), par(1), gate, gate, par(2), par(2), par(2)],
        out_specs=tuple(out_specs),
        scratch_shapes=_lru_scratch(t_ctx, LRU_TILES),
        compiler_params=_cparams("arbitrary", "arbitrary"),
        name="rglru_ctx",
    )(lx_c, conv_w, conv_b, wa_bd, wx_bd, ba, bx, lam)
    return (res[0], res[1]) if need_ctx else (res[0], None)


def _ssd_kernel(*args, t_lat, t_ctx, need_ctx):
    if need_ctx:
        (xl_hbm, xc_hbm, cw_ref, cb_ref, dtb_ref, alog_ref, dsk_ref, yl_hbm, yc_ref,
         xg, ybuf, sem_in, sem_out, stg, xm, scc, sbt, cumb, srct, etb, sbuf, hst, yacc) = args
    else:
        (xl_hbm, xc_hbm, cw_ref, cb_ref, dtb_ref, alog_ref, dsk_ref, yl_hbm,
         xg, ybuf, sem_in, sem_out, stg, xm, scc, sbt, cumb, srct, etb, sbuf, hst, yacc) = args
        yc_ref = None
    ck = SSD_CHUNK
    rows = t_lat // GRID_W
    nc_ctx = t_ctx // ck
    nc_lat = t_lat // ck
    nx = SSD_HEADS * SSD_HEAD_DIM // LANES
    ng = SSD_GROUPS
    nslab = nx + 2 * ng
    halo = SUBLANES

    ri = lax.broadcasted_iota(jnp.int32, (ck, ck), 0)
    ci = lax.broadcasted_iota(jnp.int32, (ck, ck), 1)
    lane_lo = ci < SSD_HEAD_DIM

    b = pl.program_id(0)
    nb = pl.num_programs(0)
    slot = lax.rem(b, 2)
    ctx0 = halo
    lat0 = ctx0 + t_ctx + halo
    width = xg.shape[-1]

    def in_copies(bi, sl):
        cps = [pltpu.make_async_copy(xc_hbm.at[bi], xg.at[sl, pl.ds(ctx0, t_ctx), :], sem_in.at[sl])]
        for w in range(GRID_W):
            cps.append(pltpu.make_async_copy(
                xl_hbm.at[bi, :, w, :], xg.at[sl, pl.ds(lat0 + w * rows, rows), :], sem_in.at[sl]))
        return cps

    def out_copies(bi):
        return [pltpu.make_async_copy(ybuf.at[pl.ds(w * rows, rows), :], yl_hbm.at[bi, :, w, :], sem_out.at[0])
                for w in range(GRID_W)]

    @pl.when(b == 0)
    def _():
        zpad = jnp.zeros((halo, width), F32)
        for sl in range(2):
            for r in (0, ctx0 + t_ctx, lat0 + t_lat):
                xg[sl, r:r + halo, :] = zpad
        for n, cp in enumerate(in_copies(0, 0)):
            cp.start(priority=n % 2)

    @pl.when(b + 1 < nb)
    def _():
        for n, cp in enumerate(in_copies(b + 1, 1 - slot)):
            cp.start(priority=n % 2)

    for cp in in_copies(b, slot):
        cp.wait()

    xs = xg.at[slot]

    def src_row(k):
        return pl.multiple_of(jnp.where(k < nc_ctx, ctx0 + k * ck, lat0 + (k - nc_ctx) * ck), SUBLANES)

    def conv_silu(src0, j):
        stg[j] = xs[pl.ds(src0 - halo, ck + 2 * halo), j * LANES:(j + 1) * LANES]
        cw = cw_ref[j]
        half = cb_ref[j]
        for tap in range(CONV_K):
            half = half + cw[tap:tap + 1, :] * stg[j, halo + tap - 1:halo + tap - 1 + ck, :]
        return half * jnp.tanh(half) + half

    dtb = dtb_ref[...]
    lane_ok = lax.broadcasted_iota(jnp.int32, (1, LANES), 1) < 2 * SSD_HEADS
    nega2 = jnp.where(lane_ok, -jnp.exp(alog_ref[...]) * math.log2(math.e), 0.0)

    nck = nc_ctx + nc_lat
    nh2 = 2 * SSD_HEADS
    hpp = SSD_HEADS // ng // 2
    dir1_col = lax.broadcasted_iota(jnp.int32, (1, LANES), 1) >= SSD_HEADS
    tris = (ri >= ci, ri <= ci)

    def xcat_of(j, r0):
        return jnp.concatenate([xm[j, pl.ds(r0, ck), :], xm[nx + j, pl.ds(r0, ck), :]], axis=0)

    def pass1(kc, carry):
        r0 = pl.multiple_of(kc * ck, ck)
        src0 = src_row(kc)
        dt = _softplus(xs[pl.ds(src0, ck), nslab * LANES:(nslab + 1) * LANES] + dtb)
        la = dt * nega2
        cumf = la
        sh = 1
        while sh < ck:
            if sh < SUBLANES:
                cumf = cumf + jnp.where(ri >= sh, pltpu.roll(cumf, sh, 0), 0.0)
            else:
                cumf = cumf + jnp.concatenate([jnp.zeros((sh, LANES), F32), cumf[0:ck - sh, :]], axis=0)
            sh *= 2
        tot = cumf[ck - 1:ck, :]
        cum = jnp.where(dir1_col, tot - cumf + la, cumf)
        ldt = jnp.log2(dt)
        cumb[pl.ds(r0, ck), :] = cum
        srct[pl.ds(pl.multiple_of(kc * nh2, nh2), nh2), :] = (cum - ldt).T[0:nh2, :]
        wgt_t = jnp.exp2((tot - cum + ldt).T[0:nh2, :])
        etb[pl.ds(pl.multiple_of(kc * SUBLANES, SUBLANES), SUBLANES), :] = jnp.broadcast_to(
            jnp.exp2(tot), (SUBLANES, LANES))
        for g in range(ng):
            scc[g, pl.ds(r0, ck), :] = conv_silu(src0, nx + ng + g).astype(BF16)
            btf = conv_silu(src0, nx + g).T
            sbt[g, pl.ds(r0, ck), :] = btf.astype(BF16)
            for pr in range(hpp):
                j = g * hpp + pr
                x = conv_silu(src0, j)
                yacc[j, pl.ds(r0, ck), :] = dsk_ref[j] * x
                xb = x.astype(BF16)
                zb = jnp.zeros_like(xb)
                xlo = jnp.where(lane_lo, xb, zb)
                xhi = jnp.where(lane_lo, zb, xb)
                xm[j, pl.ds(r0, ck), :] = xlo
                xm[nx + j, pl.ds(r0, ck), :] = xhi
                btw = []
                for d in range(2):
                    hc = (d * SSD_HEADS + 2 * j, d * SSD_HEADS + 2 * j + 1)
                    btw.append(jnp.concatenate([(btf * wgt_t[h:h + 1, :]).astype(BF16) for h in hc], axis=1))
                s = _dot(jnp.concatenate(btw, axis=0), jnp.concatenate([xlo, xhi], axis=0)).astype(BF16)
                sbuf[j, pl.ds(r0, ck), :] = s[0:SSD_STATE, :]
                sbuf[nx + j, pl.ds(r0, ck), :] = s[SSD_STATE:2 * SSD_STATE, :]
        return carry

    lax.fori_loop(0, nck, pass1, 0, unroll=6)

    hst[...] = jnp.zeros(hst.shape, F32)

    def pass2(i, carry):
        for d in range(2):
            if d == 0:
                kc = i
            else:
                kc = jnp.where(i < nc_ctx, nc_ctx - 1 - i, nck - 1 - (i - nc_ctx))
            r0 = pl.multiple_of(kc * ck, ck)
            e = etb[pl.ds(pl.multiple_of(kc * SUBLANES, SUBLANES), 1), :]
            for j in range(nx):
                hc = (d * SSD_HEADS + 2 * j, d * SSD_HEADS + 2 * j + 1)
                et = jnp.where(lane_lo[0:1, :], jnp.broadcast_to(e[:, hc[0]:hc[0] + 1], (1, LANES)),
                               jnp.broadcast_to(e[:, hc[1]:hc[1] + 1], (1, LANES)))
                h = hst[d * nx + j]
                s = sbuf[d * nx + j, pl.ds(r0, ck), :].astype(F32)
                sbuf[d * nx + j, pl.ds(r0, ck), :] = h.astype(BF16)
                hst[d * nx + j] = h * et + s
        return carry

    lax.fori_loop(0, nck, pass2, 0, unroll=3)

    def pass3(kc, is_ctx):
        r0 = kc * ck if isinstance(kc, int) else pl.multiple_of(kc * ck, ck)
        s0 = kc * nh2 if isinstance(kc, int) else pl.multiple_of(kc * nh2, nh2)
        cum = cumb[pl.ds(r0, ck), :]
        src_t = srct[pl.ds(s0, nh2), :]
        for g in range(ng):
            cm = scc[g, pl.ds(r0, ck), :]
            cb = _dot(cm, sbt[g, pl.ds(r0, ck), :])
            for pr in range(hpp):
                j = g * hpp + pr
                dec = []
                ecs = []
                for d in range(2):
                    hc = (d * SSD_HEADS + 2 * j, d * SSD_HEADS + 2 * j + 1)
                    col = [jnp.broadcast_to(cum[:, h:h + 1], (ck, LANES)) for h in hc]
                    dec.append([jnp.exp2(jnp.where(tris[d], col[q] - src_t[hc[q]:hc[q] + 1, :], NEG_BIG))
                                for q in range(2)])
                    ecs.append(jnp.exp2(jnp.where(lane_lo, col[0], col[1])))
                ms = [(cb * (dec[0][q] + dec[1][q])).astype(BF16) for q in range(2)]
                y = yacc[j, pl.ds(r0, ck), :] + _dot(jnp.concatenate(ms, axis=1), xcat_of(j, r0))
                hin = jnp.concatenate([sbuf[j, pl.ds(r0, ck), :], sbuf[nx + j, pl.ds(r0, ck), :]], axis=1)
                yoff = _dot(cm, hin)
                y = y + yoff[:, 0:LANES] * ecs[0] + yoff[:, LANES:2 * LANES] * ecs[1]
                if is_ctx:
                    yc_ref[0, pl.ds(r0, ck), j * LANES:(j + 1) * LANES] = y
                else:
                    y0 = pl.multiple_of((kc - nc_ctx) * ck, ck)
                    ybuf[pl.ds(y0, ck), j * LANES:(j + 1) * LANES] = y

    if need_ctx:
        for kc in range(nc_ctx):
            pass3(kc, True)

    @pl.when(b > 0)
    def _():
        for cp in out_copies(b - 1):
            cp.wait()

    def lat_body(c, carry):
        pass3(nc_ctx + c, False)
        return carry

    lax.fori_loop(0, nc_lat, lat_body, 0, unroll=8)

    for n, cp in enumerate(out_copies(b)):
        cp.start(priority=n % 2)

    @pl.when(b == nb - 1)
    def _():
        for cp in out_copies(b):
            cp.wait()


def _ssd(xd_l, xd_c, cw, cb, dtb, alog, dsk, *, need_ctx):
    b, t_lat, width = xd_l.shape
    t_ctx = xd_c.shape[1]
    nx = SSD_HEADS * SSD_HEAD_DIM // LANES
    ng = SSD_GROUPS
    inner = nx * LANES
    tot = t_lat + t_ctx
    rows = t_lat // GRID_W
    kern = functools.partial(_ssd_kernel, t_lat=t_lat, t_ctx=t_ctx, need_ctx=need_ctx)
    full = lambda a: pl.BlockSpec(a.shape, lambda i: (0,) * a.ndim)
    hbm = pl.BlockSpec(memory_space=pl.ANY)
    out_shape = [jax.ShapeDtypeStruct((b, rows, GRID_W, inner), F32)]
    out_specs = [hbm]
    if need_ctx:
        out_shape.append(jax.ShapeDtypeStruct((b, t_ctx, inner), F32))
        out_specs.append(pl.BlockSpec((1, t_ctx, inner), lambda i: (i, 0, 0)))
    res = pl.pallas_call(
        kern,
        out_shape=tuple(out_shape),
        grid=(b,),
        in_specs=[hbm, hbm, full(cw), full(cb), full(dtb), full(alog), full(dsk)],
        out_specs=tuple(out_specs),
        scratch_shapes=[
            pltpu.VMEM((2, tot + 3 * SUBLANES, width), F32),
            pltpu.VMEM((t_lat, inner), F32),
            pltpu.SemaphoreType.DMA((2,)),
            pltpu.SemaphoreType.DMA((1,)),
            pltpu.VMEM((nx + 2 * ng, SSD_CHUNK + 2 * SUBLANES, LANES), F32),
            pltpu.VMEM((2 * nx, tot, LANES), BF16),
            pltpu.VMEM((ng, tot, LANES), BF16),
            pltpu.VMEM((ng, tot, LANES), BF16),
            pltpu.VMEM((tot, LANES), F32),
            pltpu.VMEM((tot // SSD_CHUNK * 2 * SSD_HEADS, LANES), F32),
            pltpu.VMEM((tot // SSD_CHUNK * SUBLANES, LANES), F32),
            pltpu.VMEM((2 * nx, tot, LANES), BF16),
            pltpu.VMEM((2 * nx, SSD_STATE, LANES), F32),
            pltpu.VMEM((nx, tot, LANES), F32),
        ],
        compiler_params=_cparams("arbitrary"),
        name="ssd",
    )(xd_l.reshape(b, rows, GRID_W, width), xd_c, cw, cb, dtb, alog, dsk)
    y_l = res[0].reshape(b, t_lat, inner)
    return (y_l, res[1]) if need_ctx else (y_l, None)


MLP_FF_CHUNK = 1024


def _outmlp_body(lru_raw, refs, fg_ref, o_ref):
    x_ref, lg_ref, ssd_ref, z_ref, mod_ref, sg_ref, n2_ref, wo_ref, w1_ref, w2_ref = refs
    lw = z_ref.shape[-1]
    gw = lw // SSD_GROUPS
    dff = w1_ref.shape[1]
    parts = []
    for g in range(SSD_GROUPS):
        y = ssd_ref[0, :, g * gw:(g + 1) * gw] * _silu(z_ref[0, :, g * gw:(g + 1) * gw])
        y = y * lax.rsqrt(jnp.mean(y * y, axis=-1, keepdims=True) + EPS)
        parts.append(y * sg_ref[:, g * gw:(g + 1) * gw])
    ssd_n = jnp.concatenate(parts, axis=1).astype(BF16)
    lru = (lru_raw * _gelu_tanh(lg_ref[0])).astype(BF16)
    y = _dot(lru, wo_ref[0:lw, :]) + _dot(ssd_n, wo_ref[lw:2 * lw, :])
    x1 = x_ref[0] + mod_ref[0, 2:3, :] * y
    h = (_rmsnorm(x1, n2_ref[...]) * (1.0 + mod_ref[0, 4:5, :]) + mod_ref[0, 3:4, :]).astype(BF16)
    acc = jnp.zeros_like(x1)
    for j in range(dff // MLP_FF_CHUNK):
        a = _dot(h, w1_ref[:, j * MLP_FF_CHUNK:(j + 1) * MLP_FF_CHUNK])
        a = jnp.square(jnp.maximum(a, 0.0)).astype(BF16)
        acc = acc + _dot(a, w2_ref[j * MLP_FF_CHUNK:(j + 1) * MLP_FF_CHUNK, :])
    x2 = x1 + mod_ref[0, 5:6, :] * acc
    if fg_ref is not None:
        x2 = _rmsnorm(x2, fg_ref[...])
    o_ref[0] = x2


def _outmlp_kernel(*args, final_norm):
    x_ref, lru_ref, *rest = args
    if final_norm:
        *refs, fg_ref, o_ref = rest
    else:
        *refs, o_ref = rest
        fg_ref = None
    _outmlp_body(lru_ref[0], (x_ref, *refs), fg_ref, o_ref)


def _outmlp(x, lru, lg, ssd, z, mod, mod_row, sg, n2, wo, w1, w2, layer, fg, *, tm):
    b, t, d = x.shape
    lw = lru.shape[-1]
    final_norm = fg is not None
    kern = functools.partial(_outmlp_kernel, final_norm=final_norm)
    tok = lambda n: pl.BlockSpec((1, tm, n), lambda i, j: (i, j, 0))
    res = lambda a: pl.BlockSpec((None,) + a.shape[1:], lambda i, j: (layer, 0, 0), pipeline_mode=pl.Buffered(1))
    row = lambda n: pl.BlockSpec((1, n), lambda i, j: (0, 0))
    ins = [x, lru, lg, ssd, z, mod, sg, n2, wo, w1, w2]
    in_specs = [tok(d), tok(lw), tok(lw), tok(lw), tok(lw),
                pl.BlockSpec((1, N_MOD, d), lambda i, j: (mod_row(i), 0, 0)),
                row(lw), row(d), res(wo), res(w1), res(w2)]
    if final_norm:
        ins.append(fg)
        in_specs.append(row(d))
    return pl.pallas_call(
        kern,
        out_shape=jax.ShapeDtypeStruct((b, t, d), F32),
        grid=(b, t // tm),
        in_specs=in_specs,
        out_specs=tok(d),
        compiler_params=_cparams("arbitrary", "arbitrary"),
        name="outmlp",
    )(*ins)


def _lru_lat_kernel(lx_ref, h0_ref, *rest, t_lat):
    *refs, y_ref, upad, ubuf, uint, hbuf, pbuf, ybuf = rest
    seg = t_lat // SUBLANES

    def write_out(c, s, rows):
        y_ref[0, s * seg:(s + 1) * seg, c * LANES:(c + 1) * LANES] = rows

    h0 = [h0_ref[0, d:d + 1, c * LANES:(c + 1) * LANES] for c in range(LRU_TILES) for d in range(2)]
    _lru_seq(t_lat, lx_ref.at[0], write_out, h0, refs,
             (upad, ubuf, uint, hbuf, pbuf, ybuf), ntile=LRU_TILES, straight_line=True)


def _lru_lat(lx, h0, conv_w, conv_b, wa_bd, wx_bd, ba, bx, lam):
    b, t, w = lx.shape
    cwid = LRU_TILES * LANES
    seq = lambda r: pl.BlockSpec((1, r, cwid), lambda i, j: (i, 0, j))
    par = lambda r: pl.BlockSpec((r, cwid), lambda i, j: (0, j))
    gate = pl.BlockSpec((2, LRU_TILES, LANES, LANES), lambda i, j: (0, j, 0, 0))
    return pl.pallas_call(
        functools.partial(_lru_lat_kernel, t_lat=t),
        out_shape=jax.ShapeDtypeStruct((b, t, w), F32),
        grid=(b, w // cwid),
        in_specs=[seq(t), seq(2), par(CONV_K), par(1), gate, gate, par(2), par(2), par(2)],
        out_specs=seq(t),
        scratch_shapes=_lru_scratch(t, LRU_TILES),
        compiler_params=_cparams("arbitrary", "arbitrary"),
        name="rglru_lat",
    )(lx, h0, conv_w, conv_b, wa_bd, wx_bd, ba, bx, lam)


def _pair_block_diag(w):
    two, h, k, _ = w.shape
    w = w.reshape(two, h // 2, 2, k, k)
    z = jnp.zeros_like(w[:, :, 0])
    top = jnp.concatenate([w[:, :, 0], z], axis=-1)
    bot = jnp.concatenate([z, w[:, :, 1]], axis=-1)
    return jnp.concatenate([top, bot], axis=-2)


def _pad_lanes(a):
    return jnp.pad(a, [(0, 0)] * (a.ndim - 1) + [(0, LANES - a.shape[-1])])


def kernel(x, c, ctx, c_ctx, ada_w, ada_b, norm1_g, norm2_g, w_in, lru_conv_w, lru_conv_b, lru_wa, lru_ba,
           lru_wx, lru_bx, lru_lambda, ssd_conv_w, ssd_conv_b, ssd_dt_bias, ssd_a_log, ssd_d, ssd_norm_g,
           w_out, mlp_w1, mlp_w2, final_g):
    bsz, t_lat, d = x.shape
    t_ctx = ctx.shape[1]
    depth = ada_w.shape[0]
    lw = lru_conv_w.shape[-1]
    nxbc = ssd_conv_w.shape[-1]
    ndt = 2 * SSD_HEADS
    nscan = lw + nxbc + ndt
    assert bsz + 1 <= MOD_ROWS and t_lat % (GRID_W * SUBLANES) == 0 and t_ctx % SSD_CHUNK == 0

    cvec = jnp.zeros((MOD_ROWS, d), F32).at[:bsz].set(c).at[bsz].set(c_ctx)
    mod = _modulation(cvec, ada_w, ada_b).reshape(depth, MOD_ROWS, N_MOD, d)

    lat_row = lambda i: i
    ctx_row = lambda i: bsz
    n_ctx = bsz * t_ctx
    in_rows = math.gcd(INPROJ_ROWS, n_ctx)
    mlp_rows = math.gcd(MLP_ROWS, n_ctx)

    def ctx_tiles(a, rows):
        return a.reshape(n_ctx // rows, rows, a.shape[-1])

    nslab = nxbc // LANES
    nxd = nxbc + LANES

    w_re = jnp.concatenate(
        [w_in[:, :, :lw + nxbc], _pad_lanes(w_in[:, :, lw + nxbc:nscan]), w_in[:, :, nscan:]], axis=2).astype(BF16)
    wo = w_out.astype(BF16)
    w1 = mlp_w1.astype(BF16)
    w2 = mlp_w2.astype(BF16)

    for l in range(depth):
        need_ctx = l < depth - 1
        g1 = norm1_g[l].reshape(1, d)
        lx_l, xd_l, lg_l, z_l = _inproj(x, mod[l], lat_row, g1, w_re, l, lw=lw, nxd=nxd, tm=INPROJ_ROWS)
        lx_c, xd_c, lg_c, z_c = (a.reshape(bsz, t_ctx, -1) for a in _inproj(
            ctx_tiles(ctx, in_rows), mod[l], ctx_row, g1, w_re, l, lw=lw, nxd=nxd, tm=in_rows))

        lru_par = (0.5 * lru_conv_w[l], 0.5 * lru_conv_b[l].reshape(1, lw),
                   _pair_block_diag(lru_wa[l]).astype(BF16), _pair_block_diag(lru_wx[l]).astype(BF16),
                   0.5 * lru_ba[l], 0.5 * lru_bx[l], lru_lambda[l])
        h0_c, lru_c = _lru_ctx(lx_c, *lru_par, need_ctx=need_ctx)

        cw = 0.5 * ssd_conv_w[l].reshape(CONV_K, nslab, LANES).transpose(1, 0, 2)
        cb = 0.5 * ssd_conv_b[l].reshape(nslab, 1, LANES)
        dtb = _pad_lanes(ssd_dt_bias[l].reshape(1, ndt))
        alog = _pad_lanes(ssd_a_log[l].reshape(1, ndt))
        dsk = jnp.repeat(ssd_d[l], SSD_HEAD_DIM).reshape(lw // LANES, 1, LANES)
        ssd_l, ssd_c = _ssd(xd_l, xd_c, cw, cb, dtb, alog, dsk, need_ctx=need_ctx)

        sg = ssd_norm_g[l].reshape(1, lw)
        n2 = norm2_g[l].reshape(1, d)
        fg = final_g.reshape(1, d) if l == depth - 1 else None
        lru_l = _lru_lat(lx_l, h0_c, *lru_par)
        x = _outmlp(x, lru_l, lg_l, ssd_l, z_l, mod[l], lat_row, sg, n2, wo, w1, w2, l, fg, tm=MLP_ROWS)
        if need_ctx:
            ctx = _outmlp(*(ctx_tiles(a, mlp_rows) for a in (ctx, lru_c, lg_c, ssd_c, z_c)), mod[l], ctx_row,
                          sg, n2, wo, w1, w2, l, None, tm=mlp_rows).reshape(bsz, t_ctx, d)
    return x
```
